```python
import math
import jax
import jax.numpy as jnp
from jax import lax
import numpy as np

D_MODEL = 4096
BATCH = 2
SEQ = 4096
DEPTH = 2

N_BRANCH = 4
BRANCH_WIDTH = D_MODEL // 4
RWKV_WIDTH = BRANCH_WIDTH
RWKV_HEAD = 64
RWKV_HEADS = RWKV_WIDTH // RWKV_HEAD
RWKV_DECAY_RANK = 64
RWKV_ICLR_RANK = 64
RWKV_GATE_RANK = 160
RWKV_GN_EPS = 64e-5
ATT_HEAD = 128
ATT_GROUPS = ((128, 1), (512, 4), (2048, 16))
ATT_HEADS_PER_GROUP = BRANCH_WIDTH // ATT_HEAD
ATT_BLOCK = 128
POOL_WINDOWS = (2, 4, 8, 16)
POOL_GROUP = BRANCH_WIDTH // len(POOL_WINDOWS)
SSM_INNER = BRANCH_WIDTH
SSM_HEAD = 64
SSM_HEADS = SSM_INNER // SSM_HEAD
SSM_GROUPS = 4
SSM_HEADS_PER_GROUP = SSM_HEADS // SSM_GROUPS
SSM_STATE = 128
SSM_CONV = 4
SSM_CHUNK = 128
SSM_CONV_DIM = SSM_INNER + 2 * SSM_GROUPS * SSM_STATE
GATE_RANK = 256
MOE_GROUPS = 4
MOE_EXPERTS_PER_GROUP = 8
MOE_EXPERTS = MOE_GROUPS * MOE_EXPERTS_PER_GROUP
MOE_TOP_K = 2
MOE_FF = 512
MOE_BLOCK = 128
NORM_EPS = 1e-6
A_COLS = 3 * RWKV_WIDTH + RWKV_DECAY_RANK + RWKV_ICLR_RANK + RWKV_GATE_RANK
B_COLS = 3 * len(ATT_GROUPS) * ATT_HEADS_PER_GROUP * ATT_HEAD
C_COLS = BRANCH_WIDTH
D_COLS = SSM_INNER + SSM_CONV_DIM + SSM_HEADS
IN_COLS = A_COLS + B_COLS + C_COLS + D_COLS + GATE_RANK
IN_OFFSETS = (0, A_COLS, A_COLS + B_COLS, A_COLS + B_COLS + C_COLS, A_COLS + B_COLS + C_COLS + D_COLS, IN_COLS)
RWKV_SPLITS = (RWKV_WIDTH, 2 * RWKV_WIDTH, 3 * RWKV_WIDTH, 3 * RWKV_WIDTH + RWKV_DECAY_RANK,
               3 * RWKV_WIDTH + RWKV_DECAY_RANK + RWKV_ICLR_RANK)

kernel_name = 'hybrid_rwkv7_dilattn_pool_ssd_hmoe_block'


def rmsnorm(x, g):
    xf = x.astype(jnp.float32)
    y = xf * lax.rsqrt(jnp.mean(xf * xf, axis=-1, keepdims=True) + NORM_EPS)
    return (y * g.astype(jnp.float32)).astype(x.dtype)


def shift_one(t):
    return jnp.pad(t, ((0, 0), (1, 0), (0, 0)))[:, :-1]


def rwkv7_time_mix(p, mu, w0, w2, a0, a2, g2, k_k, k_a, r_k, lnx_w, lnx_b):
    bsz, seq, _ = p.shape
    p = p.astype(jnp.float32)
    p = p + (shift_one(p) - p) * mu
    r, k, v, xw, xa, xg = jnp.split(p, RWKV_SPLITS, axis=-1)
    w = -jax.nn.softplus(-(w0 + jnp.tanh(xw) @ w2)) - 0.5
    decay = jnp.exp(-jnp.exp(w))
    a = jax.nn.sigmoid(a0 + xa @ a2)
    g = jax.nn.sigmoid(xg) @ g2
    kk = k * k_k
    k = k * (1.0 + (a - 1.0) * k_a)
    heads = lambda t: t.reshape(bsz, seq, RWKV_HEADS, RWKV_HEAD)
    r, k, v, kk, a, decay = (heads(t) for t in (r, k, v, kk, a, decay))
    kk = kk / jnp.maximum(jnp.linalg.norm(kk, axis=-1, keepdims=True), 1e-12)

    def step(state, inp):
        r_t, w_t, k_t, v_t, kk_t, a_t = inp
        sa = jnp.einsum('bhvk,bhk->bhv', state, -kk_t)
        state = (state * w_t[:, :, None, :] + sa[..., None] * (kk_t * a_t)[:, :, None, :]
                 + v_t[..., None] * k_t[:, :, None, :])
        return state, jnp.einsum('bhvk,bhk->bhv', state, r_t)

    xs = tuple(jnp.swapaxes(t, 0, 1) for t in (r, decay, k, v, kk, a))
    state0 = jnp.zeros((bsz, RWKV_HEADS, RWKV_HEAD, RWKV_HEAD), jnp.float32)
    _, y = lax.scan(step, state0, xs)
    y = jnp.swapaxes(y, 0, 1)
    mean = jnp.mean(y, axis=-1, keepdims=True)
    var = jnp.mean(jnp.square(y - mean), axis=-1, keepdims=True)
    y = ((y - mean) * lax.rsqrt(var + RWKV_GN_EPS)).reshape(bsz, seq, RWKV_WIDTH) * lnx_w + lnx_b
    bonus = jnp.sum(r * k * r_k, axis=-1, keepdims=True) * v
    return (y + bonus.reshape(bsz, seq, RWKV_WIDTH)) * g


def dilated_window_attention(q, k, v, window, dilation):
    bsz, seq, nh, dh = q.shape
    steps = window // dilation
    n_phase = seq // dilation
    n_blk = -(-n_phase // ATT_BLOCK)
    pad = n_blk * ATT_BLOCK - n_phase

    def to_blocks(t):
        t = t.reshape(bsz, n_phase, dilation, nh, dh).transpose(0, 2, 1, 3, 4)
        t = jnp.pad(t, ((0, 0), (0, 0), (0, pad), (0, 0), (0, 0)))
        return t.reshape(bsz, dilation, n_blk, ATT_BLOCK, nh, dh)

    def with_prev(t):
        prev = jnp.pad(t, ((0, 0), (0, 0), (1, 0), (0, 0), (0, 0), (0, 0)))[:, :, :-1]
        return jnp.concatenate([prev, t], axis=3)

    qb = to_blocks(q)
    kb = with_prev(to_blocks(k))
    vb = with_prev(to_blocks(v))
    qi = np.arange(ATT_BLOCK)[:, None]
    ki = np.arange(2 * ATT_BLOCK)[None, :]
    dist = ATT_BLOCK + qi - ki
    band = (dist >= 0) & (dist <= steps)
    has_prev = (np.arange(n_blk)[:, None, None] > 0) | (ki[None] >= ATT_BLOCK)
    mask = jnp.asarray(band[None] & has_prev)
    s = jnp.einsum('bpnqhd,bpnkhd->bpnhqk', qb, kb, preferred_element_type=jnp.float32) * (dh ** -0.5)
    s = jnp.where(mask[None, None, :, None], s, -jnp.inf)
    m = jnp.max(s, axis=-1, keepdims=True)
    e = jnp.exp(s - m)
    den = jnp.sum(e, axis=-1, keepdims=True)
    o = jnp.einsum('bpnhqk,bpnkhd->bpnqhd', e, vb.astype(jnp.float32)) / jnp.swapaxes(den, 3, 4)
    lse = jnp.swapaxes((m + jnp.log(den))[..., 0], 3, 4)

    def from_blocks(t):
        tail = t.shape[4:]
        t = t.reshape((bsz, dilation, n_blk * ATT_BLOCK) + tail)[:, :, :n_phase]
        return jnp.swapaxes(t, 1, 2).reshape((bsz, seq) + tail)

    return from_blocks(o), from_blocks(lse)


def dilated_attention_mixer(p):
    bsz, seq, _ = p.shape
    qkv = p.reshape(bsz, seq, 3, len(ATT_GROUPS), ATT_HEADS_PER_GROUP, ATT_HEAD)
    outs, lses = [], []
    for gi, (window, dilation) in enumerate(ATT_GROUPS):
        o, lse = dilated_window_attention(qkv[:, :, 0, gi], qkv[:, :, 1, gi], qkv[:, :, 2, gi], window, dilation)
        outs.append(o)
        lses.append(lse)
    alpha = jax.nn.softmax(jnp.stack(lses), axis=0)
    o = jnp.sum(alpha[..., None] * jnp.stack(outs), axis=0)
    return o.reshape(bsz, seq, BRANCH_WIDTH)


def multiscale_pool_mixer(p, w_pool, scale):
    bsz, seq, _ = p.shape
    pf = p.astype(jnp.float32).reshape(bsz, seq, len(POOL_WINDOWS), POOL_GROUP)
    csum = jnp.cumsum(pf, axis=1)
    pos = jnp.arange(seq)
    means = []
    for gi, w in enumerate(POOL_WINDOWS):
        cg = csum[:, :, gi]
        lag = jnp.pad(cg, ((0, 0), (w, 0), (0, 0)))[:, :seq]
        cnt = jnp.minimum(pos + 1, w).astype(jnp.float32)
        means.append((cg - lag) / cnt[None, :, None])
    mixed = jnp.stack(means, axis=2) - pf
    y = jnp.einsum('bsgc,gcd->bsgd', mixed, w_pool.astype(jnp.float32))
    return y.reshape(bsz, seq, BRANCH_WIDTH) * scale


def ssd_chunked(x, a, b, c):
    bsz, seq, ng, ne, hp = x.shape
    ns = b.shape[-1]
    nc = seq // SSM_CHUNK
    x = x.reshape(bsz, nc, SSM_CHUNK, ng, ne, hp)
    a = a.reshape(bsz, nc, SSM_CHUNK, ng, ne).transpose(0, 1, 3, 4, 2)
    b = b.reshape(bsz, nc, SSM_CHUNK, ng, ns)
    c = c.reshape(bsz, nc, SSM_CHUNK, ng, ns)
    a_cum = jnp.cumsum(a, axis=-1)
    seg = a_cum[..., :, None] - a_cum[..., None, :]
    causal = jnp.asarray(np.tril(np.ones((SSM_CHUNK, SSM_CHUNK), dtype=bool)))
    lmat = jnp.exp(jnp.where(causal, seg, -jnp.inf))
    cb = jnp.einsum('bclgn,bcsgn->bcgls', c, b)
    y_diag = jnp.einsum('bcgls,bcgels,bcsgep->bclgep', cb, lmat, x)
    decay_states = jnp.exp(a_cum[..., -1:] - a_cum)
    states = jnp.einsum('bclgn,bcgel,bclgep->bcgepn', b, decay_states, x)
    chunk_decay = jnp.exp(a_cum[..., -1])

    def step(h, inp):
        st, dec = inp
        return h * dec[..., None, None] + st, h

    h0 = jnp.zeros((bsz, ng, ne, hp, ns), jnp.float32)
    _, states_in = lax.scan(step, h0, (jnp.moveaxis(states, 1, 0), jnp.moveaxis(chunk_decay, 1, 0)))
    states_in = jnp.moveaxis(states_in, 0, 1)
    y_off = jnp.einsum('bclgn,bcgepn,bcgel->bclgep', c, states_in, jnp.exp(a_cum))
    return (y_diag + y_off).reshape(bsz, seq, ng, ne, hp)


def mamba2_mixer(p, conv_w, conv_b, dt_bias, a_log, d_skip, norm_w):
    bsz, seq, _ = p.shape
    z, xbc, dt = jnp.split(p, (SSM_INNER, SSM_INNER + SSM_CONV_DIM), axis=-1)
    xbc = lax.conv_general_dilated(xbc, conv_w[:, None, :], window_strides=(1,),
                                   padding=((SSM_CONV - 1, 0),), dimension_numbers=('NWC', 'WIO', 'NWC'),
                                   feature_group_count=SSM_CONV_DIM)
    xbc = jax.nn.silu((xbc + conv_b).astype(jnp.float32))
    xs, bm, cm = jnp.split(xbc, (SSM_INNER, SSM_INNER + SSM_GROUPS * SSM_STATE), axis=-1)
    xs = xs.reshape(bsz, seq, SSM_GROUPS, SSM_HEADS_PER_GROUP, SSM_HEAD)
    bm = bm.reshape(bsz, seq, SSM_GROUPS, SSM_STATE)
    cm = cm.reshape(bsz, seq, SSM_GROUPS, SSM_STATE)
    dt = jax.nn.softplus(dt.astype(jnp.float32) + dt_bias).reshape(bsz, seq, SSM_GROUPS, SSM_HEADS_PER_GROUP)
    a_head = -jnp.exp(a_log.astype(jnp.float32)).reshape(SSM_GROUPS, SSM_HEADS_PER_GROUP)
    y = ssd_chunked(xs * dt[..., None], dt * a_head, bm, cm)
    y = y + xs * d_skip.reshape(SSM_GROUPS, SSM_HEADS_PER_GROUP)[..., None]
    gsize = SSM_INNER // SSM_GROUPS
    y = y.reshape(bsz, seq, SSM_GROUPS, gsize) * jax.nn.silu(z.astype(jnp.float32)).reshape(bsz, seq, SSM_GROUPS, gsize)
    y = y * lax.rsqrt(jnp.mean(y * y, axis=-1, keepdims=True) + NORM_EPS)
    return y.reshape(bsz, seq, SSM_INNER) * norm_w


def hierarchical_moe(h, rg_w, rg_b, re_w, re_b, w_gate, w_up, w_down):
    bsz, seq, dm = h.shape
    t = h.reshape(bsz * seq, dm)
    n_tok = t.shape[0]
    g_logits = (t @ rg_w + rg_b).astype(jnp.float32)
    g_sel = jnp.argmax(g_logits, axis=-1)
    g_prob = jnp.take_along_axis(jax.nn.softmax(g_logits, axis=-1), g_sel[:, None], axis=-1)
    e_logits = (t @ re_w + re_b).astype(jnp.float32).reshape(n_tok, MOE_GROUPS, MOE_EXPERTS_PER_GROUP)
    e_in = jnp.take_along_axis(e_logits, g_sel[:, None, None], axis=1)[:, 0]
    top_v, top_i = lax.top_k(e_in, MOE_TOP_K)
    weight = jax.nn.softmax(top_v, axis=-1) * g_prob
    expert = g_sel[:, None] * MOE_EXPERTS_PER_GROUP + top_i
    n_assign = n_tok * MOE_TOP_K
    flat_e = expert.reshape(-1)
    order = jnp.argsort(flat_e)
    sorted_e = flat_e[order]
    counts = jnp.bincount(flat_e, length=MOE_EXPERTS)
    padded = (counts + MOE_BLOCK - 1) // MOE_BLOCK * MOE_BLOCK
    pad_end = jnp.cumsum(padded)
    pad_start = pad_end - padded
    start = jnp.cumsum(counts) - counts
    dest = pad_start[sorted_e] + jnp.arange(n_assign) - start[sorted_e]
    n_blocks = -(-n_assign // MOE_BLOCK) + MOE_EXPERTS
    n_slots = n_blocks * MOE_BLOCK
    tok = jnp.repeat(jnp.arange(n_tok, dtype=jnp.int32), MOE_TOP_K)
    slot_tok = jnp.full((n_slots,), n_tok, jnp.int32).at[dest].set(tok[order])
    slot_w = jnp.zeros((n_slots,), jnp.float32).at[dest].set(weight.reshape(-1)[order])
    block_e = jnp.minimum(jnp.searchsorted(pad_end, jnp.arange(n_blocks) * MOE_BLOCK, side='right'),
                          MOE_EXPERTS - 1)
    t_pad = jnp.concatenate([t, jnp.zeros((1, dm), t.dtype)], axis=0)
    xb = t_pad[slot_tok].reshape(n_blocks, MOE_BLOCK, dm)

    def expert_block(args):
        xblk, e = args
        return (jax.nn.silu(xblk @ w_gate[e]) * (xblk @ w_up[e])) @ w_down[e]

    yb = lax.map(expert_block, (xb, block_e)).reshape(n_slots, dm)
    out = jnp.zeros((n_tok + 1, dm), jnp.float32).at[slot_tok].add(yb.astype(jnp.float32) * slot_w[:, None])
    return out[:n_tok].reshape(bsz, seq, dm).astype(h.dtype)


def setup_inputs(seed: int = 0) -> dict:
    key = jax.random.key(seed)
    ks = iter(jax.random.split(key, 64))
    f32 = jnp.float32
    L, D = DEPTH, D_MODEL

    def nrm(shape, scale):
        return jax.random.normal(next(ks), shape, f32) * scale

    def unif(shape, lo, hi):
        return jax.random.uniform(next(ks), shape, f32, lo, hi)

    x = nrm((BATCH, SEQ, D), 1.0)
    c = nrm((BATCH, D), 1.0)
    ada_w = nrm((D, 6 * D), 0.5 * D ** -0.5)
    ada_b = nrm((6 * D,), 0.02)
    ada_table = nrm((L, 6 * D), 0.1)
    norm1_g = 1.0 + nrm((L, D), 0.02)
    norm2_g = 1.0 + nrm((L, D), 0.02)
    w_in = nrm((L, D, IN_COLS), D ** -0.5)
    rwkv_mu = unif((L, A_COLS), 0.0, 1.0)
    rwkv_w0 = nrm((L, RWKV_WIDTH), 0.5)
    rwkv_w2 = nrm((L, RWKV_DECAY_RANK, RWKV_WIDTH), RWKV_DECAY_RANK ** -0.5)
    rwkv_a0 = nrm((L, RWKV_WIDTH), 0.1)
    rwkv_a2 = nrm((L, RWKV_ICLR_RANK, RWKV_WIDTH), RWKV_ICLR_RANK ** -0.5)
    rwkv_g2 = nrm((L, RWKV_GATE_RANK, RWKV_WIDTH), RWKV_GATE_RANK ** -0.5)
    rwkv_kk = 0.85 + nrm((L, RWKV_WIDTH), 0.05)
    rwkv_ka = 1.0 + nrm((L, RWKV_WIDTH), 0.05)
    rwkv_rk = nrm((L, RWKV_HEADS, RWKV_HEAD), 0.1)
    rwkv_lnx_w = 1.0 + nrm((L, RWKV_WIDTH), 0.02)
    rwkv_lnx_b = nrm((L, RWKV_WIDTH), 0.02)
    pool_w = nrm((L, len(POOL_WINDOWS), POOL_GROUP, POOL_GROUP), POOL_GROUP ** -0.5)
    pool_scale = 1.0 + nrm((L, BRANCH_WIDTH), 0.02)
    ssm_conv_w = nrm((L, SSM_CONV, SSM_CONV_DIM), SSM_CONV ** -0.5)
    ssm_conv_b = nrm((L, SSM_CONV_DIM), 0.02)
    dt0 = jnp.exp(unif((L, SSM_HEADS), math.log(1e-3), math.log(1e-1)))
    ssm_dt_bias = dt0 + jnp.log(-jnp.expm1(-dt0))
    ssm_a_log = jnp.log(unif((L, SSM_HEADS), 1.0, 16.0))
    ssm_d = 1.0 + nrm((L, SSM_HEADS), 0.02)
    ssm_norm_w = 1.0 + nrm((L, SSM_INNER), 0.02)
    gate_up = nrm((L, N_BRANCH, GATE_RANK, D), GATE_RANK ** -0.5)
    gate_b = nrm((L, N_BRANCH, D), 0.02)
    w_branch = nrm((L, N_BRANCH, BRANCH_WIDTH, D), BRANCH_WIDTH ** -0.5)
    w_out = nrm((L, D, D), D ** -0.5)
    router_group_w = nrm((L, D, MOE_GROUPS), D ** -0.5)
    router_group_b = nrm((L, MOE_GROUPS), 0.01)
    router_expert_w = nrm((L, D, MOE_EXPERTS), D ** -0.5)
    router_expert_b = nrm((L, MOE_EXPERTS), 0.01)
    exp_w_gate = nrm((L, MOE_EXPERTS, D, MOE_FF), D ** -0.5)
    exp_w_up = nrm((L, MOE_EXPERTS, D, MOE_FF), D ** -0.5)
    exp_w_down = nrm((L, MOE_EXPERTS, MOE_FF, D), MOE_FF ** -0.5)
    final_g = 1.0 + nrm((D,), 0.02)
    return {'x': x, 'c': c, 'ada_w': ada_w, 'ada_b': ada_b, 'ada_table': ada_table,
            'norm1_g': norm1_g, 'norm2_g': norm2_g, 'w_in': w_in,
            'rwkv_mu': rwkv_mu, 'rwkv_w0': rwkv_w0, 'rwkv_w2': rwkv_w2, 'rwkv_a0': rwkv_a0,
            'rwkv_a2': rwkv_a2, 'rwkv_g2': rwkv_g2, 'rwkv_kk': rwkv_kk, 'rwkv_ka': rwkv_ka,
            'rwkv_rk': rwkv_rk, 'rwkv_lnx_w': rwkv_lnx_w, 'rwkv_lnx_b': rwkv_lnx_b,
            'pool_w': pool_w, 'pool_scale': pool_scale,
            'ssm_conv_w': ssm_conv_w, 'ssm_conv_b': ssm_conv_b, 'ssm_dt_bias': ssm_dt_bias,
            'ssm_a_log': ssm_a_log, 'ssm_d': ssm_d, 'ssm_norm_w': ssm_norm_w,
            'gate_up': gate_up, 'gate_b': gate_b, 'w_branch': w_branch, 'w_out': w_out,
            'router_group_w': router_group_w, 'router_group_b': router_group_b,
            'router_expert_w': router_expert_w, 'router_expert_b': router_expert_b,
            'exp_w_gate': exp_w_gate, 'exp_w_up': exp_w_up, 'exp_w_down': exp_w_down,
            'final_g': final_g}


def reference(x, c, ada_w, ada_b, ada_table, norm1_g, norm2_g, w_in,
              rwkv_mu, rwkv_w0, rwkv_w2, rwkv_a0, rwkv_a2, rwkv_g2, rwkv_kk, rwkv_ka,
              rwkv_rk, rwkv_lnx_w, rwkv_lnx_b, pool_w, pool_scale,
              ssm_conv_w, ssm_conv_b, ssm_dt_bias, ssm_a_log, ssm_d, ssm_norm_w,
              gate_up, gate_b, w_branch, w_out,
              router_group_w, router_group_b, router_expert_w, router_expert_b,
              exp_w_gate, exp_w_up, exp_w_down, final_g):
    cond = jax.nn.silu(c) @ ada_w + ada_b
    for l in range(DEPTH):
        mod = (cond + ada_table[l])[:, None, :]
        shift1, scale1, gate1, shift2, scale2, gate2 = jnp.split(mod, 6, axis=-1)
        h = rmsnorm(x, norm1_g[l]) * (1.0 + scale1) + shift1
        w_l = w_in[l]
        proj = lambda i: h @ w_l[:, IN_OFFSETS[i]:IN_OFFSETS[i + 1]]
        o_a = rwkv7_time_mix(proj(0), rwkv_mu[l], rwkv_w0[l], rwkv_w2[l], rwkv_a0[l], rwkv_a2[l],
                             rwkv_g2[l], rwkv_kk[l], rwkv_ka[l], rwkv_rk[l], rwkv_lnx_w[l], rwkv_lnx_b[l])
        o_b = dilated_attention_mixer(proj(1))
        o_c = multiscale_pool_mixer(proj(2), pool_w[l], pool_scale[l])
        o_d = mamba2_mixer(proj(3), ssm_conv_w[l], ssm_conv_b[l], ssm_dt_bias[l], ssm_a_log[l],
                           ssm_d[l], ssm_norm_w[l])
        pg = proj(4)
        branches = (o_a, o_b, o_c, o_d)
        merged = None
        for bi in range(N_BRANCH):
            gate = jax.nn.sigmoid((pg @ gate_up[l, bi] + gate_b[l, bi]).astype(jnp.float32)).astype(x.dtype)
            term = gate * (branches[bi].astype(x.dtype) @ w_branch[l, bi])
            merged = term if merged is None else merged + term
        x = x + gate1 * (merged @ w_out[l])
        h2 = rmsnorm(x, norm2_g[l]) * (1.0 + scale2) + shift2
        x = x + gate2 * hierarchical_moe(h2, router_group_w[l], router_group_b[l], router_expert_w[l],
                                         router_expert_b[l], exp_w_gate[l], exp_w_up[l], exp_w_down[l])
    return rmsnorm(x, final_g)
```

```python
import functools

import jax
import jax.numpy as jnp
from jax import lax
from jax.experimental import pallas as pl
from jax.experimental.pallas import tpu as pltpu

F32 = jnp.float32
BF16 = jnp.bfloat16
HI = lax.Precision.HIGHEST
SDS = jax.ShapeDtypeStruct

DEPTH = 2
NORM_EPS = 1e-6
BRANCH_WIDTH = 1024
RWKV_HEAD = 64
RWKV_HEADS = 16
RWKV_DECAY_RANK = 64
RWKV_ICLR_RANK = 64
RWKV_GATE_RANK = 160
RWKV_GN_EPS = 64e-5
RWKV_CHUNK = 64
ATT_HEAD = 128
ATT_GROUPS = ((128, 1), (512, 4), (2048, 16))
ATT_HEADS = 8
ATT_BLOCK = 128
POOL_WINDOWS = (2, 4, 8, 16)
POOL_GROUP = 256
POOL_HALO = 16
SSM_INNER = 1024
SSM_HEAD = 64
SSM_HEADS = 16
SSM_GROUPS = 4
SSM_STATE = 128
SSM_CONV = 4
SSM_CHUNK = 128
SSM_CONV_DIM = SSM_INNER + 2 * SSM_GROUPS * SSM_STATE
GATE_RANK = 256
MOE_GROUPS = 4
MOE_EXPERTS_PER_GROUP = 8
MOE_EXPERTS = 32
MOE_TOP_K = 2
MOE_FF = 512
MOE_ROWS = 256
LANES = 128
NEG = -1e30
A_MAIN = 3 * BRANCH_WIDTH
A_PAD = A_MAIN + 128 + 128 + 256
D_PAD = SSM_INNER + SSM_CONV_DIM + LANES
VMEM_LIMIT = 56 * 1024 * 1024


def _cp(*sem):
    return pltpu.CompilerParams(dimension_semantics=sem, vmem_limit_bytes=VMEM_LIMIT)


def _sigmoid(x):
    return 1.0 / (1.0 + jnp.exp(-x))


def _silu(x):
    return x * _sigmoid(x)


def _softplus(x):
    return jnp.maximum(x, 0.0) + jnp.log(1.0 + jnp.exp(-jnp.abs(x)))


def _dot(a, b):
    return jnp.dot(a, b, preferred_element_type=F32)


def _dot_hi(a, b):
    return jnp.dot(a, b, precision=HI, preferred_element_type=F32)


def _cond_kernel(c_ref, w_ref, b_ref, o_ref):
    a = _silu(c_ref[...]).astype(BF16)
    o_ref[...] = _dot(a, w_ref[...].astype(BF16)) + b_ref[...]


def _cond(c, ada_w, ada_b):
    bsz, d = c.shape
    n = ada_w.shape[1]
    tn = 512
    cp = jnp.zeros((8, d), F32).at[:bsz].set(c)
    out = pl.pallas_call(
        _cond_kernel, grid=(n // tn,),
        in_specs=[pl.BlockSpec((8, d), lambda j: (0, 0)),
                  pl.BlockSpec((d, tn), lambda j: (0, j)),
                  pl.BlockSpec((1, tn), lambda j: (0, j))],
        out_specs=pl.BlockSpec((8, tn), lambda j: (0, j)),
        out_shape=SDS((8, n), F32), compiler_params=_cp("parallel"), name="cond",
    )(cp, ada_w, ada_b.reshape(1, n))
    return out[:bsz]


def _norm_mod_kernel(x_ref, g_ref, sc_ref, sh_ref, o_ref):
    x = x_ref[0]
    ms = jnp.mean(x * x, axis=-1, keepdims=True)
    y = x * lax.rsqrt(ms + NORM_EPS) * g_ref[...]
    o_ref[0] = (y * (1.0 + sc_ref[0]) + sh_ref[0]).astype(o_ref.dtype)


def _norm_mod(x, g, scale, shift, out_dtype, tm=256):
    bsz, seq, d = x.shape
    return pl.pallas_call(
        _norm_mod_kernel, grid=(bsz, seq // tm),
        in_specs=[pl.BlockSpec((1, tm, d), lambda b, i: (b, i, 0)),
                  pl.BlockSpec((1, d), lambda b, i: (0, 0)),
                  pl.BlockSpec((1, 1, d), lambda b, i: (b, 0, 0)),
                  pl.BlockSpec((1, 1, d), lambda b, i: (b, 0, 0))],
        out_specs=pl.BlockSpec((1, tm, d), lambda b, i: (b, i, 0)),
        out_shape=SDS((bsz, seq, d), out_dtype), compiler_params=_cp("parallel", "parallel"), name="norm_mod",
    )(x, g.reshape(1, d), scale.reshape(bsz, 1, d), shift.reshape(bsz, 1, d))


def _mm_kernel(a_ref, b_ref, o_ref):
    o_ref[...] = _dot(a_ref[...], b_ref[...]).astype(o_ref.dtype)


def _matmul(a, b, out_dtype, tm=512, tn=512):
    m, k = a.shape
    n = b.shape[1]
    tn = min(tn, n)
    return pl.pallas_call(
        _mm_kernel, grid=(m // tm, n // tn),
        in_specs=[pl.BlockSpec((tm, k), lambda i, j: (i, 0)),
                  pl.BlockSpec((k, tn), lambda i, j: (0, j))],
        out_specs=pl.BlockSpec((tm, tn), lambda i, j: (i, j)),
        out_shape=SDS((m, n), out_dtype), compiler_params=_cp("parallel", "parallel"), name="matmul",
    )(a, b)


def _mm_resid_kernel(a_ref, b_ref, r_ref, g_ref, o_ref):
    o_ref[...] = r_ref[...] + g_ref[0] * _dot(a_ref[...], b_ref[...])


def _matmul_resid(a, b, resid, gate, seq, tm=512, tn=512):
    m, k = a.shape
    n = b.shape[1]
    bsz = gate.shape[0]
    return pl.pallas_call(
        _mm_resid_kernel, grid=(m // tm, n // tn),
        in_specs=[pl.BlockSpec((tm, k), lambda i, j: (i, 0)),
                  pl.BlockSpec((k, tn), lambda i, j: (0, j)),
                  pl.BlockSpec((tm, tn), lambda i, j: (i, j)),
                  pl.BlockSpec((1, 1, tn), lambda i, j: (i * tm // seq, 0, j))],
        out_specs=pl.BlockSpec((tm, tn), lambda i, j: (i, j)),
        out_shape=SDS((m, n), F32), compiler_params=_cp("parallel", "parallel"), name="matmul_resid",
    )(a, b, resid, gate.reshape(bsz, 1, n))


def _rwkv_prep_kernel(p_ref, halo_ref, mu_ref, w0_ref, a0_ref, kkw_ref, kaw_ref, w2_ref, a2_ref, g2_ref,
                      r_ref, k_ref, v_ref, kk_ref, a_ref, lw_ref, g_ref):
    x = p_ref[0]
    w = BRANCH_WIDTH
    prev_first = jnp.where(pl.program_id(1) > 0, halo_ref[0][7:8, :], 0.0)
    row = lax.broadcasted_iota(jnp.int32, x.shape, 0)
    prev = jnp.where(row == 0, prev_first, pltpu.roll(x, 1, axis=0))
    p = x + (prev - x) * mu_ref[...]
    r, k, v = p[:, 0:w], p[:, w:2 * w], p[:, 2 * w:3 * w]
    xw, xa, xg = p[:, A_MAIN:A_MAIN + 128], p[:, A_MAIN + 128:A_MAIN + 256], p[:, A_MAIN + 256:A_PAD]
    wlog = -_softplus(-(w0_ref[...] + _dot_hi(jnp.tanh(xw), w2_ref[...]))) - 0.5
    a = _sigmoid(a0_ref[...] + _dot_hi(xa, a2_ref[...]))
    r_ref[0] = r
    k_ref[0] = k * (1.0 + (a - 1.0) * kaw_ref[...])
    v_ref[0] = v
    kk_ref[0] = k * kkw_ref[...]
    a_ref[0] = a
    lw_ref[0] = -jnp.exp(wlog)
    g_ref[0] = _dot_hi(_sigmoid(xg), g2_ref[...])


def _rwkv_prep(pa, mu, w0, a0, kkw, kaw, w2, a2, g2, tm=256):
    bsz, seq, ap = pa.shape
    w = BRANCH_WIDTH
    row = lambda t: t.reshape(1, -1)
    full = lambda shape: pl.BlockSpec(shape, lambda b, i: (0, 0))
    out_spec = pl.BlockSpec((1, tm, w), lambda b, i: (b, i, 0))
    return pl.pallas_call(
        _rwkv_prep_kernel, grid=(bsz, seq // tm),
        in_specs=[pl.BlockSpec((1, tm, ap), lambda b, i: (b, i, 0)),
                  pl.BlockSpec((1, 8, ap), lambda b, i: (b, jnp.maximum(i * (tm // 8) - 1, 0), 0)),
                  full((1, ap)), full((1, w)), full((1, w)), full((1, w)), full((1, w)),
                  full((128, w)), full((128, w)), full((256, w))],
        out_specs=[out_spec] * 7,
        out_shape=[SDS((bsz, seq, w), F32)] * 7,
        compiler_params=_cp("parallel", "parallel"), name="rwkv_prep",
    )(pa, pa, row(mu), row(w0), row(a0), row(kkw), row(kaw), w2, a2, g2)


def _rwkv_rec_kernel(r_ref, k_ref, v_ref, kk_ref, a_ref, lw_ref, rk_ref, lnw_ref, lnb_ref, o_ref, s_ref):
    @pl.when(pl.program_id(1) == 0)
    def _():
        s_ref[...] = jnp.zeros_like(s_ref)

    r, k, v, kk, a, lw = r_ref[0], k_ref[0], v_ref[0], kk_ref[0], a_ref[0], lw_ref[0]
    nh, c, _ = r.shape
    kn = kk / jnp.maximum(jnp.sqrt(jnp.sum(kk * kk, axis=-1, keepdims=True)), 1e-12)
    b = kn * a
    row = lax.broadcasted_iota(jnp.int32, (c, c), 0)
    col = lax.broadcasted_iota(jnp.int32, (c, c), 1)
    strict, incl = (row > col)[None], (row >= col)[None]
    nt = lambda x, y: jnp.einsum('hik,hjk->hij', x, y, precision=HI, preferred_element_type=F32)
    nn = lambda x, y: jnp.einsum('hij,hjk->hik', x, y, precision=HI, preferred_element_type=F32)
    tn = lambda x, y: jnp.einsum('hiv,hik->hvk', x, y, precision=HI, preferred_element_type=F32)
    lc = nn(jnp.broadcast_to((row >= col).astype(F32)[None], (nh, c, c)), lw)
    lc_last = lc[:, c - 1:c, :]
    e_neg, e_end = jnp.exp(-lc), jnp.exp(lc_last - lc)
    kt, rt = kn * jnp.exp(lc - lw), r * jnp.exp(lc)
    bt, kd = b * e_neg, k * e_neg
    lm = jnp.where(strict, -nt(kt, bt), 0.0)
    ak = jnp.where(strict, nt(kt, kd), 0.0)
    bb = jnp.where(incl, nt(rt, bt), 0.0)
    bk = jnp.where(incl, nt(rt, kd), 0.0)
    s0 = s_ref[...]
    u = -(nt(kt, s0) + nn(ak, v))
    pw = lm
    n_doublings = c.bit_length() - 1
    for it in range(n_doublings):
        u = u + nn(pw, u)
        if it + 1 < n_doublings:
            pw = nn(pw, pw)
    y = nt(rt, s0) + nn(bb, u) + nn(bk, v)
    s_ref[...] = s0 * jnp.exp(lc_last) + tn(u, b * e_end) + tn(v, k * e_end)
    mean = jnp.mean(y, axis=-1, keepdims=True)
    var = jnp.mean(jnp.square(y - mean), axis=-1, keepdims=True)
    yn = (y - mean) * lax.rsqrt(var + RWKV_GN_EPS) * lnw_ref[...] + lnb_ref[...]
    o_ref[0] = yn + jnp.sum(r * k * rk_ref[...], axis=-1, keepdims=True) * v


def _rwkv_rec(r, k, v, kk, a, lw, rk, lnw, lnb, chunk=RWKV_CHUNK):
    bsz, nh, seq, n = r.shape
    blk = pl.BlockSpec((1, nh, chunk, n), lambda b, i: (b, 0, i, 0))
    par = pl.BlockSpec((nh, 1, n), lambda b, i: (0, 0, 0))
    return pl.pallas_call(
        _rwkv_rec_kernel, grid=(bsz, seq // chunk),
        in_specs=[blk] * 6 + [par] * 3, out_specs=blk,
        out_shape=SDS((bsz, nh, seq, n), F32),
        scratch_shapes=[pltpu.VMEM((nh, n, n), F32)],
        compiler_params=_cp("parallel", "arbitrary"), name="rwkv_rec",
    )(r, k, v, kk, a, lw, rk.reshape(nh, 1, n), lnw.reshape(nh, 1, n), lnb.reshape(nh, 1, n))


def _pad_rows(w, rows):
    return jnp.zeros((rows, w.shape[1]), w.dtype).at[:w.shape[0]].set(w)


def _rwkv_pack_cols(t):
    lead = t.shape[:-1]
    z = lambda n: jnp.zeros(lead + (n,), t.dtype)
    o1 = A_MAIN + RWKV_DECAY_RANK
    o2 = o1 + RWKV_ICLR_RANK
    return jnp.concatenate([t[..., :A_MAIN], t[..., A_MAIN:o1], z(128 - RWKV_DECAY_RANK), t[..., o1:o2],
                            z(128 - RWKV_ICLR_RANK), t[..., o2:], z(256 - RWKV_GATE_RANK)], axis=-1)


def _rwkv_mixer(pa, mu, w0, w2, a0, a2, g2, kkw, kaw, rk, lnw, lnb):
    bsz, seq, _ = pa.shape
    r, k, v, kk, a, lw, g = _rwkv_prep(pa, _rwkv_pack_cols(mu), w0, a0, kkw, kaw,
                                       _pad_rows(w2, 128), _pad_rows(a2, 128), _pad_rows(g2, 256))
    heads = lambda t: t.reshape(bsz, seq, RWKV_HEADS, RWKV_HEAD).transpose(0, 2, 1, 3)
    y = _rwkv_rec(*(heads(t) for t in (r, k, v, kk, a, lw)), rk,
                  lnw.reshape(RWKV_HEADS, RWKV_HEAD), lnb.reshape(RWKV_HEADS, RWKV_HEAD))
    return y.transpose(0, 2, 1, 3).reshape(bsz, seq, BRANCH_WIDTH), g


def _att_kernel(q_ref, kc_ref, kp_ref, vc_ref, vp_ref, o_ref, lse_ref):
    q = q_ref[0]
    nt = lambda x, y: lax.dot_general(x, y, (((1,), (1,)), ((), ())), preferred_element_type=F32)
    scale = ATT_HEAD ** -0.5
    qi = lax.broadcasted_iota(jnp.int32, (ATT_BLOCK, ATT_BLOCK), 0)
    ki = lax.broadcasted_iota(jnp.int32, (ATT_BLOCK, ATT_BLOCK), 1)
    s_c = jnp.where(ki <= qi, nt(q, kc_ref[0]) * scale, NEG)
    s_p = jnp.where((ki >= qi) & (pl.program_id(3) > 0), nt(q, kp_ref[0]) * scale, NEG)
    m = jnp.maximum(jnp.max(s_c, axis=-1, keepdims=True), jnp.max(s_p, axis=-1, keepdims=True))
    e_c, e_p = jnp.exp(s_c - m), jnp.exp(s_p - m)
    den = jnp.sum(e_c, axis=-1, keepdims=True) + jnp.sum(e_p, axis=-1, keepdims=True)
    o = _dot(e_c.astype(BF16), vc_ref[0]) + _dot(e_p.astype(BF16), vp_ref[0])
    o_ref[0] = o / den
    lse_ref[0] = jnp.broadcast_to(m + jnp.log(den), (ATT_BLOCK, ATT_HEAD))


def _att_group(pb, gi, dilation):
    bsz, seq, cols = pb.shape
    cb = cols // ATT_HEAD
    n_phase = seq // dilation
    nblk = n_phase // ATT_BLOCK
    ng = len(ATT_GROUPS)
    pbv = pb.reshape(bsz, n_phase, dilation * cols)
    col = lambda t: (lambda b, ph, h, i: (b, i, ph * cb + (t * ng + gi) * ATT_HEADS + h))
    colp = lambda t: (lambda b, ph, h, i: (b, jnp.maximum(i - 1, 0), ph * cb + (t * ng + gi) * ATT_HEADS + h))
    blk = (1, ATT_BLOCK, ATT_HEAD)
    out_spec = pl.BlockSpec(blk, lambda b, ph, h, i: (b, i, ph * ATT_HEADS + h))
    out_sds = SDS((bsz, n_phase, dilation * ATT_HEADS * ATT_HEAD), F32)
    o, lse = pl.pallas_call(
        _att_kernel, grid=(bsz, dilation, ATT_HEADS, nblk),
        in_specs=[pl.BlockSpec(blk, col(0)), pl.BlockSpec(blk, col(1)), pl.BlockSpec(blk, colp(1)),
                  pl.BlockSpec(blk, col(2)), pl.BlockSpec(blk, colp(2))],
        out_specs=[out_spec, out_spec], out_shape=[out_sds, out_sds],
        compiler_params=_cp("parallel", "parallel", "parallel", "arbitrary"), name=f"att_g{gi}",
    )(pbv, pbv, pbv, pbv, pbv)
    w = ATT_HEADS * ATT_HEAD
    return o.reshape(bsz, seq, w), lse.reshape(bsz, seq, w)


def _att_combine_kernel(o0, o1, o2, l0, l1, l2, out_ref):
    a0, a1, a2 = l0[...], l1[...], l2[...]
    m = jnp.maximum(jnp.maximum(a0, a1), a2)
    e0, e1, e2 = jnp.exp(a0 - m), jnp.exp(a1 - m), jnp.exp(a2 - m)
    out_ref[...] = ((e0 * o0[...] + e1 * o1[...] + e2 * o2[...]) / (e0 + e1 + e2)).astype(out_ref.dtype)


def _attention_mixer(pb, tm=512):
    bsz, seq, _ = pb.shape
    res = [_att_group(pb, gi, d) for gi, (_, d) in enumerate(ATT_GROUPS)]
    m, w = bsz * seq, ATT_HEADS * ATT_HEAD
    flat = [t[0].reshape(m, w) for t in res] + [t[1].reshape(m, w) for t in res]
    spec = pl.BlockSpec((tm, w), lambda i: (i, 0))
    return pl.pallas_call(
        _att_combine_kernel, grid=(m // tm,), in_specs=[spec] * 6, out_specs=spec,
        out_shape=SDS((m, w), BF16), compiler_params=_cp("parallel"), name="att_combine",
    )(*flat)


def _pool_kernel(x_ref, halo_ref, w_ref, sc_ref, o_ref, ext_ref):
    tm = x_ref.shape[1]
    i = pl.program_id(1)
    x = x_ref[0]
    ext_ref[0:POOL_HALO, :] = jnp.where(i > 0, halo_ref[0], 0.0)
    ext_ref[POOL_HALO:, :] = x
    pos = i * tm + lax.broadcasted_iota(jnp.int32, (tm, POOL_GROUP), 0)
    outs = []
    for gi, win in enumerate(POOL_WINDOWS):
        cols = slice(gi * POOL_GROUP, (gi + 1) * POOL_GROUP)
        xg = x[:, cols]
        s = xg
        for j in range(1, win):
            s = s + ext_ref[pl.ds(POOL_HALO - j, tm), cols]
        mixed = s / jnp.minimum(pos + 1, win).astype(F32) - xg
        outs.append(_dot(mixed.astype(BF16), w_ref[gi]))
    o_ref[0] = (jnp.concatenate(outs, axis=-1) * sc_ref[...]).astype(o_ref.dtype)


def _pool_mixer(pc, w_pool, scale, tm=256):
    bsz, seq, w = pc.shape
    out = pl.pallas_call(
        _pool_kernel, grid=(bsz, seq // tm),
        in_specs=[pl.BlockSpec((1, tm, w), lambda b, i: (b, i, 0)),
                  pl.BlockSpec((1, POOL_HALO, w), lambda b, i: (b, jnp.maximum(i * (tm // POOL_HALO) - 1, 0), 0)),
                  pl.BlockSpec(w_pool.shape, lambda b, i: (0, 0, 0)),
                  pl.BlockSpec((1, w), lambda b, i: (0, 0))],
        out_specs=pl.BlockSpec((1, tm, w), lambda b, i: (b, i, 0)),
        out_shape=SDS((bsz, seq, w), BF16),
        scratch_shapes=[pltpu.VMEM((tm + POOL_HALO, w), F32)],
        compiler_params=_cp("parallel", "parallel"), name="pool",
    )(pc, pc, w_pool.astype(BF16), scale.reshape(1, w))
    return out.reshape(bsz * seq, w)


SSM_HALO = 8


def _ssd_kernel(p_ref, halo_ref, cw_ref, cb_ref, dtb_ref, ah_ref, dsk_ref, nw_ref, e64_ref, e128_ref,
                o_ref, ext_ref, h_ref):
    ci = pl.program_id(1)
    q = SSM_CHUNK
    inner = SSM_INNER
    xbc_lo, xbc_hi = inner, inner + SSM_CONV_DIM

    @pl.when(ci == 0)
    def _():
        h_ref[...] = jnp.zeros_like(h_ref)

    z = p_ref[0, :, 0:inner]
    ext_ref[0:SSM_HALO, :] = jnp.where(ci > 0, halo_ref[0, :, xbc_lo:xbc_hi], 0.0)
    ext_ref[SSM_HALO:, :] = p_ref[0, :, xbc_lo:xbc_hi]
    conv = cb_ref[...]
    for j in range(SSM_CONV):
        conv = conv + cw_ref[j:j + 1, :] * ext_ref[pl.ds(SSM_HALO - (SSM_CONV - 1) + j, q), :]
    xbc = _silu(conv)
    xs = xbc[:, 0:inner]
    dt = _softplus(p_ref[0, :, xbc_hi:xbc_hi + LANES] + dtb_ref[...])
    a = dt * ah_ref[...]
    row = lax.broadcasted_iota(jnp.int32, (q, q), 0)
    col = lax.broadcasted_iota(jnp.int32, (q, q), 1)
    causal = row >= col
    a_cum = _dot_hi(causal.astype(F32), a)
    a_cum_t = a_cum.T
    dt_full = _dot_hi(dt, e64_ref[...])
    acum_full = _dot_hi(a_cum, e64_ref[...])
    alast_full = acum_full[q - 1:q, :]
    acum_b = _dot_hi(a_cum, e128_ref[...])
    xdt = xs * dt_full
    x_to_end = xdt * jnp.exp(alast_full - acum_full)
    exp_ac = jnp.exp(acum_full)
    chunk_dec = jnp.exp(alast_full)
    lane = lax.broadcasted_iota(jnp.int32, (q, LANES), 1)
    first_head = lane < SSM_HEAD
    ys = []
    for g in range(SSM_GROUPS):
        bm = xbc[:, inner + g * SSM_STATE:inner + (g + 1) * SSM_STATE]
        cm = xbc[:, inner + (SSM_GROUPS + g) * SSM_STATE:inner + (SSM_GROUPS + g + 1) * SSM_STATE].astype(BF16)
        bt = bm.T.astype(BF16)
        cb = _dot(cm, bt)
        for pr in range(SSM_HEADS // SSM_GROUPS // 2):
            pi = g * (SSM_HEADS // SSM_GROUPS // 2) + pr
            cols = slice(pi * LANES, (pi + 1) * LANES)
            mats = []
            for hd in (2 * pi, 2 * pi + 1):
                seg = acum_b[:, hd * LANES:(hd + 1) * LANES] - a_cum_t[hd:hd + 1, :]
                mats.append((cb * jnp.exp(jnp.where(causal, seg, NEG))).astype(BF16))
            xp = xdt[:, cols]
            x_blockdiag = jnp.concatenate([jnp.where(first_head, xp, 0.0), jnp.where(first_head, 0.0, xp)], axis=0)
            y_diag = _dot(jnp.concatenate(mats, axis=1), x_blockdiag.astype(BF16))
            h_t = h_ref[pi]
            y_off = _dot(cm, h_t.astype(BF16)) * exp_ac[:, cols]
            h_ref[pi] = h_t * chunk_dec[:, cols] + _dot(bt, x_to_end[:, cols].astype(BF16))
            ys.append(y_diag + y_off)
    y = (jnp.concatenate(ys, axis=-1) + xs * dsk_ref[...]) * _silu(z)
    gsize = inner // SSM_GROUPS
    outs = []
    for g in range(SSM_GROUPS):
        yg = y[:, g * gsize:(g + 1) * gsize]
        outs.append(yg * lax.rsqrt(jnp.mean(yg * yg, axis=-1, keepdims=True) + NORM_EPS))
    o_ref[0] = (jnp.concatenate(outs, axis=-1) * nw_ref[...]).astype(o_ref.dtype)


def _mamba_mixer(pd, conv_w, conv_b, dt_bias, a_log, d_skip, norm_w):
    bsz, seq, dp = pd.shape
    q = SSM_CHUNK
    pad_heads = lambda t: jnp.zeros((1, LANES), F32).at[0, :SSM_HEADS].set(t)
    head_of = lambda width: jnp.arange(SSM_HEADS * width, dtype=jnp.int32)[None, :] // width
    expand = lambda width: (jnp.arange(LANES, dtype=jnp.int32)[:, None] == head_of(width)).astype(F32)
    full2 = lambda shape: pl.BlockSpec(shape, lambda b, i: (0, 0))
    out = pl.pallas_call(
        _ssd_kernel, grid=(bsz, seq // q),
        in_specs=[pl.BlockSpec((1, q, dp), lambda b, i: (b, i, 0)),
                  pl.BlockSpec((1, SSM_HALO, dp), lambda b, i: (b, jnp.maximum(i * (q // SSM_HALO) - 1, 0), 0)),
                  full2((SSM_CONV, SSM_CONV_DIM)), full2((1, SSM_CONV_DIM)), full2((1, LANES)), full2((1, LANES)),
                  full2((1, SSM_INNER)), full2((1, SSM_INNER)),
                  full2((LANES, SSM_HEADS * SSM_HEAD)), full2((LANES, SSM_HEADS * LANES))],
        out_specs=pl.BlockSpec((1, q, SSM_INNER), lambda b, i: (b, i, 0)),
        out_shape=SDS((bsz, seq, SSM_INNER), BF16),
        scratch_shapes=[pltpu.VMEM((q + SSM_HALO, SSM_CONV_DIM), F32),
                        pltpu.VMEM((SSM_HEADS // 2, SSM_STATE, LANES), F32)],
        compiler_params=_cp("parallel", "arbitrary"), name="ssd",
    )(pd, pd, conv_w, conv_b.reshape(1, -1), pad_heads(dt_bias), pad_heads(-jnp.exp(a_log)),
      jnp.repeat(d_skip, SSM_HEAD).reshape(1, -1), norm_w.reshape(1, -1), expand(SSM_HEAD), expand(LANES))
    return out.reshape(bsz * seq, SSM_INNER)


def _merge_kernel(pg_ref, ya_ref, ga_ref, bb_ref, bc_ref, bd_ref, gu_ref, gb_ref, wb_ref, o_ref):
    pg = pg_ref[...]
    branches = ((ya_ref[...] * ga_ref[...]).astype(BF16), bb_ref[...], bc_ref[...], bd_ref[...])
    acc = None
    for bi, br in enumerate(branches):
        term = _sigmoid(_dot(pg, gu_ref[bi]) + gb_ref[bi]) * _dot(br, wb_ref[bi])
        acc = term if acc is None else acc + term
    o_ref[...] = acc.astype(o_ref.dtype)


def _merge(pg, ya, ga, bb, bc, bd, gate_up, gate_b, w_branch, tm=512, tn=512):
    m = pg.shape[0]
    nb, kw, d = w_branch.shape
    rows = lambda k: pl.BlockSpec((tm, k), lambda i, j: (i, 0))
    return pl.pallas_call(
        _merge_kernel, grid=(m // tm, d // tn),
        in_specs=[rows(pg.shape[1]), rows(kw), rows(kw), rows(kw), rows(kw), rows(kw),
                  pl.BlockSpec((nb, pg.shape[1], tn), lambda i, j: (0, 0, j)),
                  pl.BlockSpec((nb, 1, tn), lambda i, j: (0, 0, j)),
                  pl.BlockSpec((nb, kw, tn), lambda i, j: (0, 0, j))],
        out_specs=pl.BlockSpec((tm, tn), lambda i, j: (i, j)),
        out_shape=SDS((m, d), BF16), compiler_params=_cp("parallel", "parallel"), name="merge",
    )(pg, ya, ga, bb, bc, bd, gate_up.astype(BF16), gate_b.reshape(nb, 1, d), w_branch.astype(BF16))


def _norm_route_kernel(x_ref, g_ref, sc_ref, sh_ref, wr_ref, br_ref, h_ref, ids_ref, wts_ref):
    x = x_ref[0]
    ms = jnp.mean(x * x, axis=-1, keepdims=True)
    h = x * lax.rsqrt(ms + NORM_EPS) * g_ref[...] * (1.0 + sc_ref[0]) + sh_ref[0]
    h_ref[0] = h
    lg = _dot_hi(h, wr_ref[...]) + br_ref[...]
    lane = lax.broadcasted_iota(jnp.int32, lg.shape, 1)
    lane_f = lane.astype(F32)
    first = lambda hit: jnp.min(jnp.where(hit, lane_f, float(LANES)), axis=-1, keepdims=True)
    gmask = lane < MOE_GROUPS
    gl = jnp.where(gmask, lg, NEG)
    gmax = jnp.max(gl, axis=-1, keepdims=True)
    gsel = first(gl == gmax)
    gprob = 1.0 / jnp.sum(jnp.where(gmask, jnp.exp(gl - gmax), 0.0), axis=-1, keepdims=True)
    lo = MOE_GROUPS + gsel * MOE_EXPERTS_PER_GROUP
    emask = (lane_f >= lo) & (lane_f < lo + MOE_EXPERTS_PER_GROUP)
    el = jnp.where(emask, lg, NEG)
    v1 = jnp.max(el, axis=-1, keepdims=True)
    i1 = first(el == v1)
    el2 = jnp.where(lane_f == i1, NEG, el)
    v2 = jnp.max(el2, axis=-1, keepdims=True)
    i2 = first((el2 == v2) & emask & (lane_f != i1))
    t = jnp.exp(v2 - v1)
    w1 = gprob / (1.0 + t)
    w2 = gprob * t / (1.0 + t)
    ids_ref[0] = jnp.where(lane == 0, i1, jnp.where(lane == 1, i2, float(MOE_GROUPS))).astype(jnp.int32) - MOE_GROUPS
    wts_ref[0] = jnp.where(lane == 0, w1, jnp.where(lane == 1, w2, 0.0))


def _norm_route(x, g, scale, shift, rg_w, rg_b, re_w, re_b, tm=256):
    bsz, seq, d = x.shape
    n_log = MOE_GROUPS + MOE_EXPERTS
    wr = jnp.zeros((d, LANES), F32).at[:, :n_log].set(jnp.concatenate([rg_w, re_w], axis=1))
    br = jnp.zeros((1, LANES), F32).at[0, :n_log].set(jnp.concatenate([rg_b, re_b]))
    tok = lambda width: pl.BlockSpec((1, tm, width), lambda b, i: (b, i, 0))
    h, ids, wts = pl.pallas_call(
        _norm_route_kernel, grid=(bsz, seq // tm),
        in_specs=[tok(d), pl.BlockSpec((1, d), lambda b, i: (0, 0)),
                  pl.BlockSpec((1, 1, d), lambda b, i: (b, 0, 0)), pl.BlockSpec((1, 1, d), lambda b, i: (b, 0, 0)),
                  pl.BlockSpec((d, LANES), lambda b, i: (0, 0)), pl.BlockSpec((1, LANES), lambda b, i: (0, 0))],
        out_specs=[tok(d), tok(LANES), tok(LANES)],
        out_shape=[SDS((bsz, seq, d), F32), SDS((bsz, seq, LANES), jnp.int32), SDS((bsz, seq, LANES), F32)],
        compiler_params=_cp("parallel", "parallel"), name="norm_route",
    )(x, g.reshape(1, d), scale.reshape(bsz, 1, d), shift.reshape(bsz, 1, d), wr, br)
    m = bsz * seq
    return h.reshape(m, d), ids.reshape(m, LANES)[:, :MOE_TOP_K], wts.reshape(m, LANES)[:, :MOE_TOP_K]


GATHER_ROWS = 512


def _row_gather_kernel(idx_ref, src_ref, dst_ref, sem):
    base = pl.program_id(0) * GATHER_ROWS

    def row_copy(s, src_row):
        return pltpu.make_async_copy(src_ref.at[pl.ds(src_row, 1)], dst_ref.at[pl.ds(base + s, 1)], sem)

    def issue(s, carry):
        row_copy(s, idx_ref[base + s]).start()
        return carry

    def wait(s, carry):
        row_copy(s, 0).wait()
        return carry

    lax.fori_loop(0, GATHER_ROWS, issue, 0)
    lax.fori_loop(0, GATHER_ROWS, wait, 0)


def _row_gather(src, idx):
    n = idx.shape[0]
    return pl.pallas_call(
        _row_gather_kernel,
        grid_spec=pltpu.PrefetchScalarGridSpec(
            num_scalar_prefetch=1, grid=(n // GATHER_ROWS,),
            in_specs=[pl.BlockSpec(memory_space=pl.ANY)], out_specs=pl.BlockSpec(memory_space=pl.ANY),
            scratch_shapes=[pltpu.SemaphoreType.DMA(())]),
        out_shape=SDS((n, src.shape[1]), src.dtype),
        compiler_params=pltpu.CompilerParams(dimension_semantics=("arbitrary",)), name="row_gather",
    )(idx, src)


def _ffn_kernel(be_ref, x_ref, wg_ref, wu_ref, wd_ref, sw_ref, o_ref):
    x = x_ref[...].astype(BF16)
    hidden = (_silu(_dot(x, wg_ref[0])) * _dot(x, wu_ref[0])).astype(BF16)
    o_ref[...] = _dot(hidden, wd_ref[0]) * sw_ref[...]


def _moe_plan(expert, weight):
    n_tok = expert.shape[0]
    n_assign = n_tok * MOE_TOP_K
    flat_e = expert.reshape(-1)
    order = jnp.argsort(flat_e)
    sorted_e = flat_e[order]
    counts = jnp.bincount(flat_e, length=MOE_EXPERTS)
    padded = (counts + MOE_ROWS - 1) // MOE_ROWS * MOE_ROWS
    pad_end = jnp.cumsum(padded)
    pad_start = pad_end - padded
    start = jnp.cumsum(counts) - counts
    dest = (pad_start[sorted_e] + jnp.arange(n_assign) - start[sorted_e]).astype(jnp.int32)
    n_blocks = n_assign // MOE_ROWS + MOE_EXPERTS
    n_slots = n_blocks * MOE_ROWS
    tok = jnp.repeat(jnp.arange(n_tok, dtype=jnp.int32), MOE_TOP_K)
    slot_tok = jnp.zeros((n_slots,), jnp.int32).at[dest].set(tok[order])
    slot_w = jnp.zeros((n_slots,), F32).at[dest].set(weight.reshape(-1)[order])
    block_e = jnp.minimum(jnp.searchsorted(pad_end, jnp.arange(n_blocks) * MOE_ROWS, side='right'),
                          MOE_EXPERTS - 1).astype(jnp.int32)
    slot_of = jnp.zeros((n_assign,), jnp.int32).at[order].set(dest)
    return slot_tok, slot_w, block_e, slot_of


def _moe_apply(h, expert, weight, w_gate, w_up, w_down):
    n_tok, d = h.shape
    slot_tok, slot_w, block_e, slot_of = _moe_plan(expert, weight)
    n_slots = slot_tok.shape[0]
    xs = _row_gather(h, slot_tok)
    yb = pl.pallas_call(
        _ffn_kernel,
        grid_spec=pltpu.PrefetchScalarGridSpec(
            num_scalar_prefetch=1, grid=(n_slots // MOE_ROWS,),
            in_specs=[pl.BlockSpec((MOE_ROWS, d), lambda j, be: (j, 0)),
                      pl.BlockSpec((1, d, MOE_FF), lambda j, be: (be[j], 0, 0)),
                      pl.BlockSpec((1, d, MOE_FF), lambda j, be: (be[j], 0, 0)),
                      pl.BlockSpec((1, MOE_FF, d), lambda j, be: (be[j], 0, 0)),
                      pl.BlockSpec((MOE_ROWS, 1), lambda j, be: (j, 0))],
            out_specs=pl.BlockSpec((MOE_ROWS, d), lambda j, be: (j, 0))),
        out_shape=SDS((n_slots, d), F32), compiler_params=_cp("arbitrary"), name="moe_ffn",
    )(block_e, xs, w_gate.astype(BF16), w_up.astype(BF16), w_down.astype(BF16), slot_w.reshape(n_slots, 1))
    k_major = slot_of.reshape(n_tok, MOE_TOP_K).T.reshape(-1)
    return _row_gather(yb, k_major).reshape(MOE_TOP_K, n_tok, d)


def _moe_combine_kernel(x_ref, y0_ref, y1_ref, g_ref, o_ref):
    o_ref[0] = x_ref[0] + g_ref[0] * (y0_ref[0, 0] + y1_ref[0, 0])


def _moe_combine(x, y2, gate, tm=256):
    bsz, seq, d = x.shape
    y4 = y2.reshape(MOE_TOP_K, bsz, seq, d)
    tok = pl.BlockSpec((1, tm, d), lambda b, i: (b, i, 0))
    return pl.pallas_call(
        _moe_combine_kernel, grid=(bsz, seq // tm),
        in_specs=[tok, pl.BlockSpec((1, 1, tm, d), lambda b, i: (0, b, i, 0)),
                  pl.BlockSpec((1, 1, tm, d), lambda b, i: (1, b, i, 0)),
                  pl.BlockSpec((1, 1, d), lambda b, i: (b, 0, 0))],
        out_specs=tok, out_shape=SDS((bsz, seq, d), F32),
        compiler_params=_cp("parallel", "parallel"), name="moe_combine",
    )(x, y4, y4, gate.reshape(bsz, 1, d))


def kernel(x, c, ada_w, ada_b, ada_table, norm1_g, norm2_g, w_in, rwkv_mu, rwkv_w0, rwkv_w2, rwkv_a0, rwkv_a2,
           rwkv_g2, rwkv_kk, rwkv_ka, rwkv_rk, rwkv_lnx_w, rwkv_lnx_b, pool_w, pool_scale, ssm_conv_w, ssm_conv_b,
           ssm_dt_bias, ssm_a_log, ssm_d, ssm_norm_w, gate_up, gate_b, w_branch, w_out, router_group_w,
           router_group_b, router_expert_w, router_expert_b, exp_w_gate, exp_w_up, exp_w_down, final_g):
    bsz, seq, d = x.shape
    m = bsz * seq
    a_cols = rwkv_mu.shape[1]
    b_cols = 3 * len(ATT_GROUPS) * ATT_HEADS * ATT_HEAD
    d_cols = SSM_INNER + SSM_CONV_DIM + SSM_HEADS
    offs = (0, a_cols, a_cols + b_cols, a_cols + b_cols + BRANCH_WIDTH, a_cols + b_cols + BRANCH_WIDTH + d_cols)
    cond = _cond(c, ada_w, ada_b)
    for l in range(DEPTH):
        shift1, scale1, gate1, shift2, scale2, gate2 = jnp.split(cond + ada_table[l], 6, axis=-1)
        h = _norm_mod(x, norm1_g[l], scale1, shift1, BF16).reshape(m, d)
        w_l = w_in[l]
        wa = _rwkv_pack_cols(w_l[:, offs[0]:offs[1]]).astype(BF16)
        wb = w_l[:, offs[1]:offs[2]].astype(BF16)
        wc = w_l[:, offs[2]:offs[3]].astype(BF16)
        wd = jnp.zeros((d, D_PAD), BF16).at[:, :d_cols].set(w_l[:, offs[3]:offs[4]].astype(BF16))
        wg = w_l[:, offs[4]:].astype(BF16)
        pa = _matmul(h, wa, F32).reshape(bsz, seq, A_PAD)
        pb = _matmul(h, wb, BF16).reshape(bsz, seq, b_cols)
        pc = _matmul(h, wc, F32).reshape(bsz, seq, BRANCH_WIDTH)
        pd = _matmul(h, wd, F32, tn=640).reshape(bsz, seq, D_PAD)
        pg = _matmul(h, wg, BF16)
        ya, ga = _rwkv_mixer(pa, rwkv_mu[l], rwkv_w0[l], rwkv_w2[l], rwkv_a0[l], rwkv_a2[l], rwkv_g2[l],
                             rwkv_kk[l], rwkv_ka[l], rwkv_rk[l], rwkv_lnx_w[l], rwkv_lnx_b[l])
        ob = _attention_mixer(pb)
        oc = _pool_mixer(pc, pool_w[l], pool_scale[l])
        od = _mamba_mixer(pd, ssm_conv_w[l], ssm_conv_b[l], ssm_dt_bias[l], ssm_a_log[l], ssm_d[l], ssm_norm_w[l])
        merged = _merge(pg, ya.reshape(m, BRANCH_WIDTH), ga.reshape(m, BRANCH_WIDTH), ob, oc, od,
                        gate_up[l], gate_b[l], w_branch[l])
        x = _matmul_resid(merged, w_out[l].astype(BF16), x.reshape(m, d), gate1, seq).reshape(bsz, seq, d)
        h2, expert, weight = _norm_route(x, norm2_g[l], scale2, shift2, router_group_w[l], router_group_b[l],
                                         router_expert_w[l], router_expert_b[l])
        y2 = _moe_apply(h2, expert, weight, exp_w_gate[l], exp_w_up[l], exp_w_down[l])
        x = _moe_combine(x, y2, gate2)
    zeros = jnp.zeros((bsz, d), F32)
    return _norm_mod(x, final_g, zeros, zeros, F32)
```

```python
import functools

import jax
import jax.numpy as jnp
from jax import lax
from jax.experimental import pallas as pl
from jax.experimental.pallas import tpu as pltpu

F32 = jnp.float32
BF16 = jnp.bfloat16
HI = lax.Precision.HIGHEST
SDS = jax.ShapeDtypeStruct

DEPTH = 2
NORM_EPS = 1e-6
BRANCH_WIDTH = 1024
RWKV_HEAD = 64
RWKV_HEADS = 16
RWKV_DECAY_RANK = 64
RWKV_ICLR_RANK = 64
RWKV_GATE_RANK = 160
RWKV_GN_EPS = 64e-5
RWKV_CHUNK = 64
ATT_HEAD = 128
ATT_GROUPS = ((128, 1), (512, 4), (2048, 16))
ATT_HEADS = 8
ATT_BLOCK = 128
POOL_WINDOWS = (2, 4, 8, 16)
POOL_GROUP = 256
POOL_HALO = 16
SSM_INNER = 1024
SSM_HEAD = 64
SSM_HEADS = 16
SSM_GROUPS = 4
SSM_STATE = 128
SSM_CONV = 4
SSM_CHUNK = 128
SSM_CONV_DIM = SSM_INNER + 2 * SSM_GROUPS * SSM_STATE
GATE_RANK = 256
MOE_GROUPS = 4
MOE_EXPERTS_PER_GROUP = 8
MOE_EXPERTS = 32
MOE_TOP_K = 2
MOE_FF = 512
MOE_ROWS = 256
LANES = 128
NEG = -1e30
A_MAIN = 3 * BRANCH_WIDTH
A_PAD = A_MAIN + 128 + 128 + 256
D_PAD = SSM_INNER + SSM_CONV_DIM + LANES
VMEM_LIMIT = 56 * 1024 * 1024


def _cp(*sem):
    return pltpu.CompilerParams(dimension_semantics=sem, vmem_limit_bytes=VMEM_LIMIT)


def _sigmoid(x):
    return 1.0 / (1.0 + jnp.exp(-x))


def _silu(x):
    return x * _sigmoid(x)


def _softplus(x):
    return jnp.maximum(x, 0.0) + jnp.log(1.0 + jnp.exp(-jnp.abs(x)))


def _dot(a, b):
    return jnp.dot(a, b, preferred_element_type=F32)


def _dot_hi(a, b):
    return jnp.dot(a, b, precision=HI, preferred_element_type=F32)


def _cond_kernel(c_ref, w_ref, b_ref, o_ref):
    a = _silu(c_ref[...]).astype(BF16)
    o_ref[...] = _dot(a, w_ref[...].astype(BF16)) + b_ref[...]


def _cond(c, ada_w, ada_b):
    bsz, d = c.shape
    n = ada_w.shape[1]
    tn = 512
    cp = jnp.zeros((8, d), F32).at[:bsz].set(c)
    out = pl.pallas_call(
        _cond_kernel, grid=(n // tn,),
        in_specs=[pl.BlockSpec((8, d), lambda j: (0, 0)),
                  pl.BlockSpec((d, tn), lambda j: (0, j)),
                  pl.BlockSpec((1, tn), lambda j: (0, j))],
        out_specs=pl.BlockSpec((8, tn), lambda j: (0, j)),
        out_shape=SDS((8, n), F32), compiler_params=_cp("parallel"), name="cond",
    )(cp, ada_w, ada_b.reshape(1, n))
    return out[:bsz]


def _norm_mod_kernel(x_ref, g_ref, sc_ref, sh_ref, o_ref):
    x = x_ref[0]
    ms = jnp.mean(x * x, axis=-1, keepdims=True)
    y = x * lax.rsqrt(ms + NORM_EPS) * g_ref[...]
    o_ref[0] = (y * (1.0 + sc_ref[0]) + sh_ref[0]).astype(o_ref.dtype)


def _norm_mod(x, g, scale, shift, out_dtype, tm=256):
    bsz, seq, d = x.shape
    return pl.pallas_call(
        _norm_mod_kernel, grid=(bsz, seq // tm),
        in_specs=[pl.BlockSpec((1, tm, d), lambda b, i: (b, i, 0)),
                  pl.BlockSpec((1, d), lambda b, i: (0, 0)),
                  pl.BlockSpec((1, 1, d), lambda b, i: (b, 0, 0)),
                  pl.BlockSpec((1, 1, d), lambda b, i: (b, 0, 0))],
        out_specs=pl.BlockSpec((1, tm, d), lambda b, i: (b, i, 0)),
        out_shape=SDS((bsz, seq, d), out_dtype), compiler_params=_cp("parallel", "parallel"), name="norm_mod",
    )(x, g.reshape(1, d), scale.reshape(bsz, 1, d), shift.reshape(bsz, 1, d))


def _mm_kernel(a_ref, b_ref, o_ref):
    o_ref[...] = _dot(a_ref[...], b_ref[...]).astype(o_ref.dtype)


def _matmul(a, b, out_dtype, tm=512, tn=512):
    m, k = a.shape
    n = b.shape[1]
    tn = min(tn, n)
    return pl.pallas_call(
        _mm_kernel, grid=(m // tm, n // tn),
        in_specs=[pl.BlockSpec((tm, k), lambda i, j: (i, 0)),
                  pl.BlockSpec((k, tn), lambda i, j: (0, j))],
        out_specs=pl.BlockSpec((tm, tn), lambda i, j: (i, j)),
        out_shape=SDS((m, n), out_dtype), compiler_params=_cp("parallel", "parallel"), name="matmul",
    )(a, b)


def _mm_resid_kernel(a_ref, b_ref, r_ref, g_ref, o_ref):
    o_ref[...] = r_ref[...] + g_ref[0] * _dot(a_ref[...], b_ref[...])


def _matmul_resid(a, b, resid, gate, seq, tm=512, tn=512):
    m, k = a.shape
    n = b.shape[1]
    bsz = gate.shape[0]
    return pl.pallas_call(
        _mm_resid_kernel, grid=(m // tm, n // tn),
        in_specs=[pl.BlockSpec((tm, k), lambda i, j: (i, 0)),
                  pl.BlockSpec((k, tn), lambda i, j: (0, j)),
                  pl.BlockSpec((tm, tn), lambda i, j: (i, j)),
                  pl.BlockSpec((1, 1, tn), lambda i, j: (i * tm // seq, 0, j))],
        out_specs=pl.BlockSpec((tm, tn), lambda i, j: (i, j)),
        out_shape=SDS((m, n), F32), compiler_params=_cp("parallel", "parallel"), name="matmul_resid",
    )(a, b, resid, gate.reshape(bsz, 1, n))


def _rwkv_prep_kernel(p_ref, halo_ref, mu_ref, w0_ref, a0_ref, kkw_ref, kaw_ref, w2_ref, a2_ref, g2_ref,
                      r_ref, k_ref, v_ref, kk_ref, a_ref, lw_ref, g_ref):
    x = p_ref[0]
    w = BRANCH_WIDTH
    prev_first = jnp.where(pl.program_id(1) > 0, halo_ref[0][7:8, :], 0.0)
    row = lax.broadcasted_iota(jnp.int32, x.shape, 0)
    prev = jnp.where(row == 0, prev_first, pltpu.roll(x, 1, axis=0))
    p = x + (prev - x) * mu_ref[...]
    r, k, v = p[:, 0:w], p[:, w:2 * w], p[:, 2 * w:3 * w]
    xw, xa, xg = p[:, A_MAIN:A_MAIN + 128], p[:, A_MAIN + 128:A_MAIN + 256], p[:, A_MAIN + 256:A_PAD]
    wlog = -_softplus(-(w0_ref[...] + _dot_hi(jnp.tanh(xw), w2_ref[...]))) - 0.5
    a = _sigmoid(a0_ref[...] + _dot_hi(xa, a2_ref[...]))
    r_ref[0] = r
    k_ref[0] = k * (1.0 + (a - 1.0) * kaw_ref[...])
    v_ref[0] = v
    kk_ref[0] = k * kkw_ref[...]
    a_ref[0] = a
    lw_ref[0] = -jnp.exp(wlog)
    g_ref[0] = _dot_hi(_sigmoid(xg), g2_ref[...])


def _rwkv_prep(pa, mu, w0, a0, kkw, kaw, w2, a2, g2, tm=256):
    bsz, seq, ap = pa.shape
    w = BRANCH_WIDTH
    row = lambda t: t.reshape(1, -1)
    full = lambda shape: pl.BlockSpec(shape, lambda b, i: (0, 0))
    out_spec = pl.BlockSpec((1, tm, w), lambda b, i: (b, i, 0))
    return pl.pallas_call(
        _rwkv_prep_kernel, grid=(bsz, seq // tm),
        in_specs=[pl.BlockSpec((1, tm, ap), lambda b, i: (b, i, 0)),
                  pl.BlockSpec((1, 8, ap), lambda b, i: (b, jnp.maximum(i * (tm // 8) - 1, 0), 0)),
                  full((1, ap)), full((1, w)), full((1, w)), full((1, w)), full((1, w)),
                  full((128, w)), full((128, w)), full((256, w))],
        out_specs=[out_spec] * 7,
        out_shape=[SDS((bsz, seq, w), F32)] * 7,
        compiler_params=_cp("parallel", "parallel"), name="rwkv_prep",
    )(pa, pa, row(mu), row(w0), row(a0), row(kkw), row(kaw), w2, a2, g2)


def _rwkv_rec_kernel(r_ref, k_ref, v_ref, kk_ref, a_ref, lw_ref, rk_ref, lnw_ref, lnb_ref, o_ref, s_ref):
    @pl.when(pl.program_id(1) == 0)
    def _():
        s_ref[...] = jnp.zeros_like(s_ref)

    nh, n = s_ref.shape[0], s_ref.shape[1]
    heads = lambda ref: jnp.stack([ref[0, :, h * n:(h + 1) * n] for h in range(nh)])
    r, k, v, kk, a, lw = (heads(ref) for ref in (r_ref, k_ref, v_ref, kk_ref, a_ref, lw_ref))
    c = r.shape[1]
    kn = kk / jnp.maximum(jnp.sqrt(jnp.sum(kk * kk, axis=-1, keepdims=True)), 1e-12)
    b = kn * a
    row = lax.broadcasted_iota(jnp.int32, (c, c), 0)
    col = lax.broadcasted_iota(jnp.int32, (c, c), 1)
    strict, incl = (row > col)[None], (row >= col)[None]
    bdot = lambda spec: (lambda x, y: jnp.einsum(spec, x.astype(BF16), y.astype(BF16), preferred_element_type=F32))
    nt, nn, tn = bdot('hik,hjk->hij'), bdot('hij,hjk->hik'), bdot('hiv,hik->hvk')
    tril = jnp.broadcast_to((row >= col).astype(BF16)[None], (nh, c, c))
    lw_hi = lw.astype(BF16)
    lw_r1 = lw - lw_hi.astype(F32)
    lw_mid = lw_r1.astype(BF16)
    lw_lo = (lw_r1 - lw_mid.astype(F32)).astype(BF16)
    lc = nn(tril, lw_hi) + (nn(tril, lw_mid) + nn(tril, lw_lo))
    lc_last = lc[:, c - 1:c, :]
    e_neg, e_end = jnp.exp(-lc), jnp.exp(lc_last - lc)
    kt, rt = kn * jnp.exp(lc - lw), r * jnp.exp(lc)
    bt, kd = b * e_neg, k * e_neg
    lm = jnp.where(strict, -nt(kt, bt), 0.0)
    ak = jnp.where(strict, nt(kt, kd), 0.0)
    bb = jnp.where(incl, nt(rt, bt), 0.0)
    bk = jnp.where(incl, nt(rt, kd), 0.0)
    s0 = s_ref[...]
    u = -(nt(kt, s0) + nn(ak, v))
    pw = lm
    n_doublings = c.bit_length() - 1
    for it in range(n_doublings):
        u = u + nn(pw, u)
        if it + 1 < n_doublings:
            pw = nn(pw, pw)
    y = nt(rt, s0) + nn(bb, u) + nn(bk, v)
    s_ref[...] = s0 * jnp.exp(lc_last) + tn(u, b * e_end) + tn(v, k * e_end)
    mean = jnp.mean(y, axis=-1, keepdims=True)
    var = jnp.mean(jnp.square(y - mean), axis=-1, keepdims=True)
    out = (y - mean) * lax.rsqrt(var + RWKV_GN_EPS) * lnw_ref[...] + lnb_ref[...]
    out = out + jnp.sum(r * k * rk_ref[...], axis=-1, keepdims=True) * v
    for h in range(nh):
        o_ref[0, :, h * n:(h + 1) * n] = out[h]


def _rwkv_rec(r, k, v, kk, a, lw, rk, lnw, lnb, chunk=RWKV_CHUNK):
    bsz, seq, w = r.shape
    nh, n = rk.shape
    blk = pl.BlockSpec((1, chunk, w), lambda b, i: (b, i, 0))
    par = pl.BlockSpec((nh, 1, n), lambda b, i: (0, 0, 0))
    return pl.pallas_call(
        _rwkv_rec_kernel, grid=(bsz, seq // chunk),
        in_specs=[blk] * 6 + [par] * 3, out_specs=blk,
        out_shape=SDS((bsz, seq, w), F32),
        scratch_shapes=[pltpu.VMEM((nh, n, n), F32)],
        compiler_params=_cp("parallel", "arbitrary"), name="rwkv_rec",
    )(r, k, v, kk, a, lw, rk.reshape(nh, 1, n), lnw.reshape(nh, 1, n), lnb.reshape(nh, 1, n))


def _pad_rows(w, rows):
    return jnp.zeros((rows, w.shape[1]), w.dtype).at[:w.shape[0]].set(w)


def _rwkv_pack_cols(t):
    lead = t.shape[:-1]
    z = lambda n: jnp.zeros(lead + (n,), t.dtype)
    o1 = A_MAIN + RWKV_DECAY_RANK
    o2 = o1 + RWKV_ICLR_RANK
    return jnp.concatenate([t[..., :A_MAIN], t[..., A_MAIN:o1], z(128 - RWKV_DECAY_RANK), t[..., o1:o2],
                            z(128 - RWKV_ICLR_RANK), t[..., o2:], z(256 - RWKV_GATE_RANK)], axis=-1)


def _rwkv_mixer(pa, mu, w0, w2, a0, a2, g2, kkw, kaw, rk, lnw, lnb):
    bsz, seq, _ = pa.shape
    r, k, v, kk, a, lw, g = _rwkv_prep(pa, _rwkv_pack_cols(mu), w0, a0, kkw, kaw,
                                       _pad_rows(w2, 128), _pad_rows(a2, 128), _pad_rows(g2, 256))
    y = _rwkv_rec(r, k, v, kk, a, lw, rk, lnw.reshape(RWKV_HEADS, RWKV_HEAD), lnb.reshape(RWKV_HEADS, RWKV_HEAD))
    return y, g


def _att_kernel(q_ref, kc_ref, kp_ref, vc_ref, vp_ref, o_ref, lse_ref):
    nt = lambda x, y: lax.dot_general(x, y, (((1,), (1,)), ((), ())), preferred_element_type=F32)
    scale = ATT_HEAD ** -0.5
    qi = lax.broadcasted_iota(jnp.int32, (ATT_BLOCK, ATT_BLOCK), 0)
    ki = lax.broadcasted_iota(jnp.int32, (ATT_BLOCK, ATT_BLOCK), 1)
    cur_ok = ki <= qi
    prev_ok = (ki >= qi) & (pl.program_id(2) > 0)
    for h in range(ATT_HEADS):
        cols = slice(h * ATT_HEAD, (h + 1) * ATT_HEAD)
        q = q_ref[0, :, cols]
        s_c = jnp.where(cur_ok, nt(q, kc_ref[0, :, cols]) * scale, NEG)
        s_p = jnp.where(prev_ok, nt(q, kp_ref[0, :, cols]) * scale, NEG)
        m = jnp.maximum(jnp.max(s_c, axis=-1, keepdims=True), jnp.max(s_p, axis=-1, keepdims=True))
        e_c, e_p = jnp.exp(s_c - m), jnp.exp(s_p - m)
        den = jnp.sum(e_c, axis=-1, keepdims=True) + jnp.sum(e_p, axis=-1, keepdims=True)
        o = _dot(e_c.astype(BF16), vc_ref[0, :, cols]) + _dot(e_p.astype(BF16), vp_ref[0, :, cols])
        o_ref[0, :, cols] = o / den
        lse_ref[0, :, cols] = jnp.broadcast_to(m + jnp.log(den), (ATT_BLOCK, ATT_HEAD))


def _att_group(pb, gi, dilation):
    bsz, seq, cols = pb.shape
    w = ATT_HEADS * ATT_HEAD
    cb = cols // w
    n_phase = seq // dilation
    nblk = n_phase // ATT_BLOCK
    ng = len(ATT_GROUPS)
    pbv = pb.reshape(bsz, n_phase, dilation * cols)
    col = lambda t: (lambda b, ph, i: (b, i, ph * cb + t * ng + gi))
    colp = lambda t: (lambda b, ph, i: (b, jnp.maximum(i - 1, 0), ph * cb + t * ng + gi))
    blk = (1, ATT_BLOCK, w)
    out_spec = pl.BlockSpec(blk, lambda b, ph, i: (b, i, ph))
    out_sds = SDS((bsz, n_phase, dilation * w), F32)
    o, lse = pl.pallas_call(
        _att_kernel, grid=(bsz, dilation, nblk),
        in_specs=[pl.BlockSpec(blk, col(0)), pl.BlockSpec(blk, col(1)), pl.BlockSpec(blk, colp(1)),
                  pl.BlockSpec(blk, col(2)), pl.BlockSpec(blk, colp(2))],
        out_specs=[out_spec, out_spec], out_shape=[out_sds, out_sds],
        compiler_params=_cp("parallel", "parallel", "arbitrary"), name=f"att_g{gi}",
    )(pbv, pbv, pbv, pbv, pbv)
    return o.reshape(bsz, seq, w), lse.reshape(bsz, seq, w)


def _att_combine_kernel(o0, o1, o2, l0, l1, l2, out_ref):
    a0, a1, a2 = l0[...], l1[...], l2[...]
    m = jnp.maximum(jnp.maximum(a0, a1), a2)
    e0, e1, e2 = jnp.exp(a0 - m), jnp.exp(a1 - m), jnp.exp(a2 - m)
    out_ref[...] = ((e0 * o0[...] + e1 * o1[...] + e2 * o2[...]) / (e0 + e1 + e2)).astype(out_ref.dtype)


def _attention_mixer(pb, tm=512):
    bsz, seq, _ = pb.shape
    res = [_att_group(pb, gi, d) for gi, (_, d) in enumerate(ATT_GROUPS)]
    m, w = bsz * seq, ATT_HEADS * ATT_HEAD
    flat = [t[0].reshape(m, w) for t in res] + [t[1].reshape(m, w) for t in res]
    spec = pl.BlockSpec((tm, w), lambda i: (i, 0))
    return pl.pallas_call(
        _att_combine_kernel, grid=(m // tm,), in_specs=[spec] * 6, out_specs=spec,
        out_shape=SDS((m, w), BF16), compiler_params=_cp("parallel"), name="att_combine",
    )(*flat)


def _pool_kernel(x_ref, halo_ref, w_ref, sc_ref, o_ref, ext_ref):
    tm = x_ref.shape[1]
    i = pl.program_id(1)
    x = x_ref[0]
    ext_ref[0:POOL_HALO, :] = jnp.where(i > 0, halo_ref[0], 0.0)
    ext_ref[POOL_HALO:, :] = x
    pos = i * tm + lax.broadcasted_iota(jnp.int32, (tm, POOL_GROUP), 0)
    outs = []
    for gi, win in enumerate(POOL_WINDOWS):
        cols = slice(gi * POOL_GROUP, (gi + 1) * POOL_GROUP)
        xg = x[:, cols]
        s = xg
        for j in range(1, win):
            s = s + ext_ref[pl.ds(POOL_HALO - j, tm), cols]
        mixed = s / jnp.minimum(pos + 1, win).astype(F32) - xg
        outs.append(_dot(mixed.astype(BF16), w_ref[gi]))
    o_ref[0] = (jnp.concatenate(outs, axis=-1) * sc_ref[...]).astype(o_ref.dtype)


def _pool_mixer(pc, w_pool, scale, tm=256):
    bsz, seq, w = pc.shape
    out = pl.pallas_call(
        _pool_kernel, grid=(bsz, seq // tm),
        in_specs=[pl.BlockSpec((1, tm, w), lambda b, i: (b, i, 0)),
                  pl.BlockSpec((1, POOL_HALO, w), lambda b, i: (b, jnp.maximum(i * (tm // POOL_HALO) - 1, 0), 0)),
                  pl.BlockSpec(w_pool.shape, lambda b, i: (0, 0, 0)),
                  pl.BlockSpec((1, w), lambda b, i: (0, 0))],
        out_specs=pl.BlockSpec((1, tm, w), lambda b, i: (b, i, 0)),
        out_shape=SDS((bsz, seq, w), BF16),
        scratch_shapes=[pltpu.VMEM((tm + POOL_HALO, w), F32)],
        compiler_params=_cp("parallel", "parallel"), name="pool",
    )(pc, pc, w_pool.astype(BF16), scale.reshape(1, w))
    return out.reshape(bsz * seq, w)


SSM_HALO = 8


def _ssd_kernel(p_ref, halo_ref, cw_ref, cb_ref, dtb_ref, ah_ref, dsk_ref, nw_ref, e64_ref, e128_ref,
                o_ref, ext_ref, h_ref):
    ci = pl.program_id(1)
    q = SSM_CHUNK
    inner = SSM_INNER
    xbc_lo, xbc_hi = inner, inner + SSM_CONV_DIM

    @pl.when(ci == 0)
    def _():
        h_ref[...] = jnp.zeros_like(h_ref)

    z = p_ref[0, :, 0:inner]
    ext_ref[0:SSM_HALO, :] = jnp.where(ci > 0, halo_ref[0, :, xbc_lo:xbc_hi], 0.0)
    ext_ref[SSM_HALO:, :] = p_ref[0, :, xbc_lo:xbc_hi]
    conv = cb_ref[...]
    for j in range(SSM_CONV):
        conv = conv + cw_ref[j:j + 1, :] * ext_ref[pl.ds(SSM_HALO - (SSM_CONV - 1) + j, q), :]
    xbc = _silu(conv)
    xs = xbc[:, 0:inner]
    dt = _softplus(p_ref[0, :, xbc_hi:xbc_hi + LANES] + dtb_ref[...])
    a = dt * ah_ref[...]
    row = lax.broadcasted_iota(jnp.int32, (q, q), 0)
    col = lax.broadcasted_iota(jnp.int32, (q, q), 1)
    causal = row >= col
    a_cum = _dot_hi(causal.astype(F32), a)
    a_cum_t = a_cum.T
    dt_full = _dot_hi(dt, e64_ref[...])
    acum_full = _dot_hi(a_cum, e64_ref[...])
    alast_full = acum_full[q - 1:q, :]
    acum_b = _dot_hi(a_cum, e128_ref[...])
    xdt = xs * dt_full
    x_to_end = xdt * jnp.exp(alast_full - acum_full)
    exp_ac = jnp.exp(acum_full)
    chunk_dec = jnp.exp(alast_full)
    lane = lax.broadcasted_iota(jnp.int32, (q, LANES), 1)
    first_head = lane < SSM_HEAD
    ys = []
    for g in range(SSM_GROUPS):
        bm = xbc[:, inner + g * SSM_STATE:inner + (g + 1) * SSM_STATE]
        cm = xbc[:, inner + (SSM_GROUPS + g) * SSM_STATE:inner + (SSM_GROUPS + g + 1) * SSM_STATE].astype(BF16)
        bt = bm.T.astype(BF16)
        cb = _dot(cm, bt)
        for pr in range(SSM_HEADS // SSM_GROUPS // 2):
            pi = g * (SSM_HEADS // SSM_GROUPS // 2) + pr
            cols = slice(pi * LANES, (pi + 1) * LANES)
            mats = []
            for hd in (2 * pi, 2 * pi + 1):
                seg = acum_b[:, hd * LANES:(hd + 1) * LANES] - a_cum_t[hd:hd + 1, :]
                mats.append((cb * jnp.exp(jnp.where(causal, seg, NEG))).astype(BF16))
            xp = xdt[:, cols]
            x_blockdiag = jnp.concatenate([jnp.where(first_head, xp, 0.0), jnp.where(first_head, 0.0, xp)], axis=0)
            y_diag = _dot(jnp.concatenate(mats, axis=1), x_blockdiag.astype(BF16))
            h_t = h_ref[pi]
            y_off = _dot(cm, h_t.astype(BF16)) * exp_ac[:, cols]
            h_ref[pi] = h_t * chunk_dec[:, cols] + _dot(bt, x_to_end[:, cols].astype(BF16))
            ys.append(y_diag + y_off)
    y = (jnp.concatenate(ys, axis=-1) + xs * dsk_ref[...]) * _silu(z)
    gsize = inner // SSM_GROUPS
    outs = []
    for g in range(SSM_GROUPS):
        yg = y[:, g * gsize:(g + 1) * gsize]
        outs.append(yg * lax.rsqrt(jnp.mean(yg * yg, axis=-1, keepdims=True) + NORM_EPS))
    o_ref[0] = (jnp.concatenate(outs, axis=-1) * nw_ref[...]).astype(o_ref.dtype)


def _mamba_mixer(pd, conv_w, conv_b, dt_bias, a_log, d_skip, norm_w):
    bsz, seq, dp = pd.shape
    q = SSM_CHUNK
    pad_heads = lambda t: jnp.zeros((1, LANES), F32).at[0, :SSM_HEADS].set(t)
    head_of = lambda width: jnp.arange(SSM_HEADS * width, dtype=jnp.int32)[None, :] // width
    expand = lambda width: (jnp.arange(LANES, dtype=jnp.int32)[:, None] == head_of(width)).astype(F32)
    full2 = lambda shape: pl.BlockSpec(shape, lambda b, i: (0, 0))
    out = pl.pallas_call(
        _ssd_kernel, grid=(bsz, seq // q),
        in_specs=[pl.BlockSpec((1, q, dp), lambda b, i: (b, i, 0)),
                  pl.BlockSpec((1, SSM_HALO, dp), lambda b, i: (b, jnp.maximum(i * (q // SSM_HALO) - 1, 0), 0)),
                  full2((SSM_CONV, SSM_CONV_DIM)), full2((1, SSM_CONV_DIM)), full2((1, LANES)), full2((1, LANES)),
                  full2((1, SSM_INNER)), full2((1, SSM_INNER)),
                  full2((LANES, SSM_HEADS * SSM_HEAD)), full2((LANES, SSM_HEADS * LANES))],
        out_specs=pl.BlockSpec((1, q, SSM_INNER), lambda b, i: (b, i, 0)),
        out_shape=SDS((bsz, seq, SSM_INNER), BF16),
        scratch_shapes=[pltpu.VMEM((q + SSM_HALO, SSM_CONV_DIM), F32),
                        pltpu.VMEM((SSM_HEADS // 2, SSM_STATE, LANES), F32)],
        compiler_params=_cp("parallel", "arbitrary"), name="ssd",
    )(pd, pd, conv_w, conv_b.reshape(1, -1), pad_heads(dt_bias), pad_heads(-jnp.exp(a_log)),
      jnp.repeat(d_skip, SSM_HEAD).reshape(1, -1), norm_w.reshape(1, -1), expand(SSM_HEAD), expand(LANES))
    return out.reshape(bsz * seq, SSM_INNER)


def _merge_kernel(pg_ref, ya_ref, ga_ref, bb_ref, bc_ref, bd_ref, gu_ref, gb_ref, wb_ref, o_ref):
    pg = pg_ref[...]
    branches = ((ya_ref[...] * ga_ref[...]).astype(BF16), bb_ref[...], bc_ref[...], bd_ref[...])
    acc = None
    for bi, br in enumerate(branches):
        term = _sigmoid(_dot(pg, gu_ref[bi]) + gb_ref[bi]) * _dot(br, wb_ref[bi])
        acc = term if acc is None else acc + term
    o_ref[...] = acc.astype(o_ref.dtype)


def _merge(pg, ya, ga, bb, bc, bd, gate_up, gate_b, w_branch, tm=512, tn=512):
    m = pg.shape[0]
    nb, kw, d = w_branch.shape
    rows = lambda k: pl.BlockSpec((tm, k), lambda i, j: (i, 0))
    return pl.pallas_call(
        _merge_kernel, grid=(m // tm, d // tn),
        in_specs=[rows(pg.shape[1]), rows(kw), rows(kw), rows(kw), rows(kw), rows(kw),
                  pl.BlockSpec((nb, pg.shape[1], tn), lambda i, j: (0, 0, j)),
                  pl.BlockSpec((nb, 1, tn), lambda i, j: (0, 0, j)),
                  pl.BlockSpec((nb, kw, tn), lambda i, j: (0, 0, j))],
        out_specs=pl.BlockSpec((tm, tn), lambda i, j: (i, j)),
        out_shape=SDS((m, d), BF16), compiler_params=_cp("parallel", "parallel"), name="merge",
    )(pg, ya, ga, bb, bc, bd, gate_up.astype(BF16), gate_b.reshape(nb, 1, d), w_branch.astype(BF16))


SUBLANES = 8


def _store_slabs(ref, rows):
    w = ref.shape[-1]
    for j in range(SUBLANES):
        ref[:, j, :] = rows[:, j * w:(j + 1) * w]


def _norm_route_kernel(x_ref, g_ref, sc_ref, sh_ref, wr_ref, br_ref, h_ref, ids_ref, wts_ref):
    x = x_ref[0]
    ms = jnp.mean(x * x, axis=-1, keepdims=True)
    h = x * lax.rsqrt(ms + NORM_EPS) * g_ref[...] * (1.0 + sc_ref[0]) + sh_ref[0]
    _store_slabs(h_ref, h)
    lg = _dot_hi(h, wr_ref[...]) + br_ref[...]
    lane = lax.broadcasted_iota(jnp.int32, lg.shape, 1)
    lane_f = lane.astype(F32)
    first = lambda hit: jnp.min(jnp.where(hit, lane_f, float(LANES)), axis=-1, keepdims=True)
    gmask = lane < MOE_GROUPS
    gl = jnp.where(gmask, lg, NEG)
    gmax = jnp.max(gl, axis=-1, keepdims=True)
    gsel = first(gl == gmax)
    gprob = 1.0 / jnp.sum(jnp.where(gmask, jnp.exp(gl - gmax), 0.0), axis=-1, keepdims=True)
    lo = MOE_GROUPS + gsel * MOE_EXPERTS_PER_GROUP
    emask = (lane_f >= lo) & (lane_f < lo + MOE_EXPERTS_PER_GROUP)
    el = jnp.where(emask, lg, NEG)
    v1 = jnp.max(el, axis=-1, keepdims=True)
    i1 = first(el == v1)
    el2 = jnp.where(lane_f == i1, NEG, el)
    v2 = jnp.max(el2, axis=-1, keepdims=True)
    i2 = first((el2 == v2) & emask & (lane_f != i1))
    t = jnp.exp(v2 - v1)
    w1 = gprob / (1.0 + t)
    w2 = gprob * t / (1.0 + t)
    ids_ref[0] = jnp.where(lane == 0, i1, jnp.where(lane == 1, i2, float(MOE_GROUPS))).astype(jnp.int32) - MOE_GROUPS
    wts_ref[0] = jnp.where(lane == 0, w1, jnp.where(lane == 1, w2, 0.0))


def _norm_route(x, g, scale, shift, rg_w, rg_b, re_w, re_b, tm=256):
    bsz, seq, d = x.shape
    n_log = MOE_GROUPS + MOE_EXPERTS
    wr = jnp.zeros((d, LANES), F32).at[:, :n_log].set(jnp.concatenate([rg_w, re_w], axis=1))
    br = jnp.zeros((1, LANES), F32).at[0, :n_log].set(jnp.concatenate([rg_b, re_b]))
    tok = lambda width: pl.BlockSpec((1, tm, width), lambda b, i: (b, i, 0))
    m, dw, nt = bsz * seq, d // SUBLANES, seq // tm
    h, ids, wts = pl.pallas_call(
        _norm_route_kernel, grid=(bsz, nt),
        in_specs=[tok(d), pl.BlockSpec((1, d), lambda b, i: (0, 0)),
                  pl.BlockSpec((1, 1, d), lambda b, i: (b, 0, 0)), pl.BlockSpec((1, 1, d), lambda b, i: (b, 0, 0)),
                  pl.BlockSpec((d, LANES), lambda b, i: (0, 0)), pl.BlockSpec((1, LANES), lambda b, i: (0, 0))],
        out_specs=[pl.BlockSpec((tm, SUBLANES, dw), lambda b, i: (b * nt + i, 0, 0)), tok(LANES), tok(LANES)],
        out_shape=[SDS((m, SUBLANES, dw), F32), SDS((bsz, seq, LANES), jnp.int32), SDS((bsz, seq, LANES), F32)],
        compiler_params=_cp("parallel", "parallel"), name="norm_route",
    )(x, g.reshape(1, d), scale.reshape(bsz, 1, d), shift.reshape(bsz, 1, d), wr, br)
    return h, ids.reshape(m, LANES)[:, :MOE_TOP_K], wts.reshape(m, LANES)[:, :MOE_TOP_K]


GATHER_ROWS = 512
GATHER_UNROLL = 8


GATHER_ZERO_ROWS = 64


def _row_gather_kernel(idx_ref, used_ref, src_ref, dst_ref, zero_ref, sem, zsem):
    base = pl.program_id(0) * GATHER_ROWS

    def row_copy(s, src_row):
        return pltpu.make_async_copy(src_ref.at[pl.ds(src_row, 1)], dst_ref.at[pl.ds(base + s, 1)], sem)

    def issue(s, carry):
        row_copy(s, idx_ref[base + s]).start()
        return carry

    def wait(s, carry):
        row_copy(s, 0).wait()
        return carry

    @pl.when(base < used_ref[0])
    def _():
        lax.fori_loop(0, GATHER_ROWS, issue, 0, unroll=GATHER_UNROLL)
        lax.fori_loop(0, GATHER_ROWS, wait, 0, unroll=GATHER_UNROLL)

    @pl.when(base >= used_ref[0])
    def _():
        zero_ref[...] = jnp.zeros_like(zero_ref)
        fills = [pltpu.make_async_copy(zero_ref, dst_ref.at[pl.ds(base + z * GATHER_ZERO_ROWS, GATHER_ZERO_ROWS)], zsem)
                 for z in range(GATHER_ROWS // GATHER_ZERO_ROWS)]
        for f in fills:
            f.start()
        for f in fills:
            f.wait()


def _row_gather(src, idx, n_used):
    n = idx.shape[0]
    return pl.pallas_call(
        _row_gather_kernel,
        grid_spec=pltpu.PrefetchScalarGridSpec(
            num_scalar_prefetch=2, grid=(n // GATHER_ROWS,),
            in_specs=[pl.BlockSpec(memory_space=pl.ANY)], out_specs=pl.BlockSpec(memory_space=pl.ANY),
            scratch_shapes=[pltpu.VMEM((GATHER_ZERO_ROWS,) + src.shape[1:], src.dtype),
                            pltpu.SemaphoreType.DMA(()), pltpu.SemaphoreType.DMA(())]),
        out_shape=SDS((n,) + src.shape[1:], src.dtype),
        compiler_params=pltpu.CompilerParams(dimension_semantics=("arbitrary",)), name="row_gather",
    )(idx, n_used, src)


def _ffn_kernel(be_ref, nb_ref, x_ref, wg_ref, wu_ref, wd_ref, sw_ref, o_ref):
    @pl.when(pl.program_id(0) >= nb_ref[0])
    def _():
        o_ref[...] = jnp.zeros_like(o_ref)

    @pl.when(pl.program_id(0) < nb_ref[0])
    def _():
        dw = x_ref.shape[-1]
        gate = up = None
        for j in range(SUBLANES):
            xj = x_ref[:, j, :].astype(BF16)
            gj, uj = _dot(xj, wg_ref[0, j * dw:(j + 1) * dw, :]), _dot(xj, wu_ref[0, j * dw:(j + 1) * dw, :])
            gate, up = (gj, uj) if gate is None else (gate + gj, up + uj)
        hidden = (_silu(gate) * up).astype(BF16)
        _store_slabs(o_ref, _dot(hidden, wd_ref[0]) * sw_ref[...])


def _moe_plan(expert, weight):
    n_tok = expert.shape[0]
    n_assign = n_tok * MOE_TOP_K
    flat_e = expert.reshape(-1)
    order = jnp.argsort(flat_e)
    sorted_e = flat_e[order]
    counts = jnp.bincount(flat_e, length=MOE_EXPERTS)
    padded = (counts + MOE_ROWS - 1) // MOE_ROWS * MOE_ROWS
    pad_end = jnp.cumsum(padded)
    pad_start = pad_end - padded
    start = jnp.cumsum(counts) - counts
    dest = (pad_start[sorted_e] + jnp.arange(n_assign) - start[sorted_e]).astype(jnp.int32)
    n_blocks = n_assign // MOE_ROWS + MOE_EXPERTS
    n_slots = n_blocks * MOE_ROWS
    tok = jnp.repeat(jnp.arange(n_tok, dtype=jnp.int32), MOE_TOP_K)
    slot_tok = jnp.zeros((n_slots,), jnp.int32).at[dest].set(tok[order])
    slot_w = jnp.zeros((n_slots,), F32).at[dest].set(weight.reshape(-1)[order])
    block_e = jnp.minimum(jnp.searchsorted(pad_end, jnp.arange(n_blocks) * MOE_ROWS, side='right'),
                          MOE_EXPERTS - 1).astype(jnp.int32)
    slot_of = jnp.zeros((n_assign,), jnp.int32).at[order].set(dest)
    n_used = pad_end[-1:].astype(jnp.int32)
    return slot_tok, slot_w, block_e, slot_of, n_used


def _moe_apply(h, expert, weight, w_gate, w_up, w_down):
    n_tok, _, dw = h.shape
    d = dw * SUBLANES
    slot_tok, slot_w, block_e, slot_of, n_used = _moe_plan(expert, weight)
    n_slots = slot_tok.shape[0]
    xs = _row_gather(h, slot_tok, n_used)
    last = lambda nb: nb[0] - 1
    slab = pl.BlockSpec((MOE_ROWS, SUBLANES, dw), lambda j, be, nb: (jnp.minimum(j, last(nb)), 0, 0))
    expert_blk = lambda shape: pl.BlockSpec(shape, lambda j, be, nb: (be[jnp.minimum(j, last(nb))], 0, 0))
    yb = pl.pallas_call(
        _ffn_kernel,
        grid_spec=pltpu.PrefetchScalarGridSpec(
            num_scalar_prefetch=2, grid=(n_slots // MOE_ROWS,),
            in_specs=[slab, expert_blk((1, d, MOE_FF)), expert_blk((1, d, MOE_FF)), expert_blk((1, MOE_FF, d)),
                      pl.BlockSpec((MOE_ROWS, 1), lambda j, be, nb: (jnp.minimum(j, last(nb)), 0))],
            out_specs=pl.BlockSpec((MOE_ROWS, SUBLANES, dw), lambda j, be, nb: (j, 0, 0))),
        out_shape=SDS((n_slots, SUBLANES, dw), F32), compiler_params=_cp("arbitrary"), name="moe_ffn",
    )(block_e, n_used // MOE_ROWS, xs, w_gate.astype(BF16), w_up.astype(BF16), w_down.astype(BF16),
      slot_w.reshape(n_slots, 1))
    k_major = slot_of.reshape(n_tok, MOE_TOP_K).T.reshape(-1)
    return _row_gather(yb, k_major, jnp.full((1,), k_major.shape[0], jnp.int32))


def _moe_combine_kernel(x_ref, y0_ref, y1_ref, g_ref, o_ref):
    dw = y0_ref.shape[-1]
    for j in range(SUBLANES):
        cols = slice(j * dw, (j + 1) * dw)
        o_ref[0, :, cols] = x_ref[0, :, cols] + g_ref[0, :, cols] * (y0_ref[0, 0, :, j, :] + y1_ref[0, 0, :, j, :])


def _moe_combine(x, y2, gate, tm=256):
    bsz, seq, d = x.shape
    dw = d // SUBLANES
    y5 = y2.reshape(MOE_TOP_K, bsz, seq, SUBLANES, dw)
    tok = pl.BlockSpec((1, tm, d), lambda b, i: (b, i, 0))
    slab = lambda k: pl.BlockSpec((1, 1, tm, SUBLANES, dw), lambda b, i: (k, b, i, 0, 0))
    return pl.pallas_call(
        _moe_combine_kernel, grid=(bsz, seq // tm),
        in_specs=[tok, slab(0), slab(1), pl.BlockSpec((1, 1, d), lambda b, i: (b, 0, 0))],
        out_specs=tok, out_shape=SDS((bsz, seq, d), F32),
        compiler_params=_cp("parallel", "parallel"), name="moe_combine",
    )(x, y5, y5, gate.reshape(bsz, 1, d))


def kernel(x, c, ada_w, ada_b, ada_table, norm1_g, norm2_g, w_in, rwkv_mu, rwkv_w0, rwkv_w2, rwkv_a0, rwkv_a2,
           rwkv_g2, rwkv_kk, rwkv_ka, rwkv_rk, rwkv_lnx_w, rwkv_lnx_b, pool_w, pool_scale, ssm_conv_w, ssm_conv_b,
           ssm_dt_bias, ssm_a_log, ssm_d, ssm_norm_w, gate_up, gate_b, w_branch, w_out, router_group_w,
           router_group_b, router_expert_w, router_expert_b, exp_w_gate, exp_w_up, exp_w_down, final_g):
    bsz, seq, d = x.shape
    m = bsz * seq
    a_cols = rwkv_mu.shape[1]
    b_cols = 3 * len(ATT_GROUPS) * ATT_HEADS * ATT_HEAD
    d_cols = SSM_INNER + SSM_CONV_DIM + SSM_HEADS
    offs = (0, a_cols, a_cols + b_cols, a_cols + b_cols + BRANCH_WIDTH, a_cols + b_cols + BRANCH_WIDTH + d_cols)
    cond = _cond(c, ada_w, ada_b)
    for l in range(DEPTH):
        shift1, scale1, gate1, shift2, scale2, gate2 = jnp.split(cond + ada_table[l], 6, axis=-1)
        h = _norm_mod(x, norm1_g[l], scale1, shift1, BF16).reshape(m, d)
        w_l = w_in[l]
        wa = _rwkv_pack_cols(w_l[:, offs[0]:offs[1]]).astype(BF16)
        wb = w_l[:, offs[1]:offs[2]].astype(BF16)
        wc = w_l[:, offs[2]:offs[3]].astype(BF16)
        wd = jnp.zeros((d, D_PAD), BF16).at[:, :d_cols].set(w_l[:, offs[3]:offs[4]].astype(BF16))
        wg = w_l[:, offs[4]:].astype(BF16)
        pa = _matmul(h, wa, F32).reshape(bsz, seq, A_PAD)
        pb = _matmul(h, wb, BF16).reshape(bsz, seq, b_cols)
        pc = _matmul(h, wc, F32).reshape(bsz, seq, BRANCH_WIDTH)
        pd = _matmul(h, wd, F32, tn=640).reshape(bsz, seq, D_PAD)
        pg = _matmul(h, wg, BF16)
        ya, ga = _rwkv_mixer(pa, rwkv_mu[l], rwkv_w0[l], rwkv_w2[l], rwkv_a0[l], rwkv_a2[l], rwkv_g2[l],
                             rwkv_kk[l], rwkv_ka[l], rwkv_rk[l], rwkv_lnx_w[l], rwkv_lnx_b[l])
        ob = _attention_mixer(pb)
        oc = _pool_mixer(pc, pool_w[l], pool_scale[l])
        od = _mamba_mixer(pd, ssm_conv_w[l], ssm_conv_b[l], ssm_dt_bias[l], ssm_a_log[l], ssm_d[l], ssm_norm_w[l])
        merged = _merge(pg, ya.reshape(m, BRANCH_WIDTH), ga.reshape(m, BRANCH_WIDTH), ob, oc, od,
                        gate_up[l], gate_b[l], w_branch[l])
        x = _matmul_resid(merged, w_out[l].astype(BF16), x.reshape(m, d), gate1, seq).reshape(bsz, seq, d)
        h2, expert, weight = _norm_route(x, norm2_g[l], scale2, shift2, router_group_w[l], router_group_b[l],
                                         router_expert_w[l], router_expert_b[l])
        y2 = _moe_apply(h2, expert, weight, exp_w_gate[l], exp_w_up[l], exp_w_down[l])
        x = _moe_combine(x, y2, gate2)
    zeros = jnp.zeros((bsz, d), F32)
    return _norm_mod(x, final_g, zeros, zeros, F32)
```

```python
import functools

import jax
import jax.numpy as jnp
from jax import lax
from jax.experimental import pallas as pl
from jax.experimental.pallas import tpu as pltpu

F32 = jnp.float32
BF16 = jnp.bfloat16
HI = lax.Precision.HIGHEST
SDS = jax.ShapeDtypeStruct

DEPTH = 2
NORM_EPS = 1e-6
BRANCH_WIDTH = 1024
RWKV_HEAD = 64
RWKV_HEADS = 16
RWKV_DECAY_RANK = 64
RWKV_ICLR_RANK = 64
RWKV_GATE_RANK = 160
RWKV_GN_EPS = 64e-5
RWKV_CHUNK = 64
ATT_HEAD = 128
ATT_GROUPS = ((128, 1), (512, 4), (2048, 16))
ATT_HEADS = 8
ATT_BLOCK = 128
POOL_WINDOWS = (2, 4, 8, 16)
POOL_GROUP = 256
POOL_HALO = 16
SSM_INNER = 1024
SSM_HEAD = 64
SSM_HEADS = 16
SSM_GROUPS = 4
SSM_STATE = 128
SSM_CONV = 4
SSM_CHUNK = 128
SSM_CONV_DIM = SSM_INNER + 2 * SSM_GROUPS * SSM_STATE
GATE_RANK = 256
MOE_GROUPS = 4
MOE_EXPERTS_PER_GROUP = 8
MOE_EXPERTS = 32
MOE_TOP_K = 2
MOE_FF = 512
MOE_ROWS = 256
LANES = 128
NEG = -1e30
A_MAIN = 3 * BRANCH_WIDTH
A_PAD = A_MAIN + 128 + 128 + 256
D_PAD = SSM_INNER + SSM_CONV_DIM + LANES
VMEM_LIMIT = 56 * 1024 * 1024


def _cp(*sem):
    return pltpu.CompilerParams(dimension_semantics=sem, vmem_limit_bytes=VMEM_LIMIT)


def _sigmoid(x):
    return 1.0 / (1.0 + jnp.exp(-x))


def _silu(x):
    return x * _sigmoid(x)


def _softplus(x):
    return jnp.maximum(x, 0.0) + jnp.log(1.0 + jnp.exp(-jnp.abs(x)))


def _dot(a, b):
    return jnp.dot(a, b, preferred_element_type=F32)


def _dot_hi(a, b):
    return jnp.dot(a, b, precision=HI, preferred_element_type=F32)


def _cond_kernel(c_ref, w_ref, b_ref, o_ref):
    a = _silu(c_ref[...]).astype(BF16)
    o_ref[...] = _dot(a, w_ref[...].astype(BF16)) + b_ref[...]


def _cond(c, ada_w, ada_b):
    bsz, d = c.shape
    n = ada_w.shape[1]
    tn = 512
    cp = jnp.zeros((8, d), F32).at[:bsz].set(c)
    out = pl.pallas_call(
        _cond_kernel, grid=(n // tn,),
        in_specs=[pl.BlockSpec((8, d), lambda j: (0, 0)),
                  pl.BlockSpec((d, tn), lambda j: (0, j)),
                  pl.BlockSpec((1, tn), lambda j: (0, j))],
        out_specs=pl.BlockSpec((8, tn), lambda j: (0, j)),
        out_shape=SDS((8, n), F32), compiler_params=_cp("parallel"), name="cond",
    )(cp, ada_w, ada_b.reshape(1, n))
    return out[:bsz]


def _norm_mod_kernel(x_ref, g_ref, sc_ref, sh_ref, o_ref):
    x = x_ref[0]
    ms = jnp.mean(x * x, axis=-1, keepdims=True)
    y = x * lax.rsqrt(ms + NORM_EPS) * g_ref[...]
    o_ref[0] = (y * (1.0 + sc_ref[0]) + sh_ref[0]).astype(o_ref.dtype)


def _norm_mod(x, g, scale, shift, out_dtype, tm=256):
    bsz, seq, d = x.shape
    return pl.pallas_call(
        _norm_mod_kernel, grid=(bsz, seq // tm),
        in_specs=[pl.BlockSpec((1, tm, d), lambda b, i: (b, i, 0)),
                  pl.BlockSpec((1, d), lambda b, i: (0, 0)),
                  pl.BlockSpec((1, 1, d), lambda b, i: (b, 0, 0)),
                  pl.BlockSpec((1, 1, d), lambda b, i: (b, 0, 0))],
        out_specs=pl.BlockSpec((1, tm, d), lambda b, i: (b, i, 0)),
        out_shape=SDS((bsz, seq, d), out_dtype), compiler_params=_cp("parallel", "parallel"), name="norm_mod",
    )(x, g.reshape(1, d), scale.reshape(bsz, 1, d), shift.reshape(bsz, 1, d))


def _mm_kernel(a_ref, b_ref, o_ref):
    o_ref[...] = _dot(a_ref[...], b_ref[...]).astype(o_ref.dtype)


def _matmul(a, b, out_dtype, tm=512, tn=512):
    m, k = a.shape
    n = b.shape[1]
    tn = min(tn, n)
    return pl.pallas_call(
        _mm_kernel, grid=(m // tm, n // tn),
        in_specs=[pl.BlockSpec((tm, k), lambda i, j: (i, 0)),
                  pl.BlockSpec((k, tn), lambda i, j: (0, j))],
        out_specs=pl.BlockSpec((tm, tn), lambda i, j: (i, j)),
        out_shape=SDS((m, n), out_dtype), compiler_params=_cp("parallel", "parallel"), name="matmul",
    )(a, b)


def _mm_resid_kernel(a_ref, b_ref, r_ref, g_ref, o_ref):
    o_ref[...] = r_ref[...] + g_ref[0] * _dot(a_ref[...], b_ref[...])


def _matmul_resid(a, b, resid, gate, seq, tm=512, tn=512):
    m, k = a.shape
    n = b.shape[1]
    bsz = gate.shape[0]
    return pl.pallas_call(
        _mm_resid_kernel, grid=(m // tm, n // tn),
        in_specs=[pl.BlockSpec((tm, k), lambda i, j: (i, 0)),
                  pl.BlockSpec((k, tn), lambda i, j: (0, j)),
                  pl.BlockSpec((tm, tn), lambda i, j: (i, j)),
                  pl.BlockSpec((1, 1, tn), lambda i, j: (i * tm // seq, 0, j))],
        out_specs=pl.BlockSpec((tm, tn), lambda i, j: (i, j)),
        out_shape=SDS((m, n), F32), compiler_params=_cp("parallel", "parallel"), name="matmul_resid",
    )(a, b, resid, gate.reshape(bsz, 1, n))


def _rwkv_prep_kernel(p_ref, halo_ref, mu_ref, w0_ref, a0_ref, kkw_ref, kaw_ref, w2_ref, a2_ref, g2_ref,
                      r_ref, k_ref, v_ref, kk_ref, a_ref, lw_ref, g_ref):
    x = p_ref[0]
    w = BRANCH_WIDTH
    prev_first = jnp.where(pl.program_id(1) > 0, halo_ref[0][7:8, :], 0.0)
    row = lax.broadcasted_iota(jnp.int32, x.shape, 0)
    prev = jnp.where(row == 0, prev_first, pltpu.roll(x, 1, axis=0))
    p = x + (prev - x) * mu_ref[...]
    r, k, v = p[:, 0:w], p[:, w:2 * w], p[:, 2 * w:3 * w]
    xw, xa, xg = p[:, A_MAIN:A_MAIN + 128], p[:, A_MAIN + 128:A_MAIN + 256], p[:, A_MAIN + 256:A_PAD]
    wlog = -_softplus(-(w0_ref[...] + _dot_hi(jnp.tanh(xw), w2_ref[...]))) - 0.5
    a = _sigmoid(a0_ref[...] + _dot_hi(xa, a2_ref[...]))
    r_ref[0] = r
    k_ref[0] = k * (1.0 + (a - 1.0) * kaw_ref[...])
    v_ref[0] = v
    kk_ref[0] = k * kkw_ref[...]
    a_ref[0] = a
    lw_ref[0] = -jnp.exp(wlog)
    g_ref[0] = _dot_hi(_sigmoid(xg), g2_ref[...])


def _rwkv_prep(pa, mu, w0, a0, kkw, kaw, w2, a2, g2, tm=256):
    bsz, seq, ap = pa.shape
    w = BRANCH_WIDTH
    row = lambda t: t.reshape(1, -1)
    full = lambda shape: pl.BlockSpec(shape, lambda b, i: (0, 0))
    out_spec = pl.BlockSpec((1, tm, w), lambda b, i: (b, i, 0))
    return pl.pallas_call(
        _rwkv_prep_kernel, grid=(bsz, seq // tm),
        in_specs=[pl.BlockSpec((1, tm, ap), lambda b, i: (b, i, 0)),
                  pl.BlockSpec((1, 8, ap), lambda b, i: (b, jnp.maximum(i * (tm // 8) - 1, 0), 0)),
                  full((1, ap)), full((1, w)), full((1, w)), full((1, w)), full((1, w)),
                  full((128, w)), full((128, w)), full((256, w))],
        out_specs=[out_spec] * 7,
        out_shape=[SDS((bsz, seq, w), F32)] * 7,
        compiler_params=_cp("parallel", "parallel"), name="rwkv_prep",
    )(pa, pa, row(mu), row(w0), row(a0), row(kkw), row(kaw), w2, a2, g2)


def _rwkv_rec_kernel(r_ref, k_ref, v_ref, kk_ref, a_ref, lw_ref, rk_ref, lnw_ref, lnb_ref, o_ref, s_ref):
    @pl.when(pl.program_id(1) == 0)
    def _():
        s_ref[...] = jnp.zeros_like(s_ref)

    nh, n = s_ref.shape[0], s_ref.shape[1]
    heads = lambda ref: jnp.stack([ref[0, :, h * n:(h + 1) * n] for h in range(nh)])
    r, k, v, kk, a, lw = (heads(ref) for ref in (r_ref, k_ref, v_ref, kk_ref, a_ref, lw_ref))
    c = r.shape[1]
    kn = kk / jnp.maximum(jnp.sqrt(jnp.sum(kk * kk, axis=-1, keepdims=True)), 1e-12)
    b = kn * a
    row = lax.broadcasted_iota(jnp.int32, (c, c), 0)
    col = lax.broadcasted_iota(jnp.int32, (c, c), 1)
    strict, incl = (row > col)[None], (row >= col)[None]
    bdot = lambda spec: (lambda x, y: jnp.einsum(spec, x.astype(BF16), y.astype(BF16), preferred_element_type=F32))
    nt, nn, tn = bdot('hik,hjk->hij'), bdot('hij,hjk->hik'), bdot('hiv,hik->hvk')
    tril = jnp.broadcast_to((row >= col).astype(BF16)[None], (nh, c, c))
    lw_hi = lw.astype(BF16)
    lw_r1 = lw - lw_hi.astype(F32)
    lw_mid = lw_r1.astype(BF16)
    lw_lo = (lw_r1 - lw_mid.astype(F32)).astype(BF16)
    lc = nn(tril, lw_hi) + (nn(tril, lw_mid) + nn(tril, lw_lo))
    lc_last = lc[:, c - 1:c, :]
    e_neg, e_end = jnp.exp(-lc), jnp.exp(lc_last - lc)
    kt, rt = kn * jnp.exp(lc - lw), r * jnp.exp(lc)
    bt, kd = b * e_neg, k * e_neg
    lm = jnp.where(strict, -nt(kt, bt), 0.0)
    ak = jnp.where(strict, nt(kt, kd), 0.0)
    bb = jnp.where(incl, nt(rt, bt), 0.0)
    bk = jnp.where(incl, nt(rt, kd), 0.0)
    s0 = s_ref[...]
    u = -(nt(kt, s0) + nn(ak, v))
    pw = lm
    n_doublings = c.bit_length() - 1
    for it in range(n_doublings):
        u = u + nn(pw, u)
        if it + 1 < n_doublings:
            pw = nn(pw, pw)
    y = nt(rt, s0) + nn(bb, u) + nn(bk, v)
    s_ref[...] = s0 * jnp.exp(lc_last) + tn(u, b * e_end) + tn(v, k * e_end)
    mean = jnp.mean(y, axis=-1, keepdims=True)
    var = jnp.mean(jnp.square(y - mean), axis=-1, keepdims=True)
    out = (y - mean) * lax.rsqrt(var + RWKV_GN_EPS) * lnw_ref[...] + lnb_ref[...]
    out = out + jnp.sum(r * k * rk_ref[...], axis=-1, keepdims=True) * v
    for h in range(nh):
        o_ref[0, :, h * n:(h + 1) * n] = out[h]


def _rwkv_rec(r, k, v, kk, a, lw, rk, lnw, lnb, chunk=RWKV_CHUNK):
    bsz, seq, w = r.shape
    nh, n = rk.shape
    blk = pl.BlockSpec((1, chunk, w), lambda b, i: (b, i, 0))
    par = pl.BlockSpec((nh, 1, n), lambda b, i: (0, 0, 0))
    return pl.pallas_call(
        _rwkv_rec_kernel, grid=(bsz, seq // chunk),
        in_specs=[blk] * 6 + [par] * 3, out_specs=blk,
        out_shape=SDS((bsz, seq, w), F32),
        scratch_shapes=[pltpu.VMEM((nh, n, n), F32)],
        compiler_params=_cp("parallel", "arbitrary"), name="rwkv_rec",
    )(r, k, v, kk, a, lw, rk.reshape(nh, 1, n), lnw.reshape(nh, 1, n), lnb.reshape(nh, 1, n))


def _pad_rows(w, rows):
    return jnp.zeros((rows, w.shape[1]), w.dtype).at[:w.shape[0]].set(w)


def _rwkv_pack_cols(t):
    lead = t.shape[:-1]
    z = lambda n: jnp.zeros(lead + (n,), t.dtype)
    o1 = A_MAIN + RWKV_DECAY_RANK
    o2 = o1 + RWKV_ICLR_RANK
    return jnp.concatenate([t[..., :A_MAIN], t[..., A_MAIN:o1], z(128 - RWKV_DECAY_RANK), t[..., o1:o2],
                            z(128 - RWKV_ICLR_RANK), t[..., o2:], z(256 - RWKV_GATE_RANK)], axis=-1)


def _rwkv_mixer(pa, mu, w0, w2, a0, a2, g2, kkw, kaw, rk, lnw, lnb):
    bsz, seq, _ = pa.shape
    r, k, v, kk, a, lw, g = _rwkv_prep(pa, _rwkv_pack_cols(mu), w0, a0, kkw, kaw,
                                       _pad_rows(w2, 128), _pad_rows(a2, 128), _pad_rows(g2, 256))
    y = _rwkv_rec(r, k, v, kk, a, lw, rk, lnw.reshape(RWKV_HEADS, RWKV_HEAD), lnb.reshape(RWKV_HEADS, RWKV_HEAD))
    return y, g


def _att_kernel(q_ref, kc_ref, kp_ref, vc_ref, vp_ref, o_ref, lse_ref):
    nt = lambda x, y: lax.dot_general(x, y, (((1,), (1,)), ((), ())), preferred_element_type=F32)
    scale = ATT_HEAD ** -0.5
    qi = lax.broadcasted_iota(jnp.int32, (ATT_BLOCK, ATT_BLOCK), 0)
    ki = lax.broadcasted_iota(jnp.int32, (ATT_BLOCK, ATT_BLOCK), 1)
    cur_ok = ki <= qi
    prev_ok = (ki >= qi) & (pl.program_id(2) > 0)
    for h in range(ATT_HEADS):
        cols = slice(h * ATT_HEAD, (h + 1) * ATT_HEAD)
        q = q_ref[0, :, cols]
        s_c = jnp.where(cur_ok, nt(q, kc_ref[0, :, cols]) * scale, NEG)
        s_p = jnp.where(prev_ok, nt(q, kp_ref[0, :, cols]) * scale, NEG)
        m = jnp.maximum(jnp.max(s_c, axis=-1, keepdims=True), jnp.max(s_p, axis=-1, keepdims=True))
        e_c, e_p = jnp.exp(s_c - m), jnp.exp(s_p - m)
        den = jnp.sum(e_c, axis=-1, keepdims=True) + jnp.sum(e_p, axis=-1, keepdims=True)
        o = _dot(e_c.astype(BF16), vc_ref[0, :, cols]) + _dot(e_p.astype(BF16), vp_ref[0, :, cols])
        o_ref[0, :, cols] = o / den
        lse_ref[0, :, cols] = jnp.broadcast_to(m + jnp.log(den), (ATT_BLOCK, ATT_HEAD))


def _att_group(pb, gi, dilation):
    bsz, seq, cols = pb.shape
    w = ATT_HEADS * ATT_HEAD
    cb = cols // w
    n_phase = seq // dilation
    nblk = n_phase // ATT_BLOCK
    ng = len(ATT_GROUPS)
    pbv = pb.reshape(bsz, n_phase, dilation * cols)
    col = lambda t: (lambda b, ph, i: (b, i, ph * cb + t * ng + gi))
    colp = lambda t: (lambda b, ph, i: (b, jnp.maximum(i - 1, 0), ph * cb + t * ng + gi))
    blk = (1, ATT_BLOCK, w)
    out_spec = pl.BlockSpec(blk, lambda b, ph, i: (b, i, ph))
    out_sds = SDS((bsz, n_phase, dilation * w), F32)
    o, lse = pl.pallas_call(
        _att_kernel, grid=(bsz, dilation, nblk),
        in_specs=[pl.BlockSpec(blk, col(0)), pl.BlockSpec(blk, col(1)), pl.BlockSpec(blk, colp(1)),
                  pl.BlockSpec(blk, col(2)), pl.BlockSpec(blk, colp(2))],
        out_specs=[out_spec, out_spec], out_shape=[out_sds, out_sds],
        compiler_params=_cp("parallel", "parallel", "arbitrary"), name=f"att_g{gi}",
    )(pbv, pbv, pbv, pbv, pbv)
    return o.reshape(bsz, seq, w), lse.reshape(bsz, seq, w)


def _att_combine_kernel(o0, o1, o2, l0, l1, l2, out_ref):
    a0, a1, a2 = l0[...], l1[...], l2[...]
    m = jnp.maximum(jnp.maximum(a0, a1), a2)
    e0, e1, e2 = jnp.exp(a0 - m), jnp.exp(a1 - m), jnp.exp(a2 - m)
    out_ref[...] = ((e0 * o0[...] + e1 * o1[...] + e2 * o2[...]) / (e0 + e1 + e2)).astype(out_ref.dtype)


def _attention_mixer(pb, tm=512):
    bsz, seq, _ = pb.shape
    res = [_att_group(pb, gi, d) for gi, (_, d) in enumerate(ATT_GROUPS)]
    m, w = bsz * seq, ATT_HEADS * ATT_HEAD
    flat = [t[0].reshape(m, w) for t in res] + [t[1].reshape(m, w) for t in res]
    spec = pl.BlockSpec((tm, w), lambda i: (i, 0))
    return pl.pallas_call(
        _att_combine_kernel, grid=(m // tm,), in_specs=[spec] * 6, out_specs=spec,
        out_shape=SDS((m, w), BF16), compiler_params=_cp("parallel"), name="att_combine",
    )(*flat)


def _pool_kernel(x_ref, halo_ref, w_ref, sc_ref, o_ref, ext_ref):
    tm = x_ref.shape[1]
    i = pl.program_id(1)
    x = x_ref[0]
    ext_ref[0:POOL_HALO, :] = jnp.where(i > 0, halo_ref[0], 0.0)
    ext_ref[POOL_HALO:, :] = x
    pos = i * tm + lax.broadcasted_iota(jnp.int32, (tm, POOL_GROUP), 0)
    outs = []
    for gi, win in enumerate(POOL_WINDOWS):
        cols = slice(gi * POOL_GROUP, (gi + 1) * POOL_GROUP)
        xg = x[:, cols]
        s = xg
        for j in range(1, win):
            s = s + ext_ref[pl.ds(POOL_HALO - j, tm), cols]
        mixed = s / jnp.minimum(pos + 1, win).astype(F32) - xg
        outs.append(_dot(mixed.astype(BF16), w_ref[gi]))
    o_ref[0] = (jnp.concatenate(outs, axis=-1) * sc_ref[...]).astype(o_ref.dtype)


def _pool_mixer(pc, w_pool, scale, tm=256):
    bsz, seq, w = pc.shape
    out = pl.pallas_call(
        _pool_kernel, grid=(bsz, seq // tm),
        in_specs=[pl.BlockSpec((1, tm, w), lambda b, i: (b, i, 0)),
                  pl.BlockSpec((1, POOL_HALO, w), lambda b, i: (b, jnp.maximum(i * (tm // POOL_HALO) - 1, 0), 0)),
                  pl.BlockSpec(w_pool.shape, lambda b, i: (0, 0, 0)),
                  pl.BlockSpec((1, w), lambda b, i: (0, 0))],
        out_specs=pl.BlockSpec((1, tm, w), lambda b, i: (b, i, 0)),
        out_shape=SDS((bsz, seq, w), BF16),
        scratch_shapes=[pltpu.VMEM((tm + POOL_HALO, w), F32)],
        compiler_params=_cp("parallel", "parallel"), name="pool",
    )(pc, pc, w_pool.astype(BF16), scale.reshape(1, w))
    return out.reshape(bsz * seq, w)


SSM_HALO = 8


def _ssd_kernel(p_ref, halo_ref, cw_ref, cb_ref, dtb_ref, ah_ref, dsk_ref, nw_ref, e64_ref, e128_ref,
                o_ref, ext_ref, h_ref):
    ci = pl.program_id(1)
    q = SSM_CHUNK
    inner = SSM_INNER
    xbc_lo, xbc_hi = inner, inner + SSM_CONV_DIM

    @pl.when(ci == 0)
    def _():
        h_ref[...] = jnp.zeros_like(h_ref)

    z = p_ref[0, :, 0:inner]
    ext_ref[0:SSM_HALO, :] = jnp.where(ci > 0, halo_ref[0, :, xbc_lo:xbc_hi], 0.0)
    ext_ref[SSM_HALO:, :] = p_ref[0, :, xbc_lo:xbc_hi]
    conv = cb_ref[...]
    for j in range(SSM_CONV):
        conv = conv + cw_ref[j:j + 1, :] * ext_ref[pl.ds(SSM_HALO - (SSM_CONV - 1) + j, q), :]
    xbc = _silu(conv)
    xs = xbc[:, 0:inner]
    dt = _softplus(p_ref[0, :, xbc_hi:xbc_hi + LANES] + dtb_ref[...])
    a = dt * ah_ref[...]
    row = lax.broadcasted_iota(jnp.int32, (q, q), 0)
    col = lax.broadcasted_iota(jnp.int32, (q, q), 1)
    causal = row >= col
    a_cum = _dot_hi(causal.astype(F32), a)
    a_cum_t = a_cum.T
    dt_full = _dot_hi(dt, e64_ref[...])
    acum_full = _dot_hi(a_cum, e64_ref[...])
    alast_full = acum_full[q - 1:q, :]
    acum_b = _dot_hi(a_cum, e128_ref[...])
    xdt = xs * dt_full
    x_to_end = xdt * jnp.exp(alast_full - acum_full)
    exp_ac = jnp.exp(acum_full)
    chunk_dec = jnp.exp(alast_full)
    lane = lax.broadcasted_iota(jnp.int32, (q, LANES), 1)
    first_head = lane < SSM_HEAD
    ys = []
    for g in range(SSM_GROUPS):
        bm = xbc[:, inner + g * SSM_STATE:inner + (g + 1) * SSM_STATE]
        cm = xbc[:, inner + (SSM_GROUPS + g) * SSM_STATE:inner + (SSM_GROUPS + g + 1) * SSM_STATE].astype(BF16)
        bt = bm.T.astype(BF16)
        cb = _dot(cm, bt)
        for pr in range(SSM_HEADS // SSM_GROUPS // 2):
            pi = g * (SSM_HEADS // SSM_GROUPS // 2) + pr
            cols = slice(pi * LANES, (pi + 1) * LANES)
            mats = []
            for hd in (2 * pi, 2 * pi + 1):
                seg = acum_b[:, hd * LANES:(hd + 1) * LANES] - a_cum_t[hd:hd + 1, :]
                mats.append((cb * jnp.exp(jnp.where(causal, seg, NEG))).astype(BF16))
            xp = xdt[:, cols]
            x_blockdiag = jnp.concatenate([jnp.where(first_head, xp, 0.0), jnp.where(first_head, 0.0, xp)], axis=0)
            y_diag = _dot(jnp.concatenate(mats, axis=1), x_blockdiag.astype(BF16))
            h_t = h_ref[pi]
            y_off = _dot(cm, h_t.astype(BF16)) * exp_ac[:, cols]
            h_ref[pi] = h_t * chunk_dec[:, cols] + _dot(bt, x_to_end[:, cols].astype(BF16))
            ys.append(y_diag + y_off)
    y = (jnp.concatenate(ys, axis=-1) + xs * dsk_ref[...]) * _silu(z)
    gsize = inner // SSM_GROUPS
    outs = []
    for g in range(SSM_GROUPS):
        yg = y[:, g * gsize:(g + 1) * gsize]
        outs.append(yg * lax.rsqrt(jnp.mean(yg * yg, axis=-1, keepdims=True) + NORM_EPS))
    o_ref[0] = (jnp.concatenate(outs, axis=-1) * nw_ref[...]).astype(o_ref.dtype)


def _mamba_mixer(pd, conv_w, conv_b, dt_bias, a_log, d_skip, norm_w):
    bsz, seq, dp = pd.shape
    q = SSM_CHUNK
    pad_heads = lambda t: jnp.zeros((1, LANES), F32).at[0, :SSM_HEADS].set(t)
    head_of = lambda width: jnp.arange(SSM_HEADS * width, dtype=jnp.int32)[None, :] // width
    expand = lambda width: (jnp.arange(LANES, dtype=jnp.int32)[:, None] == head_of(width)).astype(F32)
    full2 = lambda shape: pl.BlockSpec(shape, lambda b, i: (0, 0))
    out = pl.pallas_call(
        _ssd_kernel, grid=(bsz, seq // q),
        in_specs=[pl.BlockSpec((1, q, dp), lambda b, i: (b, i, 0)),
                  pl.BlockSpec((1, SSM_HALO, dp), lambda b, i: (b, jnp.maximum(i * (q // SSM_HALO) - 1, 0), 0)),
                  full2((SSM_CONV, SSM_CONV_DIM)), full2((1, SSM_CONV_DIM)), full2((1, LANES)), full2((1, LANES)),
                  full2((1, SSM_INNER)), full2((1, SSM_INNER)),
                  full2((LANES, SSM_HEADS * SSM_HEAD)), full2((LANES, SSM_HEADS * LANES))],
        out_specs=pl.BlockSpec((1, q, SSM_INNER), lambda b, i: (b, i, 0)),
        out_shape=SDS((bsz, seq, SSM_INNER), BF16),
        scratch_shapes=[pltpu.VMEM((q + SSM_HALO, SSM_CONV_DIM), F32),
                        pltpu.VMEM((SSM_HEADS // 2, SSM_STATE, LANES), F32)],
        compiler_params=_cp("parallel", "arbitrary"), name="ssd",
    )(pd, pd, conv_w, conv_b.reshape(1, -1), pad_heads(dt_bias), pad_heads(-jnp.exp(a_log)),
      jnp.repeat(d_skip, SSM_HEAD).reshape(1, -1), norm_w.reshape(1, -1), expand(SSM_HEAD), expand(LANES))
    return out.reshape(bsz * seq, SSM_INNER)


def _merge_kernel(pg_ref, ya_ref, ga_ref, bb_ref, bc_ref, bd_ref, gu_ref, gb_ref, wb_ref, o_ref):
    pg = pg_ref[...]
    branches = ((ya_ref[...] * ga_ref[...]).astype(BF16), bb_ref[...], bc_ref[...], bd_ref[...])
    acc = None
    for bi, br in enumerate(branches):
        term = _sigmoid(_dot(pg, gu_ref[bi]) + gb_ref[bi]) * _dot(br, wb_ref[bi])
        acc = term if acc is None else acc + term
    o_ref[...] = acc.astype(o_ref.dtype)


def _merge(pg, ya, ga, bb, bc, bd, gate_up, gate_b, w_branch, tm=512, tn=512):
    m = pg.shape[0]
    nb, kw, d = w_branch.shape
    rows = lambda k: pl.BlockSpec((tm, k), lambda i, j: (i, 0))
    return pl.pallas_call(
        _merge_kernel, grid=(m // tm, d // tn),
        in_specs=[rows(pg.shape[1]), rows(kw), rows(kw), rows(kw), rows(kw), rows(kw),
                  pl.BlockSpec((nb, pg.shape[1], tn), lambda i, j: (0, 0, j)),
                  pl.BlockSpec((nb, 1, tn), lambda i, j: (0, 0, j)),
                  pl.BlockSpec((nb, kw, tn), lambda i, j: (0, 0, j))],
        out_specs=pl.BlockSpec((tm, tn), lambda i, j: (i, j)),
        out_shape=SDS((m, d), BF16), compiler_params=_cp("parallel", "parallel"), name="merge",
    )(pg, ya, ga, bb, bc, bd, gate_up.astype(BF16), gate_b.reshape(nb, 1, d), w_branch.astype(BF16))


SUBLANES = 8


def _store_slabs(ref, rows):
    w = ref.shape[-1]
    for j in range(SUBLANES):
        ref[:, j, :] = rows[:, j * w:(j + 1) * w]


def _norm_route_kernel(x_ref, g_ref, sc_ref, sh_ref, wr_ref, br_ref, h_ref, ids_ref, wts_ref):
    x = x_ref[0]
    ms = jnp.mean(x * x, axis=-1, keepdims=True)
    h = x * lax.rsqrt(ms + NORM_EPS) * g_ref[...] * (1.0 + sc_ref[0]) + sh_ref[0]
    _store_slabs(h_ref, h)
    lg = _dot_hi(h, wr_ref[...]) + br_ref[...]
    lane = lax.broadcasted_iota(jnp.int32, lg.shape, 1)
    lane_f = lane.astype(F32)
    first = lambda hit: jnp.min(jnp.where(hit, lane_f, float(LANES)), axis=-1, keepdims=True)
    gmask = lane < MOE_GROUPS
    gl = jnp.where(gmask, lg, NEG)
    gmax = jnp.max(gl, axis=-1, keepdims=True)
    gsel = first(gl == gmax)
    gprob = 1.0 / jnp.sum(jnp.where(gmask, jnp.exp(gl - gmax), 0.0), axis=-1, keepdims=True)
    lo = MOE_GROUPS + gsel * MOE_EXPERTS_PER_GROUP
    emask = (lane_f >= lo) & (lane_f < lo + MOE_EXPERTS_PER_GROUP)
    el = jnp.where(emask, lg, NEG)
    v1 = jnp.max(el, axis=-1, keepdims=True)
    i1 = first(el == v1)
    el2 = jnp.where(lane_f == i1, NEG, el)
    v2 = jnp.max(el2, axis=-1, keepdims=True)
    i2 = first((el2 == v2) & emask & (lane_f != i1))
    t = jnp.exp(v2 - v1)
    w1 = gprob / (1.0 + t)
    w2 = gprob * t / (1.0 + t)
    ids_ref[0] = jnp.where(lane == 0, i1, jnp.where(lane == 1, i2, float(MOE_GROUPS))).astype(jnp.int32) - MOE_GROUPS
    wts_ref[0] = jnp.where(lane == 0, w1, jnp.where(lane == 1, w2, 0.0))


def _norm_route(x, g, scale, shift, rg_w, rg_b, re_w, re_b, tm=256):
    bsz, seq, d = x.shape
    n_log = MOE_GROUPS + MOE_EXPERTS
    wr = jnp.zeros((d, LANES), F32).at[:, :n_log].set(jnp.concatenate([rg_w, re_w], axis=1))
    br = jnp.zeros((1, LANES), F32).at[0, :n_log].set(jnp.concatenate([rg_b, re_b]))
    tok = lambda width: pl.BlockSpec((1, tm, width), lambda b, i: (b, i, 0))
    m, dw, nt = bsz * seq, d // SUBLANES, seq // tm
    h, ids, wts = pl.pallas_call(
        _norm_route_kernel, grid=(bsz, nt),
        in_specs=[tok(d), pl.BlockSpec((1, d), lambda b, i: (0, 0)),
                  pl.BlockSpec((1, 1, d), lambda b, i: (b, 0, 0)), pl.BlockSpec((1, 1, d), lambda b, i: (b, 0, 0)),
                  pl.BlockSpec((d, LANES), lambda b, i: (0, 0)), pl.BlockSpec((1, LANES), lambda b, i: (0, 0))],
        out_specs=[pl.BlockSpec((tm, SUBLANES, dw), lambda b, i: (b * nt + i, 0, 0)), tok(LANES), tok(LANES)],
        out_shape=[SDS((m, SUBLANES, dw), F32), SDS((bsz, seq, LANES), jnp.int32), SDS((bsz, seq, LANES), F32)],
        compiler_params=_cp("parallel", "parallel"), name="norm_route",
    )(x, g.reshape(1, d), scale.reshape(bsz, 1, d), shift.reshape(bsz, 1, d), wr, br)
    return h, ids.reshape(m, LANES)[:, :MOE_TOP_K], wts.reshape(m, LANES)[:, :MOE_TOP_K]


ROW_COPY_UNROLL = 8


def _ffn_kernel(be_ref, nb_ref, tok_ref, dst_ref, cnt_ref, h_hbm, wg_ref, wu_ref, wd_ref, sw_ref, y_hbm,
                xbuf, ybuf, gsem, ssem):
    j = pl.program_id(0)
    nb = nb_ref[0]
    cur = j % 2

    def gather_copy(buf, s, tok):
        return pltpu.make_async_copy(h_hbm.at[pl.ds(tok, 1)], xbuf.at[buf, pl.ds(s, 1)], gsem.at[buf])

    def scatter_copy(buf, s, dst):
        return pltpu.make_async_copy(ybuf.at[buf, pl.ds(s, 1)], y_hbm.at[pl.ds(dst, 1)], ssem.at[buf])

    def start_gather(blk, buf):
        def body(s, carry):
            gather_copy(buf, s, tok_ref[blk * MOE_ROWS + s]).start()
            return carry
        lax.fori_loop(0, MOE_ROWS, body, 0, unroll=ROW_COPY_UNROLL)

    def wait_gather(buf):
        def body(s, carry):
            gather_copy(buf, s, 0).wait()
            return carry
        lax.fori_loop(0, MOE_ROWS, body, 0, unroll=ROW_COPY_UNROLL)

    def start_scatter(blk, buf):
        def body(s, carry):
            scatter_copy(buf, s, dst_ref[blk * MOE_ROWS + s]).start()
            return carry
        lax.fori_loop(0, cnt_ref[blk], body, 0)

    def wait_scatter(blk, buf):
        def body(s, carry):
            scatter_copy(buf, s, 0).wait()
            return carry
        lax.fori_loop(0, cnt_ref[blk], body, 0)

    @pl.when(j == 0)
    def _():
        start_gather(0, 0)

    @pl.when(j + 1 < nb)
    def _():
        start_gather(j + 1, 1 - cur)

    @pl.when(j < nb)
    def _():
        wait_gather(cur)

        @pl.when(j >= 2)
        def _():
            wait_scatter(j - 2, cur)

        dw = xbuf.shape[-1]
        gate = up = None
        for c in range(SUBLANES):
            xc = xbuf[cur, :, c, :].astype(BF16)
            gc, uc = _dot(xc, wg_ref[0, c * dw:(c + 1) * dw, :]), _dot(xc, wu_ref[0, c * dw:(c + 1) * dw, :])
            gate, up = (gc, uc) if gate is None else (gate + gc, up + uc)
        hidden = (_silu(gate) * up).astype(BF16)
        _store_slabs(ybuf.at[cur], _dot(hidden, wd_ref[0]) * sw_ref[...])
        start_scatter(j, cur)

    @pl.when(j == nb - 1)
    def _():
        @pl.when(j >= 1)
        def _():
            wait_scatter(j - 1, 1 - cur)
        wait_scatter(j, cur)


def _moe_plan(expert, weight):
    n_tok = expert.shape[0]
    n_assign = n_tok * MOE_TOP_K
    flat_e = expert.reshape(-1)
    order = jnp.argsort(flat_e)
    sorted_e = flat_e[order]
    counts = jnp.bincount(flat_e, length=MOE_EXPERTS)
    padded = (counts + MOE_ROWS - 1) // MOE_ROWS * MOE_ROWS
    pad_end = jnp.cumsum(padded)
    pad_start = pad_end - padded
    start = jnp.cumsum(counts) - counts
    dest = (pad_start[sorted_e] + jnp.arange(n_assign) - start[sorted_e]).astype(jnp.int32)
    n_blocks = n_assign // MOE_ROWS + MOE_EXPERTS
    n_slots = n_blocks * MOE_ROWS
    tok = jnp.repeat(jnp.arange(n_tok, dtype=jnp.int32), MOE_TOP_K)
    slot_tok = jnp.zeros((n_slots,), jnp.int32).at[dest].set(tok[order])
    slot_w = jnp.zeros((n_slots,), F32).at[dest].set(weight.reshape(-1)[order])
    block_e = jnp.minimum(jnp.searchsorted(pad_end, jnp.arange(n_blocks) * MOE_ROWS, side='right'),
                          MOE_EXPERTS - 1).astype(jnp.int32)
    kn = (jnp.arange(n_assign, dtype=jnp.int32) % MOE_TOP_K) * n_tok + tok
    slot_dst = jnp.zeros((n_slots,), jnp.int32).at[dest].set(kn[order])
    block_cnt = jnp.zeros((n_slots,), jnp.int32).at[dest].set(1).reshape(n_blocks, MOE_ROWS).sum(axis=1)
    n_used_blocks = (pad_end[-1:] // MOE_ROWS).astype(jnp.int32)
    return slot_tok, slot_w, block_e, slot_dst, block_cnt.astype(jnp.int32), n_used_blocks


def _moe_apply(h, expert, weight, w_gate, w_up, w_down):
    n_tok, _, dw = h.shape
    d = dw * SUBLANES
    slot_tok, slot_w, block_e, slot_dst, block_cnt, n_used_blocks = _moe_plan(expert, weight)
    n_slots = slot_tok.shape[0]
    used = lambda j, nb: jnp.minimum(j, nb[0] - 1)
    expert_blk = lambda shape: pl.BlockSpec(shape, lambda j, be, nb, *_: (be[used(j, nb)], 0, 0))
    any_space = pl.BlockSpec(memory_space=pl.ANY)
    return pl.pallas_call(
        _ffn_kernel,
        grid_spec=pltpu.PrefetchScalarGridSpec(
            num_scalar_prefetch=5, grid=(n_slots // MOE_ROWS,),
            in_specs=[any_space, expert_blk((1, d, MOE_FF)), expert_blk((1, d, MOE_FF)), expert_blk((1, MOE_FF, d)),
                      pl.BlockSpec((MOE_ROWS, 1), lambda j, be, nb, *_: (used(j, nb), 0))],
            out_specs=any_space,
            scratch_shapes=[pltpu.VMEM((2, MOE_ROWS, SUBLANES, dw), F32), pltpu.VMEM((2, MOE_ROWS, SUBLANES, dw), F32),
                            pltpu.SemaphoreType.DMA((2,)), pltpu.SemaphoreType.DMA((2,))]),
        out_shape=SDS((MOE_TOP_K * n_tok, SUBLANES, dw), F32), compiler_params=_cp("arbitrary"), name="moe_ffn",
    )(block_e, n_used_blocks, slot_tok, slot_dst, block_cnt, h, w_gate.astype(BF16), w_up.astype(BF16),
      w_down.astype(BF16), slot_w.reshape(n_slots, 1))


def _moe_combine_kernel(x_ref, y0_ref, y1_ref, g_ref, o_ref):
    dw = y0_ref.shape[-1]
    for j in range(SUBLANES):
        cols = slice(j * dw, (j + 1) * dw)
        o_ref[0, :, cols] = x_ref[0, :, cols] + g_ref[0, :, cols] * (y0_ref[0, 0, :, j, :] + y1_ref[0, 0, :, j, :])


def _moe_combine(x, y2, gate, tm=256):
    bsz, seq, d = x.shape
    dw = d // SUBLANES
    y5 = y2.reshape(MOE_TOP_K, bsz, seq, SUBLANES, dw)
    tok = pl.BlockSpec((1, tm, d), lambda b, i: (b, i, 0))
    slab = lambda k: pl.BlockSpec((1, 1, tm, SUBLANES, dw), lambda b, i: (k, b, i, 0, 0))
    return pl.pallas_call(
        _moe_combine_kernel, grid=(bsz, seq // tm),
        in_specs=[tok, slab(0), slab(1), pl.BlockSpec((1, 1, d), lambda b, i: (b, 0, 0))],
        out_specs=tok, out_shape=SDS((bsz, seq, d), F32),
        compiler_params=_cp("parallel", "parallel"), name="moe_combine",
    )(x, y5, y5, gate.reshape(bsz, 1, d))


def kernel(x, c, ada_w, ada_b, ada_table, norm1_g, norm2_g, w_in, rwkv_mu, rwkv_w0, rwkv_w2, rwkv_a0, rwkv_a2,
           rwkv_g2, rwkv_kk, rwkv_ka, rwkv_rk, rwkv_lnx_w, rwkv_lnx_b, pool_w, pool_scale, ssm_conv_w, ssm_conv_b,
           ssm_dt_bias, ssm_a_log, ssm_d, ssm_norm_w, gate_up, gate_b, w_branch, w_out, router_group_w,
           router_group_b, router_expert_w, router_expert_b, exp_w_gate, exp_w_up, exp_w_down, final_g):
    bsz, seq, d = x.shape
    m = bsz * seq
    a_cols = rwkv_mu.shape[1]
    b_cols = 3 * len(ATT_GROUPS) * ATT_HEADS * ATT_HEAD
    d_cols = SSM_INNER + SSM_CONV_DIM + SSM_HEADS
    offs = (0, a_cols, a_cols + b_cols, a_cols + b_cols + BRANCH_WIDTH, a_cols + b_cols + BRANCH_WIDTH + d_cols)
    cond = _cond(c, ada_w, ada_b)
    for l in range(DEPTH):
        shift1, scale1, gate1, shift2, scale2, gate2 = jnp.split(cond + ada_table[l], 6, axis=-1)
        h = _norm_mod(x, norm1_g[l], scale1, shift1, BF16).reshape(m, d)
        w_l = w_in[l]
        wa = _rwkv_pack_cols(w_l[:, offs[0]:offs[1]]).astype(BF16)
        wb = w_l[:, offs[1]:offs[2]].astype(BF16)
        wc = w_l[:, offs[2]:offs[3]].astype(BF16)
        wd = jnp.zeros((d, D_PAD), BF16).at[:, :d_cols].set(w_l[:, offs[3]:offs[4]].astype(BF16))
        wg = w_l[:, offs[4]:].astype(BF16)
        pa = _matmul(h, wa, F32).reshape(bsz, seq, A_PAD)
        pb = _matmul(h, wb, BF16).reshape(bsz, seq, b_cols)
        pc = _matmul(h, wc, F32).reshape(bsz, seq, BRANCH_WIDTH)
        pd = _matmul(h, wd, F32, tn=640).reshape(bsz, seq, D_PAD)
        pg = _matmul(h, wg, BF16)
        ya, ga = _rwkv_mixer(pa, rwkv_mu[l], rwkv_w0[l], rwkv_w2[l], rwkv_a0[l], rwkv_a2[l], rwkv_g2[l],
                             rwkv_kk[l], rwkv_ka[l], rwkv_rk[l], rwkv_lnx_w[l], rwkv_lnx_b[l])
        ob = _attention_mixer(pb)
        oc = _pool_mixer(pc, pool_w[l], pool_scale[l])
        od = _mamba_mixer(pd, ssm_conv_w[l], ssm_conv_b[l], ssm_dt_bias[l], ssm_a_log[l], ssm_d[l], ssm_norm_w[l])
        merged = _merge(pg, ya.reshape(m, BRANCH_WIDTH), ga.reshape(m, BRANCH_WIDTH), ob, oc, od,
                        gate_up[l], gate_b[l], w_branch[l])
        x = _matmul_resid(merged, w_out[l].astype(BF16), x.reshape(m, d), gate1, seq).reshape(bsz, seq, d)
        h2, expert, weight = _norm_route(x, norm2_g[l], scale2, shift2, router_group_w[l], router_group_b[l],
                                         router_expert_w[l], router_expert_b[l])
        y2 = _moe_apply(h2, expert, weight, exp_w_gate[l], exp_w_up[l], exp_w_down[l])
        x = _moe_combine(x, y2, gate2)
    zeros = jnp.zeros((bsz, d), F32)
    return _norm_mod(x, final_g, zeros, zeros, F32)
```

```python
import functools

import jax
import jax.numpy as jnp
from jax import lax
from jax.experimental import pallas as pl
from jax.experimental.pallas import tpu as pltpu

F32 = jnp.float32
BF16 = jnp.bfloat16
HI = lax.Precision.HIGHEST
SDS = jax.ShapeDtypeStruct

DEPTH = 2
NORM_EPS = 1e-6
BRANCH_WIDTH = 1024
RWKV_HEAD = 64
RWKV_HEADS = 16
RWKV_DECAY_RANK = 64
RWKV_ICLR_RANK = 64
RWKV_GATE_RANK = 160
RWKV_GN_EPS = 64e-5
RWKV_CHUNK = 64
ATT_HEAD = 128
ATT_GROUPS = ((128, 1), (512, 4), (2048, 16))
ATT_HEADS = 8
ATT_BLOCK = 128
POOL_WINDOWS = (2, 4, 8, 16)
POOL_GROUP = 256
POOL_HALO = 16
SSM_INNER = 1024
SSM_HEAD = 64
SSM_HEADS = 16
SSM_GROUPS = 4
SSM_STATE = 128
SSM_CONV = 4
SSM_CHUNK = 128
SSM_CONV_DIM = SSM_INNER + 2 * SSM_GROUPS * SSM_STATE
GATE_RANK = 256
MOE_GROUPS = 4
MOE_EXPERTS_PER_GROUP = 8
MOE_EXPERTS = 32
MOE_TOP_K = 2
MOE_FF = 512
MOE_ROWS = 256
LANES = 128
NEG = -1e30
A_MAIN = 3 * BRANCH_WIDTH
A_PAD = A_MAIN + 128 + 128 + 256
D_PAD = SSM_INNER + SSM_CONV_DIM + LANES
VMEM_LIMIT = 56 * 1024 * 1024


def _cp(*sem):
    return pltpu.CompilerParams(dimension_semantics=sem, vmem_limit_bytes=VMEM_LIMIT)


def _sigmoid(x):
    return 1.0 / (1.0 + jnp.exp(-x))


def _silu(x):
    return x * _sigmoid(x)


def _softplus(x):
    return jnp.maximum(x, 0.0) + jnp.log(1.0 + jnp.exp(-jnp.abs(x)))


def _dot(a, b):
    return jnp.dot(a, b, preferred_element_type=F32)


def _dot_hi(a, b):
    return jnp.dot(a, b, precision=HI, preferred_element_type=F32)


def _cond_kernel(c_ref, w_ref, b_ref, o_ref):
    a = _silu(c_ref[...]).astype(BF16)
    o_ref[...] = _dot(a, w_ref[...].astype(BF16)) + b_ref[...]


def _cond(c, ada_w, ada_b):
    bsz, d = c.shape
    n = ada_w.shape[1]
    tn = 512
    cp = jnp.zeros((8, d), F32).at[:bsz].set(c)
    out = pl.pallas_call(
        _cond_kernel, grid=(n // tn,),
        in_specs=[pl.BlockSpec((8, d), lambda j: (0, 0)),
                  pl.BlockSpec((d, tn), lambda j: (0, j)),
                  pl.BlockSpec((1, tn), lambda j: (0, j))],
        out_specs=pl.BlockSpec((8, tn), lambda j: (0, j)),
        out_shape=SDS((8, n), F32), compiler_params=_cp("parallel"), name="cond",
    )(cp, ada_w, ada_b.reshape(1, n))
    return out[:bsz]


def _norm_mod_kernel(x_ref, g_ref, sc_ref, sh_ref, o_ref):
    x = x_ref[0]
    ms = jnp.mean(x * x, axis=-1, keepdims=True)
    y = x * lax.rsqrt(ms + NORM_EPS) * g_ref[...]
    o_ref[0] = (y * (1.0 + sc_ref[0]) + sh_ref[0]).astype(o_ref.dtype)


def _norm_mod(x, g, scale, shift, out_dtype, tm=256):
    bsz, seq, d = x.shape
    return pl.pallas_call(
        _norm_mod_kernel, grid=(bsz, seq // tm),
        in_specs=[pl.BlockSpec((1, tm, d), lambda b, i: (b, i, 0)),
                  pl.BlockSpec((1, d), lambda b, i: (0, 0)),
                  pl.BlockSpec((1, 1, d), lambda b, i: (b, 0, 0)),
                  pl.BlockSpec((1, 1, d), lambda b, i: (b, 0, 0))],
        out_specs=pl.BlockSpec((1, tm, d), lambda b, i: (b, i, 0)),
        out_shape=SDS((bsz, seq, d), out_dtype), compiler_params=_cp("parallel", "parallel"), name="norm_mod",
    )(x, g.reshape(1, d), scale.reshape(bsz, 1, d), shift.reshape(bsz, 1, d))


def _norm_mod_phases_kernel(x_ref, g_ref, sc_ref, sh_ref, o_ref, *refs):
    phase_refs, scr = refs[:-1], refs[-1]
    x = x_ref[0]
    tm, d = x.shape
    ms = jnp.mean(x * x, axis=-1, keepdims=True)
    y = x * lax.rsqrt(ms + NORM_EPS) * g_ref[...] * (1.0 + sc_ref[0]) + sh_ref[0]
    o_ref[0] = y.astype(o_ref.dtype)
    for ref in phase_refs:
        dil = ref.shape[1]
        n = tm // dil
        y3 = y.reshape(n, dil, d)
        for ph in range(dil):
            scr[ph * n:(ph + 1) * n, :] = y3[:, ph, :]
        for ph in range(dil):
            ref[0, ph] = scr[ph * n:(ph + 1) * n, :].astype(ref.dtype)


def _norm_mod_phases(x, g, scale, shift, dilations, tm=256):
    bsz, seq, d = x.shape
    tok = pl.BlockSpec((1, tm, d), lambda b, i: (b, i, 0))
    vec = pl.BlockSpec((1, 1, d), lambda b, i: (b, 0, 0))
    return pl.pallas_call(
        _norm_mod_phases_kernel, grid=(bsz, seq // tm),
        in_specs=[tok, pl.BlockSpec((1, d), lambda b, i: (0, 0)), vec, vec],
        out_specs=[tok] + [pl.BlockSpec((1, dil, tm // dil, d), lambda b, i: (b, 0, i, 0)) for dil in dilations],
        out_shape=[SDS((bsz, seq, d), BF16)] + [SDS((bsz, dil, seq // dil, d), BF16) for dil in dilations],
        scratch_shapes=[pltpu.VMEM((tm, d), F32)],
        compiler_params=_cp("parallel", "parallel"), name="norm_mod_phases",
    )(x, g.reshape(1, d), scale.reshape(bsz, 1, d), shift.reshape(bsz, 1, d))


def _mm_kernel(a_ref, b_ref, o_ref):
    o_ref[...] = _dot(a_ref[...], b_ref[...]).astype(o_ref.dtype)


def _matmul(a, b, out_dtype, tm=512, tn=512):
    m, k = a.shape
    n = b.shape[1]
    tn = min(tn, n)
    return pl.pallas_call(
        _mm_kernel, grid=(m // tm, n // tn),
        in_specs=[pl.BlockSpec((tm, k), lambda i, j: (i, 0)),
                  pl.BlockSpec((k, tn), lambda i, j: (0, j))],
        out_specs=pl.BlockSpec((tm, tn), lambda i, j: (i, j)),
        out_shape=SDS((m, n), out_dtype), compiler_params=_cp("parallel", "parallel"), name="matmul",
    )(a, b)


def _mm_resid_kernel(a_ref, b_ref, r_ref, g_ref, o_ref):
    o_ref[...] = r_ref[...] + g_ref[0] * _dot(a_ref[...], b_ref[...])


def _matmul_resid(a, b, resid, gate, seq, tm=512, tn=512):
    m, k = a.shape
    n = b.shape[1]
    bsz = gate.shape[0]
    return pl.pallas_call(
        _mm_resid_kernel, grid=(m // tm, n // tn),
        in_specs=[pl.BlockSpec((tm, k), lambda i, j: (i, 0)),
                  pl.BlockSpec((k, tn), lambda i, j: (0, j)),
                  pl.BlockSpec((tm, tn), lambda i, j: (i, j)),
                  pl.BlockSpec((1, 1, tn), lambda i, j: (i * tm // seq, 0, j))],
        out_specs=pl.BlockSpec((tm, tn), lambda i, j: (i, j)),
        out_shape=SDS((m, n), F32), compiler_params=_cp("parallel", "parallel"), name="matmul_resid",
    )(a, b, resid, gate.reshape(bsz, 1, n))


def _rwkv_prep_kernel(p_ref, halo_ref, mu_ref, w0_ref, a0_ref, kkw_ref, kaw_ref, w2_ref, a2_ref, g2_ref,
                      r_ref, k_ref, v_ref, kk_ref, a_ref, lw_ref, g_ref):
    x = p_ref[0]
    w = BRANCH_WIDTH
    prev_first = jnp.where(pl.program_id(1) > 0, halo_ref[0][7:8, :], 0.0)
    row = lax.broadcasted_iota(jnp.int32, x.shape, 0)
    prev = jnp.where(row == 0, prev_first, pltpu.roll(x, 1, axis=0))
    p = x + (prev - x) * mu_ref[...]
    r, k, v = p[:, 0:w], p[:, w:2 * w], p[:, 2 * w:3 * w]
    xw, xa, xg = p[:, A_MAIN:A_MAIN + 128], p[:, A_MAIN + 128:A_MAIN + 256], p[:, A_MAIN + 256:A_PAD]
    wlog = -_softplus(-(w0_ref[...] + _dot_hi(jnp.tanh(xw), w2_ref[...]))) - 0.5
    a = _sigmoid(a0_ref[...] + _dot_hi(xa, a2_ref[...]))
    r_ref[0] = r
    k_ref[0] = k * (1.0 + (a - 1.0) * kaw_ref[...])
    v_ref[0] = v
    kk_ref[0] = k * kkw_ref[...]
    a_ref[0] = a
    lw_ref[0] = -jnp.exp(wlog)
    g_ref[0] = _dot_hi(_sigmoid(xg), g2_ref[...])


def _rwkv_prep(pa, mu, w0, a0, kkw, kaw, w2, a2, g2, tm=256):
    bsz, seq, ap = pa.shape
    w = BRANCH_WIDTH
    row = lambda t: t.reshape(1, -1)
    full = lambda shape: pl.BlockSpec(shape, lambda b, i: (0, 0))
    out_spec = pl.BlockSpec((1, tm, w), lambda b, i: (b, i, 0))
    return pl.pallas_call(
        _rwkv_prep_kernel, grid=(bsz, seq // tm),
        in_specs=[pl.BlockSpec((1, tm, ap), lambda b, i: (b, i, 0)),
                  pl.BlockSpec((1, 8, ap), lambda b, i: (b, jnp.maximum(i * (tm // 8) - 1, 0), 0)),
                  full((1, ap)), full((1, w)), full((1, w)), full((1, w)), full((1, w)),
                  full((128, w)), full((128, w)), full((256, w))],
        out_specs=[out_spec] * 7,
        out_shape=[SDS((bsz, seq, w), F32)] * 7,
        compiler_params=_cp("parallel", "parallel"), name="rwkv_prep",
    )(pa, pa, row(mu), row(w0), row(a0), row(kkw), row(kaw), w2, a2, g2)


def _rwkv_rec_kernel(r_ref, k_ref, v_ref, kk_ref, a_ref, lw_ref, rk_ref, lnw_ref, lnb_ref, o_ref, s_ref):
    @pl.when(pl.program_id(1) == 0)
    def _():
        s_ref[...] = jnp.zeros_like(s_ref)

    nh, n = s_ref.shape[0], s_ref.shape[1]
    heads = lambda ref: jnp.stack([ref[0, :, h * n:(h + 1) * n] for h in range(nh)])
    r, k, v, kk, a, lw = (heads(ref) for ref in (r_ref, k_ref, v_ref, kk_ref, a_ref, lw_ref))
    c = r.shape[1]
    kn = kk / jnp.maximum(jnp.sqrt(jnp.sum(kk * kk, axis=-1, keepdims=True)), 1e-12)
    b = kn * a
    row = lax.broadcasted_iota(jnp.int32, (c, c), 0)
    col = lax.broadcasted_iota(jnp.int32, (c, c), 1)
    strict, incl = (row > col)[None], (row >= col)[None]
    bdot = lambda spec: (lambda x, y: jnp.einsum(spec, x.astype(BF16), y.astype(BF16), preferred_element_type=F32))
    nt, nn, tn = bdot('hik,hjk->hij'), bdot('hij,hjk->hik'), bdot('hiv,hik->hvk')
    tril = jnp.broadcast_to((row >= col).astype(BF16)[None], (nh, c, c))
    lw_hi = lw.astype(BF16)
    lw_r1 = lw - lw_hi.astype(F32)
    lw_mid = lw_r1.astype(BF16)
    lw_lo = (lw_r1 - lw_mid.astype(F32)).astype(BF16)
    lc = nn(tril, lw_hi) + (nn(tril, lw_mid) + nn(tril, lw_lo))
    lc_last = lc[:, c - 1:c, :]
    e_neg, e_end = jnp.exp(-lc), jnp.exp(lc_last - lc)
    kt, rt = kn * jnp.exp(lc - lw), r * jnp.exp(lc)
    bt, kd = b * e_neg, k * e_neg
    lm = jnp.where(strict, -nt(kt, bt), 0.0)
    ak = jnp.where(strict, nt(kt, kd), 0.0)
    bb = jnp.where(incl, nt(rt, bt), 0.0)
    bk = jnp.where(incl, nt(rt, kd), 0.0)
    s0 = s_ref[...]
    u = -(nt(kt, s0) + nn(ak, v))
    pw = lm
    n_doublings = c.bit_length() - 1
    for it in range(n_doublings):
        u = u + nn(pw, u)
        if it + 1 < n_doublings:
            pw = nn(pw, pw)
    y = nt(rt, s0) + nn(bb, u) + nn(bk, v)
    s_ref[...] = s0 * jnp.exp(lc_last) + tn(u, b * e_end) + tn(v, k * e_end)
    mean = jnp.mean(y, axis=-1, keepdims=True)
    var = jnp.mean(jnp.square(y - mean), axis=-1, keepdims=True)
    out = (y - mean) * lax.rsqrt(var + RWKV_GN_EPS) * lnw_ref[...] + lnb_ref[...]
    out = out + jnp.sum(r * k * rk_ref[...], axis=-1, keepdims=True) * v
    for h in range(nh):
        o_ref[0, :, h * n:(h + 1) * n] = out[h]


def _rwkv_rec(r, k, v, kk, a, lw, rk, lnw, lnb, chunk=RWKV_CHUNK):
    bsz, seq, w = r.shape
    nh, n = rk.shape
    blk = pl.BlockSpec((1, chunk, w), lambda b, i: (b, i, 0))
    par = pl.BlockSpec((nh, 1, n), lambda b, i: (0, 0, 0))
    return pl.pallas_call(
        _rwkv_rec_kernel, grid=(bsz, seq // chunk),
        in_specs=[blk] * 6 + [par] * 3, out_specs=blk,
        out_shape=SDS((bsz, seq, w), F32),
        scratch_shapes=[pltpu.VMEM((nh, n, n), F32)],
        compiler_params=_cp("parallel", "arbitrary"), name="rwkv_rec",
    )(r, k, v, kk, a, lw, rk.reshape(nh, 1, n), lnw.reshape(nh, 1, n), lnb.reshape(nh, 1, n))


def _pad_rows(w, rows):
    return jnp.zeros((rows, w.shape[1]), w.dtype).at[:w.shape[0]].set(w)


def _rwkv_pack_cols(t):
    lead = t.shape[:-1]
    z = lambda n: jnp.zeros(lead + (n,), t.dtype)
    o1 = A_MAIN + RWKV_DECAY_RANK
    o2 = o1 + RWKV_ICLR_RANK
    return jnp.concatenate([t[..., :A_MAIN], t[..., A_MAIN:o1], z(128 - RWKV_DECAY_RANK), t[..., o1:o2],
                            z(128 - RWKV_ICLR_RANK), t[..., o2:], z(256 - RWKV_GATE_RANK)], axis=-1)


def _rwkv_mixer(pa, mu, w0, w2, a0, a2, g2, kkw, kaw, rk, lnw, lnb):
    bsz, seq, _ = pa.shape
    r, k, v, kk, a, lw, g = _rwkv_prep(pa, _rwkv_pack_cols(mu), w0, a0, kkw, kaw,
                                       _pad_rows(w2, 128), _pad_rows(a2, 128), _pad_rows(g2, 256))
    y = _rwkv_rec(r, k, v, kk, a, lw, rk, lnw.reshape(RWKV_HEADS, RWKV_HEAD), lnb.reshape(RWKV_HEADS, RWKV_HEAD))
    return y, g


def _att_kernel(q_ref, kc_ref, kp_ref, vc_ref, vp_ref, o_ref, lse_ref):
    nt = lambda x, y: lax.dot_general(x, y, (((1,), (1,)), ((), ())), preferred_element_type=F32)
    scale = ATT_HEAD ** -0.5
    qi = lax.broadcasted_iota(jnp.int32, (ATT_BLOCK, ATT_BLOCK), 0)
    ki = lax.broadcasted_iota(jnp.int32, (ATT_BLOCK, ATT_BLOCK), 1)
    cur_ok = ki <= qi
    prev_ok = (ki >= qi) & (pl.program_id(2) > 0)
    lane = lax.broadcasted_iota(jnp.int32, (ATT_BLOCK, LANES), 1)
    lse_all = jnp.zeros((ATT_BLOCK, LANES), F32)
    for h in range(ATT_HEADS):
        cols = slice(h * ATT_HEAD, (h + 1) * ATT_HEAD)
        q = q_ref[0, 0, :, cols]
        s_c = jnp.where(cur_ok, nt(q, kc_ref[0, 0, :, cols]) * scale, NEG)
        s_p = jnp.where(prev_ok, nt(q, kp_ref[0, 0, :, cols]) * scale, NEG)
        m = jnp.maximum(jnp.max(s_c, axis=-1, keepdims=True), jnp.max(s_p, axis=-1, keepdims=True))
        e_c, e_p = jnp.exp(s_c - m), jnp.exp(s_p - m)
        den = jnp.sum(e_c, axis=-1, keepdims=True) + jnp.sum(e_p, axis=-1, keepdims=True)
        o = _dot(e_c.astype(BF16), vc_ref[0, 0, :, cols]) + _dot(e_p.astype(BF16), vp_ref[0, 0, :, cols])
        o_ref[0, 0, :, cols] = o / den
        lse_all = jnp.where(lane == h, m + jnp.log(den), lse_all)
    lse_ref[0, 0] = lse_all


def _att_group(pb, gi):
    bsz, dilation, n_phase, _ = pb.shape
    w = ATT_HEADS * ATT_HEAD
    nblk = n_phase // ATT_BLOCK
    blk = (1, 1, ATT_BLOCK, w)
    cur = lambda t: pl.BlockSpec(blk, lambda b, ph, i: (b, ph, i, t))
    prev = lambda t: pl.BlockSpec(blk, lambda b, ph, i: (b, ph, jnp.maximum(i - 1, 0), t))
    return pl.pallas_call(
        _att_kernel, grid=(bsz, dilation, nblk),
        in_specs=[cur(0), cur(1), prev(1), cur(2), prev(2)],
        out_specs=[pl.BlockSpec(blk, lambda b, ph, i: (b, ph, i, 0)),
                   pl.BlockSpec((1, 1, ATT_BLOCK, LANES), lambda b, ph, i: (b, ph, i, 0))],
        out_shape=[SDS((bsz, dilation, n_phase, w), F32), SDS((bsz, dilation, n_phase, LANES), F32)],
        compiler_params=_cp("parallel", "parallel", "arbitrary"), name=f"att_g{gi}",
    )(pb, pb, pb, pb, pb)


def _att_combine_kernel(o0, o1, o2, l0, l1, l2, e_ref, out_ref):
    tm = out_ref.shape[0]

    def tokens(ref):
        d = ref.shape[1]
        if d == 1:
            return ref[0, 0]
        return jnp.stack([ref[0, ph] for ph in range(d)], axis=1).reshape(tm, ref.shape[-1])

    lses = [tokens(l) for l in (l0, l1, l2)]
    m = jnp.maximum(jnp.maximum(lses[0], lses[1]), lses[2])
    es = [jnp.exp(l - m) for l in lses]
    inv = 1.0 / (es[0] + es[1] + es[2])
    acc = None
    for e, o in zip(es, (o0, o1, o2)):
        term = _dot_hi(e * inv, e_ref[...]) * tokens(o)
        acc = term if acc is None else acc + term
    out_ref[...] = acc.astype(out_ref.dtype)


def _attention_mixer(pbs, tm=256):
    res = [_att_group(pb, gi) for gi, pb in enumerate(pbs)]
    bsz, _, seq, w = res[0][0].shape
    nt = seq // tm
    spec = lambda t: pl.BlockSpec((1, t.shape[1], tm // t.shape[1], t.shape[3]), lambda b, i: (b, 0, i, 0))
    expand = (jnp.arange(LANES, dtype=jnp.int32)[:, None] == jnp.arange(w, dtype=jnp.int32)[None, :] // ATT_HEAD).astype(F32)
    args = [t[0] for t in res] + [t[1] for t in res]
    return pl.pallas_call(
        _att_combine_kernel, grid=(bsz, nt),
        in_specs=[spec(t) for t in args] + [pl.BlockSpec((LANES, w), lambda b, i: (0, 0))],
        out_specs=pl.BlockSpec((tm, w), lambda b, i: (b * nt + i, 0)),
        out_shape=SDS((bsz * seq, w), BF16), compiler_params=_cp("parallel", "parallel"), name="att_combine",
    )(*args, expand)


def _pool_kernel(x_ref, halo_ref, w_ref, sc_ref, o_ref, ext_ref):
    tm = x_ref.shape[1]
    i = pl.program_id(1)
    x = x_ref[0]
    ext_ref[0:POOL_HALO, :] = jnp.where(i > 0, halo_ref[0], 0.0)
    ext_ref[POOL_HALO:, :] = x
    pos = i * tm + lax.broadcasted_iota(jnp.int32, (tm, POOL_GROUP), 0)
    outs = []
    for gi, win in enumerate(POOL_WINDOWS):
        cols = slice(gi * POOL_GROUP, (gi + 1) * POOL_GROUP)
        xg = x[:, cols]
        s = xg
        for j in range(1, win):
            s = s + ext_ref[pl.ds(POOL_HALO - j, tm), cols]
        mixed = s / jnp.minimum(pos + 1, win).astype(F32) - xg
        outs.append(_dot(mixed.astype(BF16), w_ref[gi]))
    o_ref[0] = (jnp.concatenate(outs, axis=-1) * sc_ref[...]).astype(o_ref.dtype)


def _pool_mixer(pc, w_pool, scale, tm=256):
    bsz, seq, w = pc.shape
    out = pl.pallas_call(
        _pool_kernel, grid=(bsz, seq // tm),
        in_specs=[pl.BlockSpec((1, tm, w), lambda b, i: (b, i, 0)),
                  pl.BlockSpec((1, POOL_HALO, w), lambda b, i: (b, jnp.maximum(i * (tm // POOL_HALO) - 1, 0), 0)),
                  pl.BlockSpec(w_pool.shape, lambda b, i: (0, 0, 0)),
                  pl.BlockSpec((1, w), lambda b, i: (0, 0))],
        out_specs=pl.BlockSpec((1, tm, w), lambda b, i: (b, i, 0)),
        out_shape=SDS((bsz, seq, w), BF16),
        scratch_shapes=[pltpu.VMEM((tm + POOL_HALO, w), F32)],
        compiler_params=_cp("parallel", "parallel"), name="pool",
    )(pc, pc, w_pool.astype(BF16), scale.reshape(1, w))
    return out.reshape(bsz * seq, w)


SSM_HALO = 8


def _ssd_kernel(p_ref, halo_ref, cw_ref, cb_ref, dtb_ref, ah_ref, dsk_ref, nw_ref, e64_ref, e128_ref,
                o_ref, ext_ref, h_ref):
    ci = pl.program_id(1)
    q = SSM_CHUNK
    inner = SSM_INNER
    xbc_lo, xbc_hi = inner, inner + SSM_CONV_DIM

    @pl.when(ci == 0)
    def _():
        h_ref[...] = jnp.zeros_like(h_ref)

    z = p_ref[0, :, 0:inner]
    ext_ref[0:SSM_HALO, :] = jnp.where(ci > 0, halo_ref[0, :, xbc_lo:xbc_hi], 0.0)
    ext_ref[SSM_HALO:, :] = p_ref[0, :, xbc_lo:xbc_hi]
    conv = cb_ref[...]
    for j in range(SSM_CONV):
        conv = conv + cw_ref[j:j + 1, :] * ext_ref[pl.ds(SSM_HALO - (SSM_CONV - 1) + j, q), :]
    xbc = _silu(conv)
    xs = xbc[:, 0:inner]
    dt = _softplus(p_ref[0, :, xbc_hi:xbc_hi + LANES] + dtb_ref[...])
    a = dt * ah_ref[...]
    row = lax.broadcasted_iota(jnp.int32, (q, q), 0)
    col = lax.broadcasted_iota(jnp.int32, (q, q), 1)
    causal = row >= col
    a_cum = _dot_hi(causal.astype(F32), a)
    a_cum_t = a_cum.T
    dt_full = _dot_hi(dt, e64_ref[...])
    acum_full = _dot_hi(a_cum, e64_ref[...])
    alast_full = acum_full[q - 1:q, :]
    acum_b = _dot_hi(a_cum, e128_ref[...])
    xdt = xs * dt_full
    x_to_end = xdt * jnp.exp(alast_full - acum_full)
    exp_ac = jnp.exp(acum_full)
    chunk_dec = jnp.exp(alast_full)
    lane = lax.broadcasted_iota(jnp.int32, (q, LANES), 1)
    first_head = lane < SSM_HEAD
    ys = []
    for g in range(SSM_GROUPS):
        bm = xbc[:, inner + g * SSM_STATE:inner + (g + 1) * SSM_STATE]
        cm = xbc[:, inner + (SSM_GROUPS + g) * SSM_STATE:inner + (SSM_GROUPS + g + 1) * SSM_STATE].astype(BF16)
        bt = bm.T.astype(BF16)
        cb = _dot(cm, bt)
        for pr in range(SSM_HEADS // SSM_GROUPS // 2):
            pi = g * (SSM_HEADS // SSM_GROUPS // 2) + pr
            cols = slice(pi * LANES, (pi + 1) * LANES)
            mats = []
            for hd in (2 * pi, 2 * pi + 1):
                seg = acum_b[:, hd * LANES:(hd + 1) * LANES] - a_cum_t[hd:hd + 1, :]
                mats.append((cb * jnp.exp(jnp.where(causal, seg, NEG))).astype(BF16))
            xp = xdt[:, cols]
            x_blockdiag = jnp.concatenate([jnp.where(first_head, xp, 0.0), jnp.where(first_head, 0.0, xp)], axis=0)
            y_diag = _dot(jnp.concatenate(mats, axis=1), x_blockdiag.astype(BF16))
            h_t = h_ref[pi]
            y_off = _dot(cm, h_t.astype(BF16)) * exp_ac[:, cols]
            h_ref[pi] = h_t * chunk_dec[:, cols] + _dot(bt, x_to_end[:, cols].astype(BF16))
            ys.append(y_diag + y_off)
    y = (jnp.concatenate(ys, axis=-1) + xs * dsk_ref[...]) * _silu(z)
    gsize = inner // SSM_GROUPS
    outs = []
    for g in range(SSM_GROUPS):
        yg = y[:, g * gsize:(g + 1) * gsize]
        outs.append(yg * lax.rsqrt(jnp.mean(yg * yg, axis=-1, keepdims=True) + NORM_EPS))
    o_ref[0] = (jnp.concatenate(outs, axis=-1) * nw_ref[...]).astype(o_ref.dtype)


def _mamba_mixer(pd, conv_w, conv_b, dt_bias, a_log, d_skip, norm_w):
    bsz, seq, dp = pd.shape
    q = SSM_CHUNK
    pad_heads = lambda t: jnp.zeros((1, LANES), F32).at[0, :SSM_HEADS].set(t)
    head_of = lambda width: jnp.arange(SSM_HEADS * width, dtype=jnp.int32)[None, :] // width
    expand = lambda width: (jnp.arange(LANES, dtype=jnp.int32)[:, None] == head_of(width)).astype(F32)
    full2 = lambda shape: pl.BlockSpec(shape, lambda b, i: (0, 0))
    out = pl.pallas_call(
        _ssd_kernel, grid=(bsz, seq // q),
        in_specs=[pl.BlockSpec((1, q, dp), lambda b, i: (b, i, 0)),
                  pl.BlockSpec((1, SSM_HALO, dp), lambda b, i: (b, jnp.maximum(i * (q // SSM_HALO) - 1, 0), 0)),
                  full2((SSM_CONV, SSM_CONV_DIM)), full2((1, SSM_CONV_DIM)), full2((1, LANES)), full2((1, LANES)),
                  full2((1, SSM_INNER)), full2((1, SSM_INNER)),
                  full2((LANES, SSM_HEADS * SSM_HEAD)), full2((LANES, SSM_HEADS * LANES))],
        out_specs=pl.BlockSpec((1, q, SSM_INNER), lambda b, i: (b, i, 0)),
        out_shape=SDS((bsz, seq, SSM_INNER), BF16),
        scratch_shapes=[pltpu.VMEM((q + SSM_HALO, SSM_CONV_DIM), F32),
                        pltpu.VMEM((SSM_HEADS // 2, SSM_STATE, LANES), F32)],
        compiler_params=_cp("parallel", "arbitrary"), name="ssd",
    )(pd, pd, conv_w, conv_b.reshape(1, -1), pad_heads(dt_bias), pad_heads(-jnp.exp(a_log)),
      jnp.repeat(d_skip, SSM_HEAD).reshape(1, -1), norm_w.reshape(1, -1), expand(SSM_HEAD), expand(LANES))
    return out.reshape(bsz * seq, SSM_INNER)


def _merge_kernel(pg_ref, ya_ref, ga_ref, bb_ref, bc_ref, bd_ref, gu_ref, gb_ref, wb_ref, o_ref):
    pg = pg_ref[...]
    branches = ((ya_ref[...] * ga_ref[...]).astype(BF16), bb_ref[...], bc_ref[...], bd_ref[...])
    acc = None
    for bi, br in enumerate(branches):
        term = _sigmoid(_dot(pg, gu_ref[bi]) + gb_ref[bi]) * _dot(br, wb_ref[bi])
        acc = term if acc is None else acc + term
    o_ref[...] = acc.astype(o_ref.dtype)


def _merge(pg, ya, ga, bb, bc, bd, gate_up, gate_b, w_branch, tm=512, tn=512):
    m = pg.shape[0]
    nb, kw, d = w_branch.shape
    rows = lambda k: pl.BlockSpec((tm, k), lambda i, j: (i, 0))
    return pl.pallas_call(
        _merge_kernel, grid=(m // tm, d // tn),
        in_specs=[rows(pg.shape[1]), rows(kw), rows(kw), rows(kw), rows(kw), rows(kw),
                  pl.BlockSpec((nb, pg.shape[1], tn), lambda i, j: (0, 0, j)),
                  pl.BlockSpec((nb, 1, tn), lambda i, j: (0, 0, j)),
                  pl.BlockSpec((nb, kw, tn), lambda i, j: (0, 0, j))],
        out_specs=pl.BlockSpec((tm, tn), lambda i, j: (i, j)),
        out_shape=SDS((m, d), BF16), compiler_params=_cp("parallel", "parallel"), name="merge",
    )(pg, ya, ga, bb, bc, bd, gate_up.astype(BF16), gate_b.reshape(nb, 1, d), w_branch.astype(BF16))


SUBLANES = 8


def _store_slabs(ref, rows):
    w = ref.shape[-1]
    for j in range(SUBLANES):
        ref[:, j, :] = rows[:, j * w:(j + 1) * w]


def _norm_route_kernel(x_ref, g_ref, sc_ref, sh_ref, wr_ref, br_ref, h_ref, ids_ref, wts_ref):
    x = x_ref[0]
    ms = jnp.mean(x * x, axis=-1, keepdims=True)
    h = x * lax.rsqrt(ms + NORM_EPS) * g_ref[...] * (1.0 + sc_ref[0]) + sh_ref[0]
    _store_slabs(h_ref, h)
    lg = _dot_hi(h, wr_ref[...]) + br_ref[...]
    lane = lax.broadcasted_iota(jnp.int32, lg.shape, 1)
    lane_f = lane.astype(F32)
    first = lambda hit: jnp.min(jnp.where(hit, lane_f, float(LANES)), axis=-1, keepdims=True)
    gmask = lane < MOE_GROUPS
    gl = jnp.where(gmask, lg, NEG)
    gmax = jnp.max(gl, axis=-1, keepdims=True)
    gsel = first(gl == gmax)
    gprob = 1.0 / jnp.sum(jnp.where(gmask, jnp.exp(gl - gmax), 0.0), axis=-1, keepdims=True)
    lo = MOE_GROUPS + gsel * MOE_EXPERTS_PER_GROUP
    emask = (lane_f >= lo) & (lane_f < lo + MOE_EXPERTS_PER_GROUP)
    el = jnp.where(emask, lg, NEG)
    v1 = jnp.max(el, axis=-1, keepdims=True)
    i1 = first(el == v1)
    el2 = jnp.where(lane_f == i1, NEG, el)
    v2 = jnp.max(el2, axis=-1, keepdims=True)
    i2 = first((el2 == v2) & emask & (lane_f != i1))
    t = jnp.exp(v2 - v1)
    w1 = gprob / (1.0 + t)
    w2 = gprob * t / (1.0 + t)
    ids_ref[0] = jnp.where(lane == 0, i1, jnp.where(lane == 1, i2, float(MOE_GROUPS))).astype(jnp.int32) - MOE_GROUPS
    wts_ref[0] = jnp.where(lane == 0, w1, jnp.where(lane == 1, w2, 0.0))


def _norm_route(x, g, scale, shift, rg_w, rg_b, re_w, re_b, tm=256):
    bsz, seq, d = x.shape
    n_log = MOE_GROUPS + MOE_EXPERTS
    wr = jnp.zeros((d, LANES), F32).at[:, :n_log].set(jnp.concatenate([rg_w, re_w], axis=1))
    br = jnp.zeros((1, LANES), F32).at[0, :n_log].set(jnp.concatenate([rg_b, re_b]))
    tok = lambda width: pl.BlockSpec((1, tm, width), lambda b, i: (b, i, 0))
    m, dw, nt = bsz * seq, d // SUBLANES, seq // tm
    h, ids, wts = pl.pallas_call(
        _norm_route_kernel, grid=(bsz, nt),
        in_specs=[tok(d), pl.BlockSpec((1, d), lambda b, i: (0, 0)),
                  pl.BlockSpec((1, 1, d), lambda b, i: (b, 0, 0)), pl.BlockSpec((1, 1, d), lambda b, i: (b, 0, 0)),
                  pl.BlockSpec((d, LANES), lambda b, i: (0, 0)), pl.BlockSpec((1, LANES), lambda b, i: (0, 0))],
        out_specs=[pl.BlockSpec((tm, SUBLANES, dw), lambda b, i: (b * nt + i, 0, 0)), tok(LANES), tok(LANES)],
        out_shape=[SDS((m, SUBLANES, dw), F32), SDS((bsz, seq, LANES), jnp.int32), SDS((bsz, seq, LANES), F32)],
        compiler_params=_cp("parallel", "parallel"), name="norm_route",
    )(x, g.reshape(1, d), scale.reshape(bsz, 1, d), shift.reshape(bsz, 1, d), wr, br)
    return h, ids.reshape(m, LANES)[:, :MOE_TOP_K], wts.reshape(m, LANES)[:, :MOE_TOP_K]


ROW_COPY_UNROLL = 8


def _ffn_kernel(be_ref, nb_ref, tok_ref, dst_ref, cnt_ref, h_hbm, wg_ref, wu_ref, wd_ref, sw_ref, y_hbm,
                xbuf, ybuf, xrow, gsem, ssem):
    j = pl.program_id(0)
    nb = nb_ref[0]
    cur = j % 2

    def gather_copy(buf, s, tok):
        return pltpu.make_async_copy(h_hbm.at[pl.ds(tok, 1)], xbuf.at[buf, pl.ds(s, 1)], gsem.at[buf])

    def scatter_copy(buf, s, dst):
        return pltpu.make_async_copy(ybuf.at[buf, pl.ds(s, 1)], y_hbm.at[pl.ds(dst, 1)], ssem.at[buf])

    def start_gather(blk, buf):
        def body(s, carry):
            gather_copy(buf, s, tok_ref[blk * MOE_ROWS + s]).start()
            return carry
        lax.fori_loop(0, MOE_ROWS, body, 0, unroll=ROW_COPY_UNROLL)

    def wait_gather(buf):
        def body(s, carry):
            gather_copy(buf, s, 0).wait()
            return carry
        lax.fori_loop(0, MOE_ROWS, body, 0, unroll=ROW_COPY_UNROLL)

    def start_scatter(blk, buf):
        def body(s, carry):
            scatter_copy(buf, s, dst_ref[blk * MOE_ROWS + s]).start()
            return carry
        lax.fori_loop(0, cnt_ref[blk], body, 0)

    def wait_scatter(blk, buf):
        def body(s, carry):
            scatter_copy(buf, s, 0).wait()
            return carry
        lax.fori_loop(0, cnt_ref[blk], body, 0)

    @pl.when(j == 0)
    def _():
        start_gather(0, 0)

    @pl.when(j + 1 < nb)
    def _():
        start_gather(j + 1, 1 - cur)

    @pl.when(j < nb)
    def _():
        wait_gather(cur)

        @pl.when(j >= 2)
        def _():
            wait_scatter(j - 2, cur)

        dw = xbuf.shape[-1]
        for c in range(SUBLANES):
            xrow[:, c * dw:(c + 1) * dw] = xbuf[cur, :, c, :]
        x = xrow[...].astype(BF16)
        hidden = (_silu(_dot(x, wg_ref[0])) * _dot(x, wu_ref[0])).astype(BF16)
        _store_slabs(ybuf.at[cur], _dot(hidden, wd_ref[0]) * sw_ref[...])
        start_scatter(j, cur)

    @pl.when(j == nb - 1)
    def _():
        @pl.when(j >= 1)
        def _():
            wait_scatter(j - 1, 1 - cur)
        wait_scatter(j, cur)


def _moe_plan(expert, weight):
    n_tok = expert.shape[0]
    n_assign = n_tok * MOE_TOP_K
    flat_e = expert.reshape(-1)
    flat_w = weight.reshape(-1)
    order = jnp.argsort(flat_e).astype(jnp.int32)
    counts = jnp.sum((flat_e[:, None] == jnp.arange(MOE_EXPERTS, dtype=flat_e.dtype)[None, :]).astype(jnp.int32), axis=0)
    padded = (counts + MOE_ROWS - 1) // MOE_ROWS * MOE_ROWS
    pad_end = jnp.cumsum(padded)
    pad_start = pad_end - padded
    start = jnp.cumsum(counts) - counts
    n_blocks = n_assign // MOE_ROWS + MOE_EXPERTS
    n_slots = n_blocks * MOE_ROWS
    blk0 = jnp.arange(n_blocks, dtype=jnp.int32) * MOE_ROWS
    block_e = jnp.minimum(jnp.searchsorted(pad_end, blk0, side='right'), MOE_EXPERTS - 1).astype(jnp.int32)
    block_cnt = jnp.clip(counts[block_e] - (blk0 - pad_start[block_e]), 0, MOE_ROWS).astype(jnp.int32)
    slot_e = jnp.repeat(block_e, MOE_ROWS)
    pos = jnp.arange(n_slots, dtype=jnp.int32) - pad_start[slot_e]
    valid = pos < counts[slot_e]
    assign = order[jnp.clip(start[slot_e] + pos, 0, n_assign - 1)]
    tok, k = assign // MOE_TOP_K, assign % MOE_TOP_K
    slot_tok = jnp.where(valid, tok, 0).astype(jnp.int32)
    slot_w = jnp.where(valid, flat_w[assign], 0.0)
    slot_dst = jnp.where(valid, k * n_tok + tok, 0).astype(jnp.int32)
    n_used_blocks = (pad_end[-1:] // MOE_ROWS).astype(jnp.int32)
    return slot_tok, slot_w, block_e, slot_dst, block_cnt, n_used_blocks


def _cast_kernel(x_ref, o_ref):
    o_ref[...] = x_ref[...].astype(o_ref.dtype)


def _cast_bf16(w, block_bytes=2 * 1024 * 1024):
    e, r, c = w.shape
    tr = min(r, block_bytes // (4 * c))
    spec = pl.BlockSpec((1, tr, c), lambda i, j: (i, j, 0))
    return pl.pallas_call(_cast_kernel, grid=(e, r // tr), in_specs=[spec], out_specs=spec,
                          out_shape=SDS(w.shape, BF16), compiler_params=_cp("parallel", "parallel"), name="cast_bf16")(w)


def _moe_apply(h, expert, weight, w_gate, w_up, w_down):
    n_tok, _, dw = h.shape
    d = dw * SUBLANES
    slot_tok, slot_w, block_e, slot_dst, block_cnt, n_used_blocks = _moe_plan(expert, weight)
    n_slots = slot_tok.shape[0]
    used = lambda j, nb: jnp.minimum(j, nb[0] - 1)
    expert_blk = lambda shape: pl.BlockSpec(shape, lambda j, be, nb, *_: (be[used(j, nb)], 0, 0))
    any_space = pl.BlockSpec(memory_space=pl.ANY)
    return pl.pallas_call(
        _ffn_kernel,
        grid_spec=pltpu.PrefetchScalarGridSpec(
            num_scalar_prefetch=5, grid=(n_slots // MOE_ROWS,),
            in_specs=[any_space, expert_blk((1, d, MOE_FF)), expert_blk((1, d, MOE_FF)), expert_blk((1, MOE_FF, d)),
                      pl.BlockSpec((MOE_ROWS, 1), lambda j, be, nb, *_: (used(j, nb), 0))],
            out_specs=any_space,
            scratch_shapes=[pltpu.VMEM((2, MOE_ROWS, SUBLANES, dw), F32), pltpu.VMEM((2, MOE_ROWS, SUBLANES, dw), F32),
                            pltpu.VMEM((MOE_ROWS, d), F32),
                            pltpu.SemaphoreType.DMA((2,)), pltpu.SemaphoreType.DMA((2,))]),
        out_shape=SDS((MOE_TOP_K * n_tok, SUBLANES, dw), F32), compiler_params=_cp("arbitrary"), name="moe_ffn",
    )(block_e, n_used_blocks, slot_tok, slot_dst, block_cnt, h, _cast_bf16(w_gate), _cast_bf16(w_up),
      _cast_bf16(w_down), slot_w.reshape(n_slots, 1))


def _moe_combine_kernel(x_ref, y0_ref, y1_ref, g_ref, o_ref):
    dw = y0_ref.shape[-1]
    for j in range(SUBLANES):
        cols = slice(j * dw, (j + 1) * dw)
        o_ref[0, :, cols] = x_ref[0, :, cols] + g_ref[0, :, cols] * (y0_ref[0, 0, :, j, :] + y1_ref[0, 0, :, j, :])


def _moe_combine(x, y2, gate, tm=256):
    bsz, seq, d = x.shape
    dw = d // SUBLANES
    y5 = y2.reshape(MOE_TOP_K, bsz, seq, SUBLANES, dw)
    tok = pl.BlockSpec((1, tm, d), lambda b, i: (b, i, 0))
    slab = lambda k: pl.BlockSpec((1, 1, tm, SUBLANES, dw), lambda b, i: (k, b, i, 0, 0))
    return pl.pallas_call(
        _moe_combine_kernel, grid=(bsz, seq // tm),
        in_specs=[tok, slab(0), slab(1), pl.BlockSpec((1, 1, d), lambda b, i: (b, 0, 0))],
        out_specs=tok, out_shape=SDS((bsz, seq, d), F32),
        compiler_params=_cp("parallel", "parallel"), name="moe_combine",
    )(x, y5, y5, gate.reshape(bsz, 1, d))


PACK_ROWS = 128


def _pack_w_in_kernel(offs, w_ref, a_ref, b0_ref, b1_ref, b2_ref, c_ref, d_ref, g_ref):
    rows = w_ref.shape[0]
    ng, wh = len(ATT_GROUPS), ATT_HEADS * ATT_HEAD
    zeros = lambda n: jnp.zeros((rows, n), BF16)
    cast = lambda lo, hi: w_ref[:, lo:hi].astype(BF16)
    o_w = offs[0] + A_MAIN
    o_a = o_w + RWKV_DECAY_RANK
    o_g = o_a + RWKV_ICLR_RANK
    a_ref[...] = jnp.concatenate(
        [cast(offs[0], o_w), cast(o_w, o_a), zeros(128 - RWKV_DECAY_RANK), cast(o_a, o_g), zeros(128 - RWKV_ICLR_RANK),
         cast(o_g, offs[1]), zeros(256 - RWKV_GATE_RANK)], axis=1)
    for gi, b_ref in enumerate((b0_ref, b1_ref, b2_ref)):
        b_ref[...] = jnp.concatenate(
            [cast(offs[1] + (t * ng + gi) * wh, offs[1] + (t * ng + gi + 1) * wh) for t in range(3)], axis=1)
    c_ref[...] = cast(offs[2], offs[3])
    d_ref[...] = jnp.concatenate([cast(offs[3], offs[4]), zeros(D_PAD - (offs[4] - offs[3]))], axis=1)
    g_ref[...] = cast(offs[4], w_ref.shape[1])


def _pack_w_in(w, offs):
    d, n = w.shape
    wb = (offs[2] - offs[1]) // len(ATT_GROUPS)
    widths = (A_PAD, wb, wb, wb, offs[3] - offs[2], D_PAD, n - offs[4])
    return pl.pallas_call(
        functools.partial(_pack_w_in_kernel, offs), grid=(d // PACK_ROWS,),
        in_specs=[pl.BlockSpec((PACK_ROWS, n), lambda i: (i, 0))],
        out_specs=[pl.BlockSpec((PACK_ROWS, wd), lambda i: (i, 0)) for wd in widths],
        out_shape=[SDS((d, wd), BF16) for wd in widths], compiler_params=_cp("parallel"), name="pack_w_in",
    )(w)


def kernel(x, c, ada_w, ada_b, ada_table, norm1_g, norm2_g, w_in, rwkv_mu, rwkv_w0, rwkv_w2, rwkv_a0, rwkv_a2,
           rwkv_g2, rwkv_kk, rwkv_ka, rwkv_rk, rwkv_lnx_w, rwkv_lnx_b, pool_w, pool_scale, ssm_conv_w, ssm_conv_b,
           ssm_dt_bias, ssm_a_log, ssm_d, ssm_norm_w, gate_up, gate_b, w_branch, w_out, router_group_w,
           router_group_b, router_expert_w, router_expert_b, exp_w_gate, exp_w_up, exp_w_down, final_g):
    bsz, seq, d = x.shape
    m = bsz * seq
    a_cols = rwkv_mu.shape[1]
    b_cols = 3 * len(ATT_GROUPS) * ATT_HEADS * ATT_HEAD
    d_cols = SSM_INNER + SSM_CONV_DIM + SSM_HEADS
    offs = (0, a_cols, a_cols + b_cols, a_cols + b_cols + BRANCH_WIDTH, a_cols + b_cols + BRANCH_WIDTH + d_cols)
    cond = _cond(c, ada_w, ada_b)
    for l in range(DEPTH):
        shift1, scale1, gate1, shift2, scale2, gate2 = jnp.split(cond + ada_table[l], 6, axis=-1)
        dilations = tuple(dil for _, dil in ATT_GROUPS)
        h, *h_phases = _norm_mod_phases(x, norm1_g[l], scale1, shift1, [dil for dil in dilations if dil > 1])
        h = h.reshape(m, d)
        h_of = {1: h, **{dil: hp.reshape(m, d) for dil, hp in zip([dil for dil in dilations if dil > 1], h_phases)}}
        wa, wb0, wb1, wb2, wc, wd, wg = _pack_w_in(w_in[l], offs)
        pa = _matmul(h, wa, F32).reshape(bsz, seq, A_PAD)
        pbs = [_matmul(h_of[dil], wb, BF16).reshape(bsz, dil, seq // dil, -1)
               for dil, wb in zip(dilations, (wb0, wb1, wb2))]
        pc = _matmul(h, wc, F32).reshape(bsz, seq, BRANCH_WIDTH)
        pd = _matmul(h, wd, F32, tn=640).reshape(bsz, seq, D_PAD)
        pg = _matmul(h, wg, BF16)
        ya, ga = _rwkv_mixer(pa, rwkv_mu[l], rwkv_w0[l], rwkv_w2[l], rwkv_a0[l], rwkv_a2[l], rwkv_g2[l],
                             rwkv_kk[l], rwkv_ka[l], rwkv_rk[l], rwkv_lnx_w[l], rwkv_lnx_b[l])
        ob = _attention_mixer(pbs)
        oc = _pool_mixer(pc, pool_w[l], pool_scale[l])
        od = _mamba_mixer(pd, ssm_conv_w[l], ssm_conv_b[l], ssm_dt_bias[l], ssm_a_log[l], ssm_d[l], ssm_norm_w[l])
        merged = _merge(pg, ya.reshape(m, BRANCH_WIDTH), ga.reshape(m, BRANCH_WIDTH), ob, oc, od,
                        gate_up[l], gate_b[l], w_branch[l])
        x = _matmul_resid(merged, w_out[l].astype(BF16), x.reshape(m, d), gate1, seq).reshape(bsz, seq, d)
        h2, expert, weight = _norm_route(x, norm2_g[l], scale2, shift2, router_group_w[l], router_group_b[l],
                                         router_expert_w[l], router_expert_b[l])
        y2 = _moe_apply(h2, expert, weight, exp_w_gate[l], exp_w_up[l], exp_w_down[l])
        x = _moe_combine(x, y2, gate2)
    zeros = jnp.zeros((bsz, d), F32)
    return _norm_mod(x, final_g, zeros, zeros, F32)
```

```python
import functools

import jax
import jax.numpy as jnp
from jax import lax
from jax.experimental import pallas as pl
from jax.experimental.pallas import tpu as pltpu

F32 = jnp.float32
BF16 = jnp.bfloat16
HI = lax.Precision.HIGHEST
SDS = jax.ShapeDtypeStruct

DEPTH = 2
NORM_EPS = 1e-6
BRANCH_WIDTH = 1024
RWKV_HEAD = 64
RWKV_HEADS = 16
RWKV_DECAY_RANK = 64
RWKV_ICLR_RANK = 64
RWKV_GATE_RANK = 160
RWKV_GN_EPS = 64e-5
RWKV_CHUNK = 64
ATT_HEAD = 128
ATT_GROUPS = ((128, 1), (512, 4), (2048, 16))
ATT_HEADS = 8
ATT_BLOCK = 128
POOL_WINDOWS = (2, 4, 8, 16)
POOL_GROUP = 256
POOL_HALO = 16
SSM_INNER = 1024
SSM_HEAD = 64
SSM_HEADS = 16
SSM_GROUPS = 4
SSM_STATE = 128
SSM_CONV = 4
SSM_CHUNK = 128
SSM_CONV_DIM = SSM_INNER + 2 * SSM_GROUPS * SSM_STATE
GATE_RANK = 256
MOE_GROUPS = 4
MOE_EXPERTS_PER_GROUP = 8
MOE_EXPERTS = 32
MOE_TOP_K = 2
MOE_FF = 512
MOE_ROWS = 256
LANES = 128
NEG = -1e30
A_MAIN = 3 * BRANCH_WIDTH
A_PAD = A_MAIN + 128 + 128 + 256
D_PAD = SSM_INNER + SSM_CONV_DIM + LANES
VMEM_LIMIT = 56 * 1024 * 1024


def _cp(*sem):
    return pltpu.CompilerParams(dimension_semantics=sem, vmem_limit_bytes=VMEM_LIMIT)


def _sigmoid(x):
    return 1.0 / (1.0 + jnp.exp(-x))


def _silu(x):
    return x * _sigmoid(x)


def _softplus(x):
    return jnp.maximum(x, 0.0) + jnp.log(1.0 + jnp.exp(-jnp.abs(x)))


def _dot(a, b):
    return jnp.dot(a, b, preferred_element_type=F32)


def _dot_hi(a, b):
    return jnp.dot(a, b, precision=HI, preferred_element_type=F32)


def _cond_kernel(c_ref, w_ref, b_ref, o_ref, a_ref):
    @pl.when(pl.program_id(0) == 0)
    def _():
        a_ref[...] = _silu(c_ref[...])

    k, tn = w_ref.shape
    w = w_ref[...]
    rows = []
    for r in range(a_ref.shape[0]):
        prod = w * jnp.concatenate([a_ref[r]] * (tn // LANES), axis=1)
        part = jnp.sum(prod.reshape(k // 8, 8, tn), axis=0)
        rows.append(jnp.sum(part, axis=0, keepdims=True))
    rows.append(jnp.zeros((o_ref.shape[0] - len(rows), tn), F32))
    o_ref[...] = jnp.concatenate(rows, axis=0) + b_ref[...]


def _cond(c, ada_w, ada_b):
    bsz, d = c.shape
    n = ada_w.shape[1]
    tn = 512
    c_lanes = jnp.broadcast_to(c[:, :, None], (bsz, d, LANES))
    out = pl.pallas_call(
        _cond_kernel, grid=(n // tn,),
        in_specs=[pl.BlockSpec((bsz, d, LANES), lambda j: (0, 0, 0)),
                  pl.BlockSpec((d, tn), lambda j: (0, j)),
                  pl.BlockSpec((1, tn), lambda j: (0, j))],
        out_specs=pl.BlockSpec((8, tn), lambda j: (0, j)),
        out_shape=SDS((8, n), F32), scratch_shapes=[pltpu.VMEM((bsz, d, LANES), F32)],
        compiler_params=_cp("arbitrary"), name="cond",
    )(c_lanes, ada_w, ada_b.reshape(1, n))
    return out[:bsz]


def _norm_mod_kernel(x_ref, g_ref, sc_ref, sh_ref, o_ref):
    x = x_ref[0]
    ms = jnp.mean(x * x, axis=-1, keepdims=True)
    y = x * lax.rsqrt(ms + NORM_EPS) * g_ref[...]
    o_ref[0] = (y * (1.0 + sc_ref[0]) + sh_ref[0]).astype(o_ref.dtype)


def _norm_mod(x, g, scale, shift, out_dtype, tm=256):
    bsz, seq, d = x.shape
    return pl.pallas_call(
        _norm_mod_kernel, grid=(bsz, seq // tm),
        in_specs=[pl.BlockSpec((1, tm, d), lambda b, i: (b, i, 0)),
                  pl.BlockSpec((1, d), lambda b, i: (0, 0)),
                  pl.BlockSpec((1, 1, d), lambda b, i: (b, 0, 0)),
                  pl.BlockSpec((1, 1, d), lambda b, i: (b, 0, 0))],
        out_specs=pl.BlockSpec((1, tm, d), lambda b, i: (b, i, 0)),
        out_shape=SDS((bsz, seq, d), out_dtype), compiler_params=_cp("parallel", "parallel"), name="norm_mod",
    )(x, g.reshape(1, d), scale.reshape(bsz, 1, d), shift.reshape(bsz, 1, d))


def _norm_mod_phases_kernel(x_ref, g_ref, sc_ref, sh_ref, o_ref, *refs):
    phase_refs, scr = refs[:-1], refs[-1]
    x = x_ref[0]
    tm, d = x.shape
    ms = jnp.mean(x * x, axis=-1, keepdims=True)
    y = x * lax.rsqrt(ms + NORM_EPS) * g_ref[...] * (1.0 + sc_ref[0]) + sh_ref[0]
    o_ref[0] = y.astype(o_ref.dtype)
    for ref in phase_refs:
        dil = ref.shape[1]
        n = tm // dil
        y3 = y.reshape(n, dil, d)
        for ph in range(dil):
            scr[ph * n:(ph + 1) * n, :] = y3[:, ph, :]
        for ph in range(dil):
            ref[0, ph] = scr[ph * n:(ph + 1) * n, :].astype(ref.dtype)


def _norm_mod_phases(x, g, scale, shift, dilations, tm=256):
    bsz, seq, d = x.shape
    tok = pl.BlockSpec((1, tm, d), lambda b, i: (b, i, 0))
    vec = pl.BlockSpec((1, 1, d), lambda b, i: (b, 0, 0))
    return pl.pallas_call(
        _norm_mod_phases_kernel, grid=(bsz, seq // tm),
        in_specs=[tok, pl.BlockSpec((1, d), lambda b, i: (0, 0)), vec, vec],
        out_specs=[tok] + [pl.BlockSpec((1, dil, tm // dil, d), lambda b, i: (b, 0, i, 0)) for dil in dilations],
        out_shape=[SDS((bsz, seq, d), BF16)] + [SDS((bsz, dil, seq // dil, d), BF16) for dil in dilations],
        scratch_shapes=[pltpu.VMEM((tm, d), F32)],
        compiler_params=_cp("parallel", "parallel"), name="norm_mod_phases",
    )(x, g.reshape(1, d), scale.reshape(bsz, 1, d), shift.reshape(bsz, 1, d))


def _mm_kernel(a_ref, b_ref, o_ref):
    o_ref[...] = _dot(a_ref[...], b_ref[...]).astype(o_ref.dtype)


def _matmul(a, b, out_dtype, tm=1024, tn=512):
    m, k = a.shape
    n = b.shape[1]
    tn = min(tn, n)
    return pl.pallas_call(
        _mm_kernel, grid=(m // tm, n // tn),
        in_specs=[pl.BlockSpec((tm, k), lambda i, j: (i, 0)),
                  pl.BlockSpec((k, tn), lambda i, j: (0, j))],
        out_specs=pl.BlockSpec((tm, tn), lambda i, j: (i, j)),
        out_shape=SDS((m, n), out_dtype), compiler_params=_cp("parallel", "parallel"), name="matmul",
    )(a, b)


def _mm_resid_kernel(a_ref, b_ref, r_ref, g_ref, o_ref):
    o_ref[...] = r_ref[...] + g_ref[0] * _dot(a_ref[...], b_ref[...])


def _matmul_resid(a, b, resid, gate, seq, tm=1024, tn=512):
    m, k = a.shape
    n = b.shape[1]
    bsz = gate.shape[0]
    return pl.pallas_call(
        _mm_resid_kernel, grid=(m // tm, n // tn),
        in_specs=[pl.BlockSpec((tm, k), lambda i, j: (i, 0)),
                  pl.BlockSpec((k, tn), lambda i, j: (0, j)),
                  pl.BlockSpec((tm, tn), lambda i, j: (i, j)),
                  pl.BlockSpec((1, 1, tn), lambda i, j: (i * tm // seq, 0, j))],
        out_specs=pl.BlockSpec((tm, tn), lambda i, j: (i, j)),
        out_shape=SDS((m, n), F32), compiler_params=_cp("parallel", "parallel"), name="matmul_resid",
    )(a, b, resid, gate.reshape(bsz, 1, n))


def _rwkv_prep_kernel(p_ref, halo_ref, mu_ref, w0_ref, a0_ref, kkw_ref, kaw_ref, w2_ref, a2_ref, g2_ref,
                      r_ref, k_ref, v_ref, kk_ref, a_ref, lw_ref, g_ref):
    x = p_ref[0]
    w = BRANCH_WIDTH
    prev_first = jnp.where(pl.program_id(1) > 0, halo_ref[0][7:8, :], 0.0)
    row = lax.broadcasted_iota(jnp.int32, x.shape, 0)
    prev = jnp.where(row == 0, prev_first, pltpu.roll(x, 1, axis=0))
    p = x + (prev - x) * mu_ref[...]
    r, k, v = p[:, 0:w], p[:, w:2 * w], p[:, 2 * w:3 * w]
    xw, xa, xg = p[:, A_MAIN:A_MAIN + 128], p[:, A_MAIN + 128:A_MAIN + 256], p[:, A_MAIN + 256:A_PAD]
    wlog = -_softplus(-(w0_ref[...] + _dot_hi(jnp.tanh(xw), w2_ref[...]))) - 0.5
    a = _sigmoid(a0_ref[...] + _dot_hi(xa, a2_ref[...]))
    r_ref[0] = r
    k_ref[0] = k * (1.0 + (a - 1.0) * kaw_ref[...])
    v_ref[0] = v
    kk_ref[0] = k * kkw_ref[...]
    a_ref[0] = a
    lw_ref[0] = -jnp.exp(wlog)
    g_ref[0] = _dot_hi(_sigmoid(xg), g2_ref[...])


def _rwkv_prep(pa, mu, w0, a0, kkw, kaw, w2, a2, g2, tm=256):
    bsz, seq, ap = pa.shape
    w = BRANCH_WIDTH
    row = lambda t: t.reshape(1, -1)
    full = lambda shape: pl.BlockSpec(shape, lambda b, i: (0, 0))
    out_spec = pl.BlockSpec((1, tm, w), lambda b, i: (b, i, 0))
    return pl.pallas_call(
        _rwkv_prep_kernel, grid=(bsz, seq // tm),
        in_specs=[pl.BlockSpec((1, tm, ap), lambda b, i: (b, i, 0)),
                  pl.BlockSpec((1, 8, ap), lambda b, i: (b, jnp.maximum(i * (tm // 8) - 1, 0), 0)),
                  full((1, ap)), full((1, w)), full((1, w)), full((1, w)), full((1, w)),
                  full((128, w)), full((128, w)), full((256, w))],
        out_specs=[out_spec] * 7,
        out_shape=[SDS((bsz, seq, w), F32)] * 7,
        compiler_params=_cp("parallel", "parallel"), name="rwkv_prep",
    )(pa, pa, row(mu), row(w0), row(a0), row(kkw), row(kaw), w2, a2, g2)


def _rwkv_rec_kernel(r_ref, k_ref, v_ref, kk_ref, a_ref, lw_ref, rk_ref, lnw_ref, lnb_ref, o_ref, s_ref):
    @pl.when(pl.program_id(1) == 0)
    def _():
        s_ref[...] = jnp.zeros_like(s_ref)

    nh, n = s_ref.shape[0], s_ref.shape[1]
    heads = lambda ref: jnp.stack([ref[0, :, h * n:(h + 1) * n] for h in range(nh)])
    r, k, v, kk, a, lw = (heads(ref) for ref in (r_ref, k_ref, v_ref, kk_ref, a_ref, lw_ref))
    c = r.shape[1]
    kn = kk / jnp.maximum(jnp.sqrt(jnp.sum(kk * kk, axis=-1, keepdims=True)), 1e-12)
    b = kn * a
    row = lax.broadcasted_iota(jnp.int32, (c, c), 0)
    col = lax.broadcasted_iota(jnp.int32, (c, c), 1)
    strict, incl = (row > col)[None], (row >= col)[None]
    bdot = lambda spec: (lambda x, y: jnp.einsum(spec, x.astype(BF16), y.astype(BF16), preferred_element_type=F32))
    nt, nn, tn = bdot('hik,hjk->hij'), bdot('hij,hjk->hik'), bdot('hiv,hik->hvk')
    tril = jnp.broadcast_to((row >= col).astype(BF16)[None], (nh, c, c))
    lw_hi = lw.astype(BF16)
    lw_r1 = lw - lw_hi.astype(F32)
    lw_mid = lw_r1.astype(BF16)
    lw_lo = (lw_r1 - lw_mid.astype(F32)).astype(BF16)
    lc = nn(tril, lw_hi) + (nn(tril, lw_mid) + nn(tril, lw_lo))
    lc_last = lc[:, c - 1:c, :]
    e_neg, e_end = jnp.exp(-lc), jnp.exp(lc_last - lc)
    kt, rt = kn * jnp.exp(lc - lw), r * jnp.exp(lc)
    bt, kd = b * e_neg, k * e_neg
    lm = jnp.where(strict, -nt(kt, bt), 0.0)
    ak = jnp.where(strict, nt(kt, kd), 0.0)
    bb = jnp.where(incl, nt(rt, bt), 0.0)
    bk = jnp.where(incl, nt(rt, kd), 0.0)
    s0 = s_ref[...]
    u = -(nt(kt, s0) + nn(ak, v))
    pw = lm
    n_doublings = c.bit_length() - 1
    for it in range(n_doublings):
        u = u + nn(pw, u)
        if it + 1 < n_doublings:
            pw = nn(pw, pw)
    y = nt(rt, s0) + nn(bb, u) + nn(bk, v)
    s_ref[...] = s0 * jnp.exp(lc_last) + tn(u, b * e_end) + tn(v, k * e_end)
    mean = jnp.mean(y, axis=-1, keepdims=True)
    var = jnp.mean(jnp.square(y - mean), axis=-1, keepdims=True)
    out = (y - mean) * lax.rsqrt(var + RWKV_GN_EPS) * lnw_ref[...] + lnb_ref[...]
    out = out + jnp.sum(r * k * rk_ref[...], axis=-1, keepdims=True) * v
    for h in range(nh):
        o_ref[0, :, h * n:(h + 1) * n] = out[h]


def _rwkv_rec(r, k, v, kk, a, lw, rk, lnw, lnb, chunk=RWKV_CHUNK):
    bsz, seq, w = r.shape
    nh, n = rk.shape
    blk = pl.BlockSpec((1, chunk, w), lambda b, i: (b, i, 0))
    par = pl.BlockSpec((nh, 1, n), lambda b, i: (0, 0, 0))
    return pl.pallas_call(
        _rwkv_rec_kernel, grid=(bsz, seq // chunk),
        in_specs=[blk] * 6 + [par] * 3, out_specs=blk,
        out_shape=SDS((bsz, seq, w), F32),
        scratch_shapes=[pltpu.VMEM((nh, n, n), F32)],
        compiler_params=_cp("parallel", "arbitrary"), name="rwkv_rec",
    )(r, k, v, kk, a, lw, rk.reshape(nh, 1, n), lnw.reshape(nh, 1, n), lnb.reshape(nh, 1, n))


def _pad_rows(w, rows):
    return jnp.zeros((rows, w.shape[1]), w.dtype).at[:w.shape[0]].set(w)


def _rwkv_pack_cols(t):
    lead = t.shape[:-1]
    z = lambda n: jnp.zeros(lead + (n,), t.dtype)
    o1 = A_MAIN + RWKV_DECAY_RANK
    o2 = o1 + RWKV_ICLR_RANK
    return jnp.concatenate([t[..., :A_MAIN], t[..., A_MAIN:o1], z(128 - RWKV_DECAY_RANK), t[..., o1:o2],
                            z(128 - RWKV_ICLR_RANK), t[..., o2:], z(256 - RWKV_GATE_RANK)], axis=-1)


def _rwkv_mixer(pa, mu, w0, w2, a0, a2, g2, kkw, kaw, rk, lnw, lnb):
    bsz, seq, _ = pa.shape
    r, k, v, kk, a, lw, g = _rwkv_prep(pa, _rwkv_pack_cols(mu), w0, a0, kkw, kaw,
                                       _pad_rows(w2, 128), _pad_rows(a2, 128), _pad_rows(g2, 256))
    y = _rwkv_rec(r, k, v, kk, a, lw, rk, lnw.reshape(RWKV_HEADS, RWKV_HEAD), lnb.reshape(RWKV_HEADS, RWKV_HEAD))
    return y, g


def _att_kernel(q_ref, kc_ref, kp_ref, vc_ref, vp_ref, o_ref, lse_ref):
    nt = lambda x, y: lax.dot_general(x, y, (((1,), (1,)), ((), ())), preferred_element_type=F32)
    scale = ATT_HEAD ** -0.5
    qi = lax.broadcasted_iota(jnp.int32, (ATT_BLOCK, ATT_BLOCK), 0)
    ki = lax.broadcasted_iota(jnp.int32, (ATT_BLOCK, ATT_BLOCK), 1)
    cur_ok = ki <= qi
    prev_ok = (ki >= qi) & (pl.program_id(2) > 0)
    lane = lax.broadcasted_iota(jnp.int32, (ATT_BLOCK, LANES), 1)
    lse_all = jnp.zeros((ATT_BLOCK, LANES), F32)
    for h in range(ATT_HEADS):
        cols = slice(h * ATT_HEAD, (h + 1) * ATT_HEAD)
        q = q_ref[0, 0, :, cols]
        s_c = jnp.where(cur_ok, nt(q, kc_ref[0, 0, :, cols]) * scale, NEG)
        s_p = jnp.where(prev_ok, nt(q, kp_ref[0, 0, :, cols]) * scale, NEG)
        m = jnp.maximum(jnp.max(s_c, axis=-1, keepdims=True), jnp.max(s_p, axis=-1, keepdims=True))
        e_c, e_p = jnp.exp(s_c - m), jnp.exp(s_p - m)
        den = jnp.sum(e_c, axis=-1, keepdims=True) + jnp.sum(e_p, axis=-1, keepdims=True)
        o = _dot(e_c.astype(BF16), vc_ref[0, 0, :, cols]) + _dot(e_p.astype(BF16), vp_ref[0, 0, :, cols])
        o_ref[0, 0, :, cols] = o / den
        lse_all = jnp.where(lane == h, m + jnp.log(den), lse_all)
    lse_ref[0, 0] = lse_all


def _att_group(pb, gi):
    bsz, dilation, n_phase, _ = pb.shape
    w = ATT_HEADS * ATT_HEAD
    nblk = n_phase // ATT_BLOCK
    blk = (1, 1, ATT_BLOCK, w)
    cur = lambda t: pl.BlockSpec(blk, lambda b, ph, i: (b, ph, i, t))
    prev = lambda t: pl.BlockSpec(blk, lambda b, ph, i: (b, ph, jnp.maximum(i - 1, 0), t))
    return pl.pallas_call(
        _att_kernel, grid=(bsz, dilation, nblk),
        in_specs=[cur(0), cur(1), prev(1), cur(2), prev(2)],
        out_specs=[pl.BlockSpec(blk, lambda b, ph, i: (b, ph, i, 0)),
                   pl.BlockSpec((1, 1, ATT_BLOCK, LANES), lambda b, ph, i: (b, ph, i, 0))],
        out_shape=[SDS((bsz, dilation, n_phase, w), F32), SDS((bsz, dilation, n_phase, LANES), F32)],
        compiler_params=_cp("parallel", "parallel", "arbitrary"), name=f"att_g{gi}",
    )(pb, pb, pb, pb, pb)


def _att_combine_kernel(o0, o1, o2, l0, l1, l2, e_ref, out_ref):
    tm = out_ref.shape[0]

    def tokens(ref):
        d = ref.shape[1]
        if d == 1:
            return ref[0, 0]
        return jnp.stack([ref[0, ph] for ph in range(d)], axis=1).reshape(tm, ref.shape[-1])

    lses = [tokens(l) for l in (l0, l1, l2)]
    m = jnp.maximum(jnp.maximum(lses[0], lses[1]), lses[2])
    es = [jnp.exp(l - m) for l in lses]
    inv = 1.0 / (es[0] + es[1] + es[2])
    acc = None
    for e, o in zip(es, (o0, o1, o2)):
        term = _dot_hi(e * inv, e_ref[...]) * tokens(o)
        acc = term if acc is None else acc + term
    out_ref[...] = acc.astype(out_ref.dtype)


def _attention_mixer(pbs, tm=256):
    res = [_att_group(pb, gi) for gi, pb in enumerate(pbs)]
    bsz, _, seq, w = res[0][0].shape
    nt = seq // tm
    spec = lambda t: pl.BlockSpec((1, t.shape[1], tm // t.shape[1], t.shape[3]), lambda b, i: (b, 0, i, 0))
    expand = (jnp.arange(LANES, dtype=jnp.int32)[:, None] == jnp.arange(w, dtype=jnp.int32)[None, :] // ATT_HEAD).astype(F32)
    args = [t[0] for t in res] + [t[1] for t in res]
    return pl.pallas_call(
        _att_combine_kernel, grid=(bsz, nt),
        in_specs=[spec(t) for t in args] + [pl.BlockSpec((LANES, w), lambda b, i: (0, 0))],
        out_specs=pl.BlockSpec((tm, w), lambda b, i: (b * nt + i, 0)),
        out_shape=SDS((bsz * seq, w), BF16), compiler_params=_cp("parallel", "parallel"), name="att_combine",
    )(*args, expand)


def _pool_kernel(x_ref, halo_ref, w_ref, sc_ref, o_ref, ext_ref):
    tm = x_ref.shape[1]
    i = pl.program_id(1)
    x = x_ref[0]
    ext_ref[0:POOL_HALO, :] = jnp.where(i > 0, halo_ref[0], 0.0)
    ext_ref[POOL_HALO:, :] = x
    pos = i * tm + lax.broadcasted_iota(jnp.int32, (tm, POOL_GROUP), 0)
    outs = []
    for gi, win in enumerate(POOL_WINDOWS):
        cols = slice(gi * POOL_GROUP, (gi + 1) * POOL_GROUP)
        xg = x[:, cols]
        s = xg
        for j in range(1, win):
            s = s + ext_ref[pl.ds(POOL_HALO - j, tm), cols]
        mixed = s / jnp.minimum(pos + 1, win).astype(F32) - xg
        outs.append(_dot(mixed.astype(BF16), w_ref[gi]))
    o_ref[0] = (jnp.concatenate(outs, axis=-1) * sc_ref[...]).astype(o_ref.dtype)


def _pool_mixer(pc, w_pool, scale, tm=256):
    bsz, seq, w = pc.shape
    out = pl.pallas_call(
        _pool_kernel, grid=(bsz, seq // tm),
        in_specs=[pl.BlockSpec((1, tm, w), lambda b, i: (b, i, 0)),
                  pl.BlockSpec((1, POOL_HALO, w), lambda b, i: (b, jnp.maximum(i * (tm // POOL_HALO) - 1, 0), 0)),
                  pl.BlockSpec(w_pool.shape, lambda b, i: (0, 0, 0)),
                  pl.BlockSpec((1, w), lambda b, i: (0, 0))],
        out_specs=pl.BlockSpec((1, tm, w), lambda b, i: (b, i, 0)),
        out_shape=SDS((bsz, seq, w), BF16),
        scratch_shapes=[pltpu.VMEM((tm + POOL_HALO, w), F32)],
        compiler_params=_cp("parallel", "parallel"), name="pool",
    )(pc, pc, w_pool.astype(BF16), scale.reshape(1, w))
    return out.reshape(bsz * seq, w)


SSM_HALO = 8


def _ssd_kernel(p_ref, halo_ref, cw_ref, cb_ref, dtb_ref, ah_ref, dsk_ref, nw_ref, e64_ref, e128_ref,
                o_ref, ext_ref, h_ref):
    ci = pl.program_id(1)
    q = SSM_CHUNK
    inner = SSM_INNER
    xbc_lo, xbc_hi = inner, inner + SSM_CONV_DIM

    @pl.when(ci == 0)
    def _():
        h_ref[...] = jnp.zeros_like(h_ref)

    z = p_ref[0, :, 0:inner]
    ext_ref[0:SSM_HALO, :] = jnp.where(ci > 0, halo_ref[0, :, xbc_lo:xbc_hi], 0.0)
    ext_ref[SSM_HALO:, :] = p_ref[0, :, xbc_lo:xbc_hi]
    conv = cb_ref[...]
    for j in range(SSM_CONV):
        conv = conv + cw_ref[j:j + 1, :] * ext_ref[pl.ds(SSM_HALO - (SSM_CONV - 1) + j, q), :]
    xbc = _silu(conv)
    xs = xbc[:, 0:inner]
    dt = _softplus(p_ref[0, :, xbc_hi:xbc_hi + LANES] + dtb_ref[...])
    a = dt * ah_ref[...]
    row = lax.broadcasted_iota(jnp.int32, (q, q), 0)
    col = lax.broadcasted_iota(jnp.int32, (q, q), 1)
    causal = row >= col
    a_cum = _dot_hi(causal.astype(F32), a)
    a_cum_t = a_cum.T
    dt_full = _dot_hi(dt, e64_ref[...])
    acum_full = _dot_hi(a_cum, e64_ref[...])
    alast_full = acum_full[q - 1:q, :]
    acum_b = _dot_hi(a_cum, e128_ref[...])
    xdt = xs * dt_full
    x_to_end = xdt * jnp.exp(alast_full - acum_full)
    exp_ac = jnp.exp(acum_full)
    chunk_dec = jnp.exp(alast_full)
    lane = lax.broadcasted_iota(jnp.int32, (q, LANES), 1)
    first_head = lane < SSM_HEAD
    ys = []
    for g in range(SSM_GROUPS):
        bm = xbc[:, inner + g * SSM_STATE:inner + (g + 1) * SSM_STATE]
        cm = xbc[:, inner + (SSM_GROUPS + g) * SSM_STATE:inner + (SSM_GROUPS + g + 1) * SSM_STATE].astype(BF16)
        bt = bm.T.astype(BF16)
        cb = _dot(cm, bt)
        for pr in range(SSM_HEADS // SSM_GROUPS // 2):
            pi = g * (SSM_HEADS // SSM_GROUPS // 2) + pr
            cols = slice(pi * LANES, (pi + 1) * LANES)
            mats = []
            for hd in (2 * pi, 2 * pi + 1):
                seg = acum_b[:, hd * LANES:(hd + 1) * LANES] - a_cum_t[hd:hd + 1, :]
                mats.append((cb * jnp.exp(jnp.where(causal, seg, NEG))).astype(BF16))
            xp = xdt[:, cols]
            x_blockdiag = jnp.concatenate([jnp.where(first_head, xp, 0.0), jnp.where(first_head, 0.0, xp)], axis=0)
            y_diag = _dot(jnp.concatenate(mats, axis=1), x_blockdiag.astype(BF16))
            h_t = h_ref[pi]
            y_off = _dot(cm, h_t.astype(BF16)) * exp_ac[:, cols]
            h_ref[pi] = h_t * chunk_dec[:, cols] + _dot(bt, x_to_end[:, cols].astype(BF16))
            ys.append(y_diag + y_off)
    y = (jnp.concatenate(ys, axis=-1) + xs * dsk_ref[...]) * _silu(z)
    gsize = inner // SSM_GROUPS
    outs = []
    for g in range(SSM_GROUPS):
        yg = y[:, g * gsize:(g + 1) * gsize]
        outs.append(yg * lax.rsqrt(jnp.mean(yg * yg, axis=-1, keepdims=True) + NORM_EPS))
    o_ref[0] = (jnp.concatenate(outs, axis=-1) * nw_ref[...]).astype(o_ref.dtype)


def _mamba_mixer(pd, conv_w, conv_b, dt_bias, a_log, d_skip, norm_w):
    bsz, seq, dp = pd.shape
    q = SSM_CHUNK
    pad_heads = lambda t: jnp.zeros((1, LANES), F32).at[0, :SSM_HEADS].set(t)
    head_of = lambda width: jnp.arange(SSM_HEADS * width, dtype=jnp.int32)[None, :] // width
    expand = lambda width: (jnp.arange(LANES, dtype=jnp.int32)[:, None] == head_of(width)).astype(F32)
    full2 = lambda shape: pl.BlockSpec(shape, lambda b, i: (0, 0))
    out = pl.pallas_call(
        _ssd_kernel, grid=(bsz, seq // q),
        in_specs=[pl.BlockSpec((1, q, dp), lambda b, i: (b, i, 0)),
                  pl.BlockSpec((1, SSM_HALO, dp), lambda b, i: (b, jnp.maximum(i * (q // SSM_HALO) - 1, 0), 0)),
                  full2((SSM_CONV, SSM_CONV_DIM)), full2((1, SSM_CONV_DIM)), full2((1, LANES)), full2((1, LANES)),
                  full2((1, SSM_INNER)), full2((1, SSM_INNER)),
                  full2((LANES, SSM_HEADS * SSM_HEAD)), full2((LANES, SSM_HEADS * LANES))],
        out_specs=pl.BlockSpec((1, q, SSM_INNER), lambda b, i: (b, i, 0)),
        out_shape=SDS((bsz, seq, SSM_INNER), BF16),
        scratch_shapes=[pltpu.VMEM((q + SSM_HALO, SSM_CONV_DIM), F32),
                        pltpu.VMEM((SSM_HEADS // 2, SSM_STATE, LANES), F32)],
        compiler_params=_cp("parallel", "arbitrary"), name="ssd",
    )(pd, pd, conv_w, conv_b.reshape(1, -1), pad_heads(dt_bias), pad_heads(-jnp.exp(a_log)),
      jnp.repeat(d_skip, SSM_HEAD).reshape(1, -1), norm_w.reshape(1, -1), expand(SSM_HEAD), expand(LANES))
    return out.reshape(bsz * seq, SSM_INNER)


def _merge_kernel(pg_ref, ya_ref, ga_ref, bb_ref, bc_ref, bd_ref, gu_ref, gb_ref, wb_ref, o_ref):
    pg = pg_ref[...]
    branches = ((ya_ref[...] * ga_ref[...]).astype(BF16), bb_ref[...], bc_ref[...], bd_ref[...])
    acc = None
    for bi, br in enumerate(branches):
        term = _sigmoid(_dot(pg, gu_ref[bi]) + gb_ref[bi]) * _dot(br, wb_ref[bi])
        acc = term if acc is None else acc + term
    o_ref[...] = acc.astype(o_ref.dtype)


def _merge(pg, ya, ga, bb, bc, bd, gate_up, gate_b, w_branch, tm=512, tn=512):
    m = pg.shape[0]
    nb, kw, d = w_branch.shape
    rows = lambda k: pl.BlockSpec((tm, k), lambda i, j: (i, 0))
    return pl.pallas_call(
        _merge_kernel, grid=(m // tm, d // tn),
        in_specs=[rows(pg.shape[1]), rows(kw), rows(kw), rows(kw), rows(kw), rows(kw),
                  pl.BlockSpec((nb, pg.shape[1], tn), lambda i, j: (0, 0, j)),
                  pl.BlockSpec((nb, 1, tn), lambda i, j: (0, 0, j)),
                  pl.BlockSpec((nb, kw, tn), lambda i, j: (0, 0, j))],
        out_specs=pl.BlockSpec((tm, tn), lambda i, j: (i, j)),
        out_shape=SDS((m, d), BF16), compiler_params=_cp("parallel", "parallel"), name="merge",
    )(pg, ya, ga, bb, bc, bd, gate_up, gate_b.reshape(nb, 1, d), w_branch)


SUBLANES = 8


def _store_slabs(ref, rows):
    w = ref.shape[-1]
    for j in range(SUBLANES):
        ref[:, j, :] = rows[:, j * w:(j + 1) * w]


def _norm_route_kernel(x_ref, g_ref, sc_ref, sh_ref, wr_ref, br_ref, h_ref, ids_ref, wts_ref):
    x = x_ref[0]
    ms = jnp.mean(x * x, axis=-1, keepdims=True)
    h = x * lax.rsqrt(ms + NORM_EPS) * g_ref[...] * (1.0 + sc_ref[0]) + sh_ref[0]
    _store_slabs(h_ref, h)
    lg = _dot_hi(h, wr_ref[...]) + br_ref[...]
    lane = lax.broadcasted_iota(jnp.int32, lg.shape, 1)
    lane_f = lane.astype(F32)
    first = lambda hit: jnp.min(jnp.where(hit, lane_f, float(LANES)), axis=-1, keepdims=True)
    gmask = lane < MOE_GROUPS
    gl = jnp.where(gmask, lg, NEG)
    gmax = jnp.max(gl, axis=-1, keepdims=True)
    gsel = first(gl == gmax)
    gprob = 1.0 / jnp.sum(jnp.where(gmask, jnp.exp(gl - gmax), 0.0), axis=-1, keepdims=True)
    lo = MOE_GROUPS + gsel * MOE_EXPERTS_PER_GROUP
    emask = (lane_f >= lo) & (lane_f < lo + MOE_EXPERTS_PER_GROUP)
    el = jnp.where(emask, lg, NEG)
    v1 = jnp.max(el, axis=-1, keepdims=True)
    i1 = first(el == v1)
    el2 = jnp.where(lane_f == i1, NEG, el)
    v2 = jnp.max(el2, axis=-1, keepdims=True)
    i2 = first((el2 == v2) & emask & (lane_f != i1))
    t = jnp.exp(v2 - v1)
    w1 = gprob / (1.0 + t)
    w2 = gprob * t / (1.0 + t)
    ids_ref[0] = jnp.where(lane == 0, i1, jnp.where(lane == 1, i2, float(MOE_GROUPS))).astype(jnp.int32) - MOE_GROUPS
    wts_ref[0] = jnp.where(lane == 0, w1, jnp.where(lane == 1, w2, 0.0))


def _norm_route(x, g, scale, shift, rg_w, rg_b, re_w, re_b, tm=256):
    bsz, seq, d = x.shape
    n_log = MOE_GROUPS + MOE_EXPERTS
    wr = jnp.zeros((d, LANES), F32).at[:, :n_log].set(jnp.concatenate([rg_w, re_w], axis=1))
    br = jnp.zeros((1, LANES), F32).at[0, :n_log].set(jnp.concatenate([rg_b, re_b]))
    tok = lambda width: pl.BlockSpec((1, tm, width), lambda b, i: (b, i, 0))
    m, dw, nt = bsz * seq, d // SUBLANES, seq // tm
    h, ids, wts = pl.pallas_call(
        _norm_route_kernel, grid=(bsz, nt),
        in_specs=[tok(d), pl.BlockSpec((1, d), lambda b, i: (0, 0)),
                  pl.BlockSpec((1, 1, d), lambda b, i: (b, 0, 0)), pl.BlockSpec((1, 1, d), lambda b, i: (b, 0, 0)),
                  pl.BlockSpec((d, LANES), lambda b, i: (0, 0)), pl.BlockSpec((1, LANES), lambda b, i: (0, 0))],
        out_specs=[pl.BlockSpec((tm, SUBLANES, dw), lambda b, i: (b * nt + i, 0, 0)), tok(LANES), tok(LANES)],
        out_shape=[SDS((m, SUBLANES, dw), F32), SDS((bsz, seq, LANES), jnp.int32), SDS((bsz, seq, LANES), F32)],
        compiler_params=_cp("parallel", "parallel"), name="norm_route",
    )(x, g.reshape(1, d), scale.reshape(bsz, 1, d), shift.reshape(bsz, 1, d), wr, br)
    return h, ids.reshape(m, LANES)[:, :MOE_TOP_K], wts.reshape(m, LANES)[:, :MOE_TOP_K]


ROW_COPY_UNROLL = 8


def _ffn_kernel(be_ref, nb_ref, tok_ref, dst_ref, cnt_ref, h_hbm, wg_ref, wu_ref, wd_ref, sw_ref, y_hbm,
                xbuf, ybuf, xrow, gsem, ssem):
    j = pl.program_id(0)
    nb = nb_ref[0]
    cur = j % 2

    def gather_copy(buf, s, tok):
        return pltpu.make_async_copy(h_hbm.at[pl.ds(tok, 1)], xbuf.at[buf, pl.ds(s, 1)], gsem.at[buf])

    def scatter_copy(buf, s, dst):
        return pltpu.make_async_copy(ybuf.at[buf, pl.ds(s, 1)], y_hbm.at[pl.ds(dst, 1)], ssem.at[buf])

    def start_gather(blk, buf):
        def body(s, carry):
            gather_copy(buf, s, tok_ref[blk * MOE_ROWS + s]).start()
            return carry
        lax.fori_loop(0, MOE_ROWS, body, 0, unroll=ROW_COPY_UNROLL)

    def wait_gather(buf):
        pltpu.make_async_copy(h_hbm.at[pl.ds(0, MOE_ROWS)], xbuf.at[buf], gsem.at[buf]).wait()

    def start_scatter(blk, buf):
        def body(s, carry):
            scatter_copy(buf, s, dst_ref[blk * MOE_ROWS + s]).start()
            return carry
        full = cnt_ref[blk] == MOE_ROWS

        @pl.when(full)
        def _():
            lax.fori_loop(0, MOE_ROWS, body, 0, unroll=ROW_COPY_UNROLL)

        @pl.when(jnp.logical_not(full))
        def _():
            lax.fori_loop(0, cnt_ref[blk], body, 0)

    def wait_scatter(blk, buf):
        n = cnt_ref[blk]

        @pl.when(n > 0)
        def _():
            pltpu.make_async_copy(ybuf.at[buf, pl.ds(0, n)], y_hbm.at[pl.ds(0, n)], ssem.at[buf]).wait()

    @pl.when(j == 0)
    def _():
        start_gather(0, 0)

    @pl.when(j + 1 < nb)
    def _():
        start_gather(j + 1, 1 - cur)

    @pl.when(j < nb)
    def _():
        wait_gather(cur)

        @pl.when(j >= 2)
        def _():
            wait_scatter(j - 2, cur)

        dw = xbuf.shape[-1]
        for c in range(SUBLANES):
            xrow[:, c * dw:(c + 1) * dw] = xbuf[cur, :, c, :]
        x = xrow[...].astype(BF16)
        hidden = (_silu(_dot(x, wg_ref[0])) * _dot(x, wu_ref[0])).astype(BF16)
        _store_slabs(ybuf.at[cur], _dot(hidden, wd_ref[0]) * sw_ref[...])
        start_scatter(j, cur)

    @pl.when(j == nb - 1)
    def _():
        @pl.when(j >= 1)
        def _():
            wait_scatter(j - 1, 1 - cur)
        wait_scatter(j, cur)


def _moe_plan(expert, weight):
    n_tok = expert.shape[0]
    n_assign = n_tok * MOE_TOP_K
    flat_e = expert.reshape(-1)
    flat_w = weight.reshape(-1)
    order = jnp.argsort(flat_e).astype(jnp.int32)
    bounds = jnp.searchsorted(flat_e[order], jnp.arange(MOE_EXPERTS + 1, dtype=flat_e.dtype)).astype(jnp.int32)
    counts = bounds[1:] - bounds[:-1]
    padded = (counts + MOE_ROWS - 1) // MOE_ROWS * MOE_ROWS
    pad_end = jnp.cumsum(padded)
    pad_start = pad_end - padded
    start = jnp.cumsum(counts) - counts
    n_blocks = n_assign // MOE_ROWS + MOE_EXPERTS
    n_slots = n_blocks * MOE_ROWS
    blk0 = jnp.arange(n_blocks, dtype=jnp.int32) * MOE_ROWS
    block_e = jnp.minimum(jnp.searchsorted(pad_end, blk0, side='right'), MOE_EXPERTS - 1).astype(jnp.int32)
    block_cnt = jnp.clip(counts[block_e] - (blk0 - pad_start[block_e]), 0, MOE_ROWS).astype(jnp.int32)
    slot_e = jnp.repeat(block_e, MOE_ROWS)
    pos = jnp.arange(n_slots, dtype=jnp.int32) - pad_start[slot_e]
    valid = pos < counts[slot_e]
    assign = order[jnp.clip(start[slot_e] + pos, 0, n_assign - 1)]
    tok, k = assign // MOE_TOP_K, assign % MOE_TOP_K
    slot_tok = jnp.where(valid, tok, 0).astype(jnp.int32)
    slot_w = jnp.where(valid, flat_w[assign], 0.0)
    slot_dst = jnp.where(valid, k * n_tok + tok, 0).astype(jnp.int32)
    n_used_blocks = (pad_end[-1:] // MOE_ROWS).astype(jnp.int32)
    return slot_tok, slot_w, block_e, slot_dst, block_cnt, n_used_blocks


def _cast_kernel(x_ref, o_ref):
    o_ref[...] = x_ref[0].astype(o_ref.dtype)


def _cast_bf16(w, layer, block_bytes=2 * 1024 * 1024):
    _, e, r, c = w.shape
    tr = min(r, block_bytes // (4 * c))
    return pl.pallas_call(
        _cast_kernel, grid=(e, r // tr),
        in_specs=[pl.BlockSpec((1, 1, tr, c), lambda i, j: (layer, i, j, 0))],
        out_specs=pl.BlockSpec((1, tr, c), lambda i, j: (i, j, 0)),
        out_shape=SDS((e, r, c), BF16), compiler_params=_cp("parallel", "parallel"), name="cast_bf16")(w)


def _moe_apply(h, expert, weight, w_gate, w_up, w_down):
    n_tok, _, dw = h.shape
    d = dw * SUBLANES
    slot_tok, slot_w, block_e, slot_dst, block_cnt, n_used_blocks = _moe_plan(expert, weight)
    n_slots = slot_tok.shape[0]
    used = lambda j, nb: jnp.minimum(j, nb[0] - 1)
    expert_blk = lambda shape: pl.BlockSpec(shape, lambda j, be, nb, *_: (be[used(j, nb)], 0, 0))
    any_space = pl.BlockSpec(memory_space=pl.ANY)
    return pl.pallas_call(
        _ffn_kernel,
        grid_spec=pltpu.PrefetchScalarGridSpec(
            num_scalar_prefetch=5, grid=(n_slots // MOE_ROWS,),
            in_specs=[any_space, expert_blk((1, d, MOE_FF)), expert_blk((1, d, MOE_FF)), expert_blk((1, MOE_FF, d)),
                      pl.BlockSpec((MOE_ROWS, 1), lambda j, be, nb, *_: (used(j, nb), 0))],
            out_specs=any_space,
            scratch_shapes=[pltpu.VMEM((2, MOE_ROWS, SUBLANES, dw), F32), pltpu.VMEM((2, MOE_ROWS, SUBLANES, dw), F32),
                            pltpu.VMEM((MOE_ROWS, d), F32),
                            pltpu.SemaphoreType.DMA((2,)), pltpu.SemaphoreType.DMA((2,))]),
        out_shape=SDS((MOE_TOP_K * n_tok, SUBLANES, dw), F32), compiler_params=_cp("arbitrary"), name="moe_ffn",
    )(block_e, n_used_blocks, slot_tok, slot_dst, block_cnt, h, w_gate, w_up, w_down, slot_w.reshape(n_slots, 1))


def _moe_combine_kernel(x_ref, y0_ref, y1_ref, g_ref, o_ref):
    dw = y0_ref.shape[-1]
    for j in range(SUBLANES):
        cols = slice(j * dw, (j + 1) * dw)
        o_ref[0, :, cols] = x_ref[0, :, cols] + g_ref[0, :, cols] * (y0_ref[0, 0, :, j, :] + y1_ref[0, 0, :, j, :])


def _moe_combine(x, y2, gate, tm=256):
    bsz, seq, d = x.shape
    dw = d // SUBLANES
    y5 = y2.reshape(MOE_TOP_K, bsz, seq, SUBLANES, dw)
    tok = pl.BlockSpec((1, tm, d), lambda b, i: (b, i, 0))
    slab = lambda k: pl.BlockSpec((1, 1, tm, SUBLANES, dw), lambda b, i: (k, b, i, 0, 0))
    return pl.pallas_call(
        _moe_combine_kernel, grid=(bsz, seq // tm),
        in_specs=[tok, slab(0), slab(1), pl.BlockSpec((1, 1, d), lambda b, i: (b, 0, 0))],
        out_specs=tok, out_shape=SDS((bsz, seq, d), F32),
        compiler_params=_cp("parallel", "parallel"), name="moe_combine",
    )(x, y5, y5, gate.reshape(bsz, 1, d))


PACK_ROWS = 128


def _pack_w_in_kernel(offs, w_ref, a_ref, b0_ref, b1_ref, b2_ref, c_ref, d_ref, g_ref):
    rows = w_ref.shape[0]
    ng, wh = len(ATT_GROUPS), ATT_HEADS * ATT_HEAD
    zeros = lambda n: jnp.zeros((rows, n), BF16)
    cast = lambda lo, hi: w_ref[:, lo:hi].astype(BF16)
    o_w = offs[0] + A_MAIN
    o_a = o_w + RWKV_DECAY_RANK
    o_g = o_a + RWKV_ICLR_RANK
    a_ref[...] = jnp.concatenate(
        [cast(offs[0], o_w), cast(o_w, o_a), zeros(128 - RWKV_DECAY_RANK), cast(o_a, o_g), zeros(128 - RWKV_ICLR_RANK),
         cast(o_g, offs[1]), zeros(256 - RWKV_GATE_RANK)], axis=1)
    for gi, b_ref in enumerate((b0_ref, b1_ref, b2_ref)):
        b_ref[...] = jnp.concatenate(
            [cast(offs[1] + (t * ng + gi) * wh, offs[1] + (t * ng + gi + 1) * wh) for t in range(3)], axis=1)
    c_ref[...] = cast(offs[2], offs[3])
    d_ref[...] = jnp.concatenate([cast(offs[3], offs[4]), zeros(D_PAD - (offs[4] - offs[3]))], axis=1)
    g_ref[...] = cast(offs[4], w_ref.shape[1])


def _pack_w_in(w, layer, offs):
    _, d, n = w.shape
    wb = (offs[2] - offs[1]) // len(ATT_GROUPS)
    widths = (A_PAD, wb, wb, wb, offs[3] - offs[2], D_PAD, n - offs[4])
    return pl.pallas_call(
        functools.partial(_pack_w_in_kernel, offs), grid=(d // PACK_ROWS,),
        in_specs=[pl.BlockSpec((None, PACK_ROWS, n), lambda i: (layer, i, 0))],
        out_specs=[pl.BlockSpec((PACK_ROWS, wd), lambda i: (i, 0)) for wd in widths],
        out_shape=[SDS((d, wd), BF16) for wd in widths], compiler_params=_cp("parallel"), name="pack_w_in",
    )(w)


def kernel(x, c, ada_w, ada_b, ada_table, norm1_g, norm2_g, w_in, rwkv_mu, rwkv_w0, rwkv_w2, rwkv_a0, rwkv_a2,
           rwkv_g2, rwkv_kk, rwkv_ka, rwkv_rk, rwkv_lnx_w, rwkv_lnx_b, pool_w, pool_scale, ssm_conv_w, ssm_conv_b,
           ssm_dt_bias, ssm_a_log, ssm_d, ssm_norm_w, gate_up, gate_b, w_branch, w_out, router_group_w,
           router_group_b, router_expert_w, router_expert_b, exp_w_gate, exp_w_up, exp_w_down, final_g):
    bsz, seq, d = x.shape
    m = bsz * seq
    a_cols = rwkv_mu.shape[1]
    b_cols = 3 * len(ATT_GROUPS) * ATT_HEADS * ATT_HEAD
    d_cols = SSM_INNER + SSM_CONV_DIM + SSM_HEADS
    offs = (0, a_cols, a_cols + b_cols, a_cols + b_cols + BRANCH_WIDTH, a_cols + b_cols + BRANCH_WIDTH + d_cols)
    cond = _cond(c, ada_w, ada_b)
    for l in range(DEPTH):
        shift1, scale1, gate1, shift2, scale2, gate2 = jnp.split(cond + ada_table[l], 6, axis=-1)
        dilations = tuple(dil for _, dil in ATT_GROUPS)
        h, *h_phases = _norm_mod_phases(x, norm1_g[l], scale1, shift1, [dil for dil in dilations if dil > 1])
        h = h.reshape(m, d)
        h_of = {1: h, **{dil: hp.reshape(m, d) for dil, hp in zip([dil for dil in dilations if dil > 1], h_phases)}}
        wa, wb0, wb1, wb2, wc, wd, wg = _pack_w_in(w_in, l, offs)
        pa = _matmul(h, wa, F32).reshape(bsz, seq, A_PAD)
        pbs = [_matmul(h_of[dil], wb, BF16).reshape(bsz, dil, seq // dil, -1)
               for dil, wb in zip(dilations, (wb0, wb1, wb2))]
        pc = _matmul(h, wc, F32).reshape(bsz, seq, BRANCH_WIDTH)
        pd = _matmul(h, wd, F32, tn=640).reshape(bsz, seq, D_PAD)
        pg = _matmul(h, wg, BF16)
        ya, ga = _rwkv_mixer(pa, rwkv_mu[l], rwkv_w0[l], rwkv_w2[l], rwkv_a0[l], rwkv_a2[l], rwkv_g2[l],
                             rwkv_kk[l], rwkv_ka[l], rwkv_rk[l], rwkv_lnx_w[l], rwkv_lnx_b[l])
        ob = _attention_mixer(pbs)
        oc = _pool_mixer(pc, pool_w[l], pool_scale[l])
        od = _mamba_mixer(pd, ssm_conv_w[l], ssm_conv_b[l], ssm_dt_bias[l], ssm_a_log[l], ssm_d[l], ssm_norm_w[l])
        merged = _merge(pg, ya.reshape(m, BRANCH_WIDTH), ga.reshape(m, BRANCH_WIDTH), ob, oc, od,
                        _cast_bf16(gate_up, l), gate_b[l], _cast_bf16(w_branch, l))
        w_out_l = _cast_bf16(w_out.reshape(DEPTH, 1, d, d), l).reshape(d, d)
        x = _matmul_resid(merged, w_out_l, x.reshape(m, d), gate1, seq).reshape(bsz, seq, d)
        h2, expert, weight = _norm_route(x, norm2_g[l], scale2, shift2, router_group_w[l], router_group_b[l],
                                         router_expert_w[l], router_expert_b[l])
        y2 = _moe_apply(h2, expert, weight, _cast_bf16(exp_w_gate, l), _cast_bf16(exp_w_up, l), _cast_bf16(exp_w_down, l))
        x = _moe_combine(x, y2, gate2)
    zeros = jnp.zeros((bsz, d), F32)
    return _norm_mod(x, final_g, zeros, zeros, F32)
```

```python
import functools

import jax
import jax.numpy as jnp
from jax import lax
from jax.experimental import pallas as pl
from jax.experimental.pallas import tpu as pltpu

F32 = jnp.float32
BF16 = jnp.bfloat16
HI = lax.Precision.HIGHEST
SDS = jax.ShapeDtypeStruct

DEPTH = 2
NORM_EPS = 1e-6
BRANCH_WIDTH = 1024
RWKV_HEAD = 64
RWKV_HEADS = 16
RWKV_DECAY_RANK = 64
RWKV_ICLR_RANK = 64
RWKV_GATE_RANK = 160
RWKV_GN_EPS = 64e-5
RWKV_CHUNK = 64
ATT_HEAD = 128
ATT_GROUPS = ((128, 1), (512, 4), (2048, 16))
ATT_HEADS = 8
ATT_BLOCK = 128
POOL_WINDOWS = (2, 4, 8, 16)
POOL_GROUP = 256
POOL_HALO = 16
SSM_INNER = 1024
SSM_HEAD = 64
SSM_HEADS = 16
SSM_GROUPS = 4
SSM_STATE = 128
SSM_CONV = 4
SSM_CHUNK = 128
SSM_CONV_DIM = SSM_INNER + 2 * SSM_GROUPS * SSM_STATE
GATE_RANK = 256
MOE_GROUPS = 4
MOE_EXPERTS_PER_GROUP = 8
MOE_EXPERTS = 32
MOE_TOP_K = 2
MOE_FF = 512
MOE_ROWS = 256
LANES = 128
NEG = -1e30
A_MAIN = 3 * BRANCH_WIDTH
A_PAD = A_MAIN + 128 + 128 + 256
D_PAD = SSM_INNER + SSM_CONV_DIM + LANES
VMEM_LIMIT = 56 * 1024 * 1024


def _cp(*sem):
    return pltpu.CompilerParams(dimension_semantics=sem, vmem_limit_bytes=VMEM_LIMIT)


def _sigmoid(x):
    return 1.0 / (1.0 + jnp.exp(-x))


def _silu(x):
    return x * _sigmoid(x)


def _softplus(x):
    return jnp.maximum(x, 0.0) + jnp.log(1.0 + jnp.exp(-jnp.abs(x)))


def _dot(a, b):
    return jnp.dot(a, b, preferred_element_type=F32)


def _dot_hi(a, b):
    return jnp.dot(a, b, precision=HI, preferred_element_type=F32)


def _cond_kernel(c_ref, w_ref, b_ref, o_ref, a_ref):
    @pl.when(pl.program_id(0) == 0)
    def _():
        a_ref[...] = _silu(c_ref[...])

    k, tn = w_ref.shape
    w = w_ref[...]
    rows = []
    for r in range(a_ref.shape[0]):
        prod = w * jnp.concatenate([a_ref[r]] * (tn // LANES), axis=1)
        part = jnp.sum(prod.reshape(k // 8, 8, tn), axis=0)
        rows.append(jnp.sum(part, axis=0, keepdims=True))
    rows.append(jnp.zeros((o_ref.shape[0] - len(rows), tn), F32))
    o_ref[...] = jnp.concatenate(rows, axis=0) + b_ref[...]


def _cond(c, ada_w, ada_b):
    bsz, d = c.shape
    n = ada_w.shape[1]
    tn = 512
    c_lanes = jnp.broadcast_to(c[:, :, None], (bsz, d, LANES))
    out = pl.pallas_call(
        _cond_kernel, grid=(n // tn,),
        in_specs=[pl.BlockSpec((bsz, d, LANES), lambda j: (0, 0, 0)),
                  pl.BlockSpec((d, tn), lambda j: (0, j)),
                  pl.BlockSpec((1, tn), lambda j: (0, j))],
        out_specs=pl.BlockSpec((8, tn), lambda j: (0, j)),
        out_shape=SDS((8, n), F32), scratch_shapes=[pltpu.VMEM((bsz, d, LANES), F32)],
        compiler_params=_cp("arbitrary"), name="cond",
    )(c_lanes, ada_w, ada_b.reshape(1, n))
    return out[:bsz]


def _norm_mod_kernel(x_ref, g_ref, sc_ref, sh_ref, o_ref):
    x = x_ref[0]
    ms = jnp.mean(x * x, axis=-1, keepdims=True)
    y = x * lax.rsqrt(ms + NORM_EPS) * g_ref[...]
    o_ref[0] = (y * (1.0 + sc_ref[0]) + sh_ref[0]).astype(o_ref.dtype)


def _norm_mod(x, g, scale, shift, out_dtype, tm=256):
    bsz, seq, d = x.shape
    return pl.pallas_call(
        _norm_mod_kernel, grid=(bsz, seq // tm),
        in_specs=[pl.BlockSpec((1, tm, d), lambda b, i: (b, i, 0)),
                  pl.BlockSpec((1, d), lambda b, i: (0, 0)),
                  pl.BlockSpec((1, 1, d), lambda b, i: (b, 0, 0)),
                  pl.BlockSpec((1, 1, d), lambda b, i: (b, 0, 0))],
        out_specs=pl.BlockSpec((1, tm, d), lambda b, i: (b, i, 0)),
        out_shape=SDS((bsz, seq, d), out_dtype), compiler_params=_cp("parallel", "parallel"), name="norm_mod",
    )(x, g.reshape(1, d), scale.reshape(bsz, 1, d), shift.reshape(bsz, 1, d))


def _norm_mod_phases_kernel(x_ref, g_ref, sc_ref, sh_ref, o_ref, *refs):
    phase_refs, scr = refs[:-1], refs[-1]
    x = x_ref[0]
    tm, d = x.shape
    ms = jnp.mean(x * x, axis=-1, keepdims=True)
    y = x * lax.rsqrt(ms + NORM_EPS) * g_ref[...] * (1.0 + sc_ref[0]) + sh_ref[0]
    o_ref[0] = y.astype(o_ref.dtype)
    for ref in phase_refs:
        dil = ref.shape[1]
        n = tm // dil
        y3 = y.reshape(n, dil, d)
        for ph in range(dil):
            scr[ph * n:(ph + 1) * n, :] = y3[:, ph, :]
        for ph in range(dil):
            ref[0, ph] = scr[ph * n:(ph + 1) * n, :].astype(ref.dtype)


def _norm_mod_phases(x, g, scale, shift, dilations, tm=256):
    bsz, seq, d = x.shape
    tok = pl.BlockSpec((1, tm, d), lambda b, i: (b, i, 0))
    vec = pl.BlockSpec((1, 1, d), lambda b, i: (b, 0, 0))
    return pl.pallas_call(
        _norm_mod_phases_kernel, grid=(bsz, seq // tm),
        in_specs=[tok, pl.BlockSpec((1, d), lambda b, i: (0, 0)), vec, vec],
        out_specs=[tok] + [pl.BlockSpec((1, dil, tm // dil, d), lambda b, i: (b, 0, i, 0)) for dil in dilations],
        out_shape=[SDS((bsz, seq, d), BF16)] + [SDS((bsz, dil, seq // dil, d), BF16) for dil in dilations],
        scratch_shapes=[pltpu.VMEM((tm, d), F32)],
        compiler_params=_cp("parallel", "parallel"), name="norm_mod_phases",
    )(x, g.reshape(1, d), scale.reshape(bsz, 1, d), shift.reshape(bsz, 1, d))


def _mm_kernel(a_ref, b_ref, o_ref):
    o_ref[...] = _dot(a_ref[...], b_ref[...]).astype(o_ref.dtype)


def _matmul(a, b, out_dtype, tm=1024, tn=512):
    m, k = a.shape
    n = b.shape[1]
    tn = min(tn, n)
    return pl.pallas_call(
        _mm_kernel, grid=(m // tm, n // tn),
        in_specs=[pl.BlockSpec((tm, k), lambda i, j: (i, 0)),
                  pl.BlockSpec((k, tn), lambda i, j: (0, j))],
        out_specs=pl.BlockSpec((tm, tn), lambda i, j: (i, j)),
        out_shape=SDS((m, n), out_dtype), compiler_params=_cp("parallel", "parallel"), name="matmul",
    )(a, b)


def _mm_resid_kernel(a_ref, b_ref, r_ref, g_ref, o_ref):
    o_ref[...] = r_ref[...] + g_ref[0] * _dot(a_ref[...], b_ref[...])


def _matmul_resid(a, b, resid, gate, seq, tm=1024, tn=512):
    m, k = a.shape
    n = b.shape[1]
    bsz = gate.shape[0]
    return pl.pallas_call(
        _mm_resid_kernel, grid=(m // tm, n // tn),
        in_specs=[pl.BlockSpec((tm, k), lambda i, j: (i, 0)),
                  pl.BlockSpec((k, tn), lambda i, j: (0, j)),
                  pl.BlockSpec((tm, tn), lambda i, j: (i, j)),
                  pl.BlockSpec((1, 1, tn), lambda i, j: (i * tm // seq, 0, j))],
        out_specs=pl.BlockSpec((tm, tn), lambda i, j: (i, j)),
        out_shape=SDS((m, n), F32), compiler_params=_cp("parallel", "parallel"), name="matmul_resid",
    )(a, b, resid, gate.reshape(bsz, 1, n))


def _rwkv_prep_kernel(p_ref, halo_ref, mu_ref, w0_ref, a0_ref, kkw_ref, kaw_ref, w2_ref, a2_ref, g2_ref,
                      r_ref, k_ref, v_ref, kk_ref, a_ref, lw_ref, g_ref):
    x = p_ref[0]
    w = BRANCH_WIDTH
    prev_first = jnp.where(pl.program_id(1) > 0, halo_ref[0][7:8, :], 0.0)
    row = lax.broadcasted_iota(jnp.int32, x.shape, 0)
    prev = jnp.where(row == 0, prev_first, pltpu.roll(x, 1, axis=0))
    p = x + (prev - x) * mu_ref[...]
    r, k, v = p[:, 0:w], p[:, w:2 * w], p[:, 2 * w:3 * w]
    xw, xa, xg = p[:, A_MAIN:A_MAIN + 128], p[:, A_MAIN + 128:A_MAIN + 256], p[:, A_MAIN + 256:A_PAD]
    wlog = -_softplus(-(w0_ref[...] + _dot_hi(jnp.tanh(xw), w2_ref[...]))) - 0.5
    a = _sigmoid(a0_ref[...] + _dot_hi(xa, a2_ref[...]))
    r_ref[0] = r
    k_ref[0] = k * (1.0 + (a - 1.0) * kaw_ref[...])
    v_ref[0] = v
    kk_ref[0] = k * kkw_ref[...]
    a_ref[0] = a
    lw_ref[0] = -jnp.exp(wlog)
    g_ref[0] = _dot_hi(_sigmoid(xg), g2_ref[...])


def _rwkv_prep(pa, mu, w0, a0, kkw, kaw, w2, a2, g2, tm=256):
    bsz, seq, ap = pa.shape
    w = BRANCH_WIDTH
    row = lambda t: t.reshape(1, -1)
    full = lambda shape: pl.BlockSpec(shape, lambda b, i: (0, 0))
    out_spec = pl.BlockSpec((1, tm, w), lambda b, i: (b, i, 0))
    return pl.pallas_call(
        _rwkv_prep_kernel, grid=(bsz, seq // tm),
        in_specs=[pl.BlockSpec((1, tm, ap), lambda b, i: (b, i, 0)),
                  pl.BlockSpec((1, 8, ap), lambda b, i: (b, jnp.maximum(i * (tm // 8) - 1, 0), 0)),
                  full((1, ap)), full((1, w)), full((1, w)), full((1, w)), full((1, w)),
                  full((128, w)), full((128, w)), full((256, w))],
        out_specs=[out_spec] * 7,
        out_shape=[SDS((bsz, seq, w), F32)] * 7,
        compiler_params=_cp("parallel", "parallel"), name="rwkv_prep",
    )(pa, pa, row(mu), row(w0), row(a0), row(kkw), row(kaw), w2, a2, g2)


def _rwkv_rec_kernel(r_ref, k_ref, v_ref, kk_ref, a_ref, lw_ref, rk_ref, lnw_ref, lnb_ref, o_ref, s_ref):
    @pl.when(pl.program_id(1) == 0)
    def _():
        s_ref[...] = jnp.zeros_like(s_ref)

    nh, n = s_ref.shape[0], s_ref.shape[1]
    heads = lambda ref: jnp.stack([ref[0, :, h * n:(h + 1) * n] for h in range(nh)])
    r, k, v, kk, a, lw = (heads(ref) for ref in (r_ref, k_ref, v_ref, kk_ref, a_ref, lw_ref))
    c = r.shape[1]
    kn = kk / jnp.maximum(jnp.sqrt(jnp.sum(kk * kk, axis=-1, keepdims=True)), 1e-12)
    b = kn * a
    row = lax.broadcasted_iota(jnp.int32, (c, c), 0)
    col = lax.broadcasted_iota(jnp.int32, (c, c), 1)
    strict, incl = (row > col)[None], (row >= col)[None]
    bdot = lambda spec: (lambda x, y: jnp.einsum(spec, x.astype(BF16), y.astype(BF16), preferred_element_type=F32))
    nt, nn, tn = bdot('hik,hjk->hij'), bdot('hij,hjk->hik'), bdot('hiv,hik->hvk')
    tril = jnp.broadcast_to((row >= col).astype(BF16)[None], (nh, c, c))
    lw_hi = lw.astype(BF16)
    lw_r1 = lw - lw_hi.astype(F32)
    lw_mid = lw_r1.astype(BF16)
    lw_lo = (lw_r1 - lw_mid.astype(F32)).astype(BF16)
    lc = nn(tril, lw_hi) + (nn(tril, lw_mid) + nn(tril, lw_lo))
    lc_last = lc[:, c - 1:c, :]
    e_neg, e_end = jnp.exp(-lc), jnp.exp(lc_last - lc)
    kt, rt = kn * jnp.exp(lc - lw), r * jnp.exp(lc)
    bt, kd = b * e_neg, k * e_neg
    kr = jnp.concatenate([kt, rt], axis=1)
    sc = nt(kr, jnp.concatenate([bt, kd], axis=1))
    lm = jnp.where(strict, -sc[:, :c, :c], 0.0)
    ak = jnp.where(strict, sc[:, :c, c:], 0.0)
    bb = jnp.where(incl, sc[:, c:, :c], 0.0)
    bk = jnp.where(incl, sc[:, c:, c:], 0.0)
    s0 = s_ref[...]
    kr_s0 = nt(kr, s0)
    abk_v = nn(jnp.concatenate([ak, bk], axis=1), v)
    u = -(kr_s0[:, :c] + abk_v[:, :c])
    pw = lm
    n_doublings = c.bit_length() - 1
    for it in range(n_doublings):
        u = u + nn(pw, u)
        if it + 1 < n_doublings:
            pw = nn(pw, pw)
    y = kr_s0[:, c:] + nn(bb, u) + abk_v[:, c:]
    s_ref[...] = s0 * jnp.exp(lc_last) + tn(jnp.concatenate([u, v], axis=1),
                                            jnp.concatenate([b * e_end, k * e_end], axis=1))
    mean = jnp.mean(y, axis=-1, keepdims=True)
    var = jnp.mean(jnp.square(y - mean), axis=-1, keepdims=True)
    out = (y - mean) * lax.rsqrt(var + RWKV_GN_EPS) * lnw_ref[...] + lnb_ref[...]
    out = out + jnp.sum(r * k * rk_ref[...], axis=-1, keepdims=True) * v
    for h in range(nh):
        o_ref[0, :, h * n:(h + 1) * n] = out[h]


def _rwkv_rec(r, k, v, kk, a, lw, rk, lnw, lnb, chunk=RWKV_CHUNK):
    bsz, seq, w = r.shape
    nh, n = rk.shape
    blk = pl.BlockSpec((1, chunk, w), lambda b, i: (b, i, 0))
    par = pl.BlockSpec((nh, 1, n), lambda b, i: (0, 0, 0))
    return pl.pallas_call(
        _rwkv_rec_kernel, grid=(bsz, seq // chunk),
        in_specs=[blk] * 6 + [par] * 3, out_specs=blk,
        out_shape=SDS((bsz, seq, w), F32),
        scratch_shapes=[pltpu.VMEM((nh, n, n), F32)],
        compiler_params=_cp("parallel", "arbitrary"), name="rwkv_rec",
    )(r, k, v, kk, a, lw, rk.reshape(nh, 1, n), lnw.reshape(nh, 1, n), lnb.reshape(nh, 1, n))


def _pad_rows(w, rows):
    return jnp.zeros((rows, w.shape[1]), w.dtype).at[:w.shape[0]].set(w)


def _rwkv_pack_cols(t):
    lead = t.shape[:-1]
    z = lambda n: jnp.zeros(lead + (n,), t.dtype)
    o1 = A_MAIN + RWKV_DECAY_RANK
    o2 = o1 + RWKV_ICLR_RANK
    return jnp.concatenate([t[..., :A_MAIN], t[..., A_MAIN:o1], z(128 - RWKV_DECAY_RANK), t[..., o1:o2],
                            z(128 - RWKV_ICLR_RANK), t[..., o2:], z(256 - RWKV_GATE_RANK)], axis=-1)


def _rwkv_mixer(pa, mu, w0, w2, a0, a2, g2, kkw, kaw, rk, lnw, lnb):
    bsz, seq, _ = pa.shape
    r, k, v, kk, a, lw, g = _rwkv_prep(pa, _rwkv_pack_cols(mu), w0, a0, kkw, kaw,
                                       _pad_rows(w2, 128), _pad_rows(a2, 128), _pad_rows(g2, 256))
    y = _rwkv_rec(r, k, v, kk, a, lw, rk, lnw.reshape(RWKV_HEADS, RWKV_HEAD), lnb.reshape(RWKV_HEADS, RWKV_HEAD))
    return y, g


def _att_kernel(q_ref, kc_ref, kp_ref, vc_ref, vp_ref, o_ref, lse_ref):
    nt = lambda x, y: lax.dot_general(x, y, (((1,), (1,)), ((), ())), preferred_element_type=F32)
    scale = ATT_HEAD ** -0.5
    qi = lax.broadcasted_iota(jnp.int32, (ATT_BLOCK, ATT_BLOCK), 0)
    ki = lax.broadcasted_iota(jnp.int32, (ATT_BLOCK, ATT_BLOCK), 1)
    cur_ok = ki <= qi
    prev_ok = (ki >= qi) & (pl.program_id(2) > 0)
    lane = lax.broadcasted_iota(jnp.int32, (ATT_BLOCK, LANES), 1)
    lse_all = jnp.zeros((ATT_BLOCK, LANES), F32)
    for h in range(ATT_HEADS):
        cols = slice(h * ATT_HEAD, (h + 1) * ATT_HEAD)
        q = q_ref[0, 0, :, cols]
        s_c = jnp.where(cur_ok, nt(q, kc_ref[0, 0, :, cols]) * scale, NEG)
        s_p = jnp.where(prev_ok, nt(q, kp_ref[0, 0, :, cols]) * scale, NEG)
        m = jnp.maximum(jnp.max(s_c, axis=-1, keepdims=True), jnp.max(s_p, axis=-1, keepdims=True))
        e_c, e_p = jnp.exp(s_c - m), jnp.exp(s_p - m)
        den = jnp.sum(e_c, axis=-1, keepdims=True) + jnp.sum(e_p, axis=-1, keepdims=True)
        o = _dot(e_c.astype(BF16), vc_ref[0, 0, :, cols]) + _dot(e_p.astype(BF16), vp_ref[0, 0, :, cols])
        o_ref[0, 0, :, cols] = o / den
        lse_all = jnp.where(lane == h, m + jnp.log(den), lse_all)
    lse_ref[0, 0] = lse_all


def _att_group(pb, gi):
    bsz, dilation, n_phase, _ = pb.shape
    w = ATT_HEADS * ATT_HEAD
    nblk = n_phase // ATT_BLOCK
    blk = (1, 1, ATT_BLOCK, w)
    cur = lambda t: pl.BlockSpec(blk, lambda b, ph, i: (b, ph, i, t))
    prev = lambda t: pl.BlockSpec(blk, lambda b, ph, i: (b, ph, jnp.maximum(i - 1, 0), t))
    return pl.pallas_call(
        _att_kernel, grid=(bsz, dilation, nblk),
        in_specs=[cur(0), cur(1), prev(1), cur(2), prev(2)],
        out_specs=[pl.BlockSpec(blk, lambda b, ph, i: (b, ph, i, 0)),
                   pl.BlockSpec((1, 1, ATT_BLOCK, LANES), lambda b, ph, i: (b, ph, i, 0))],
        out_shape=[SDS((bsz, dilation, n_phase, w), F32), SDS((bsz, dilation, n_phase, LANES), F32)],
        compiler_params=_cp("parallel", "parallel", "arbitrary"), name=f"att_g{gi}",
    )(pb, pb, pb, pb, pb)


def _att_combine_kernel(o0, o1, o2, l0, l1, l2, e_ref, out_ref):
    tm = out_ref.shape[0]

    def tokens(ref):
        d = ref.shape[1]
        if d == 1:
            return ref[0, 0]
        return jnp.stack([ref[0, ph] for ph in range(d)], axis=1).reshape(tm, ref.shape[-1])

    lses = [tokens(l) for l in (l0, l1, l2)]
    m = jnp.maximum(jnp.maximum(lses[0], lses[1]), lses[2])
    es = [jnp.exp(l - m) for l in lses]
    inv = 1.0 / (es[0] + es[1] + es[2])
    acc = None
    for e, o in zip(es, (o0, o1, o2)):
        term = _dot_hi(e * inv, e_ref[...]) * tokens(o)
        acc = term if acc is None else acc + term
    out_ref[...] = acc.astype(out_ref.dtype)


def _attention_mixer(pbs, tm=256):
    res = [_att_group(pb, gi) for gi, pb in enumerate(pbs)]
    bsz, _, seq, w = res[0][0].shape
    nt = seq // tm
    spec = lambda t: pl.BlockSpec((1, t.shape[1], tm // t.shape[1], t.shape[3]), lambda b, i: (b, 0, i, 0))
    expand = (jnp.arange(LANES, dtype=jnp.int32)[:, None] == jnp.arange(w, dtype=jnp.int32)[None, :] // ATT_HEAD).astype(F32)
    args = [t[0] for t in res] + [t[1] for t in res]
    return pl.pallas_call(
        _att_combine_kernel, grid=(bsz, nt),
        in_specs=[spec(t) for t in args] + [pl.BlockSpec((LANES, w), lambda b, i: (0, 0))],
        out_specs=pl.BlockSpec((tm, w), lambda b, i: (b * nt + i, 0)),
        out_shape=SDS((bsz * seq, w), BF16), compiler_params=_cp("parallel", "parallel"), name="att_combine",
    )(*args, expand)


def _pool_kernel(x_ref, halo_ref, w_ref, sc_ref, o_ref, ext_ref):
    tm = x_ref.shape[1]
    i = pl.program_id(1)
    x = x_ref[0]
    ext_ref[0:POOL_HALO, :] = jnp.where(i > 0, halo_ref[0], 0.0)
    ext_ref[POOL_HALO:, :] = x
    pos = i * tm + lax.broadcasted_iota(jnp.int32, (tm, POOL_GROUP), 0)
    outs = []
    for gi, win in enumerate(POOL_WINDOWS):
        cols = slice(gi * POOL_GROUP, (gi + 1) * POOL_GROUP)
        xg = x[:, cols]
        s = xg
        for j in range(1, win):
            s = s + ext_ref[pl.ds(POOL_HALO - j, tm), cols]
        mixed = s / jnp.minimum(pos + 1, win).astype(F32) - xg
        outs.append(_dot(mixed.astype(BF16), w_ref[gi]))
    o_ref[0] = (jnp.concatenate(outs, axis=-1) * sc_ref[...]).astype(o_ref.dtype)


def _pool_mixer(pc, w_pool, scale, tm=256):
    bsz, seq, w = pc.shape
    out = pl.pallas_call(
        _pool_kernel, grid=(bsz, seq // tm),
        in_specs=[pl.BlockSpec((1, tm, w), lambda b, i: (b, i, 0)),
                  pl.BlockSpec((1, POOL_HALO, w), lambda b, i: (b, jnp.maximum(i * (tm // POOL_HALO) - 1, 0), 0)),
                  pl.BlockSpec(w_pool.shape, lambda b, i: (0, 0, 0)),
                  pl.BlockSpec((1, w), lambda b, i: (0, 0))],
        out_specs=pl.BlockSpec((1, tm, w), lambda b, i: (b, i, 0)),
        out_shape=SDS((bsz, seq, w), BF16),
        scratch_shapes=[pltpu.VMEM((tm + POOL_HALO, w), F32)],
        compiler_params=_cp("parallel", "parallel"), name="pool",
    )(pc, pc, w_pool.astype(BF16), scale.reshape(1, w))
    return out.reshape(bsz * seq, w)


SSM_HALO = 8


def _ssd_kernel(p_ref, halo_ref, cw_ref, cb_ref, dtb_ref, ah_ref, dsk_ref, nw_ref, e64_ref, e128_ref,
                o_ref, ext_ref, h_ref):
    ci = pl.program_id(1)
    q = SSM_CHUNK
    inner = SSM_INNER
    xbc_lo, xbc_hi = inner, inner + SSM_CONV_DIM

    @pl.when(ci == 0)
    def _():
        h_ref[...] = jnp.zeros_like(h_ref)

    z = p_ref[0, :, 0:inner]
    ext_ref[0:SSM_HALO, :] = jnp.where(ci > 0, halo_ref[0, :, xbc_lo:xbc_hi], 0.0)
    ext_ref[SSM_HALO:, :] = p_ref[0, :, xbc_lo:xbc_hi]
    conv = cb_ref[...]
    for j in range(SSM_CONV):
        conv = conv + cw_ref[j:j + 1, :] * ext_ref[pl.ds(SSM_HALO - (SSM_CONV - 1) + j, q), :]
    xbc = _silu(conv)
    xs = xbc[:, 0:inner]
    dt = _softplus(p_ref[0, :, xbc_hi:xbc_hi + LANES] + dtb_ref[...])
    a = dt * ah_ref[...]
    row = lax.broadcasted_iota(jnp.int32, (q, q), 0)
    col = lax.broadcasted_iota(jnp.int32, (q, q), 1)
    causal = row >= col
    a_cum = _dot_hi(causal.astype(F32), a)
    a_cum_t = a_cum.T
    dt_full = _dot_hi(dt, e64_ref[...])
    acum_full = _dot_hi(a_cum, e64_ref[...])
    alast_full = acum_full[q - 1:q, :]
    acum_b = _dot_hi(a_cum, e128_ref[...])
    xdt = xs * dt_full
    x_to_end = xdt * jnp.exp(alast_full - acum_full)
    exp_ac = jnp.exp(acum_full)
    chunk_dec = jnp.exp(alast_full)
    lane = lax.broadcasted_iota(jnp.int32, (q, LANES), 1)
    first_head = lane < SSM_HEAD
    ys = []
    for g in range(SSM_GROUPS):
        bm = xbc[:, inner + g * SSM_STATE:inner + (g + 1) * SSM_STATE]
        cm = xbc[:, inner + (SSM_GROUPS + g) * SSM_STATE:inner + (SSM_GROUPS + g + 1) * SSM_STATE].astype(BF16)
        bt = bm.T.astype(BF16)
        cb = _dot(cm, bt)
        for pr in range(SSM_HEADS // SSM_GROUPS // 2):
            pi = g * (SSM_HEADS // SSM_GROUPS // 2) + pr
            cols = slice(pi * LANES, (pi + 1) * LANES)
            mats = []
            for hd in (2 * pi, 2 * pi + 1):
                seg = acum_b[:, hd * LANES:(hd + 1) * LANES] - a_cum_t[hd:hd + 1, :]
                mats.append((cb * jnp.exp(jnp.where(causal, seg, NEG))).astype(BF16))
            xp = xdt[:, cols]
            x_blockdiag = jnp.concatenate([jnp.where(first_head, xp, 0.0), jnp.where(first_head, 0.0, xp)], axis=0)
            y_diag = _dot(jnp.concatenate(mats, axis=1), x_blockdiag.astype(BF16))
            h_t = h_ref[pi]
            y_off = _dot(cm, h_t.astype(BF16)) * exp_ac[:, cols]
            h_ref[pi] = h_t * chunk_dec[:, cols] + _dot(bt, x_to_end[:, cols].astype(BF16))
            ys.append(y_diag + y_off)
    y = (jnp.concatenate(ys, axis=-1) + xs * dsk_ref[...]) * _silu(z)
    gsize = inner // SSM_GROUPS
    outs = []
    for g in range(SSM_GROUPS):
        yg = y[:, g * gsize:(g + 1) * gsize]
        outs.append(yg * lax.rsqrt(jnp.mean(yg * yg, axis=-1, keepdims=True) + NORM_EPS))
    o_ref[0] = (jnp.concatenate(outs, axis=-1) * nw_ref[...]).astype(o_ref.dtype)


def _mamba_mixer(pd, conv_w, conv_b, dt_bias, a_log, d_skip, norm_w):
    bsz, seq, dp = pd.shape
    q = SSM_CHUNK
    pad_heads = lambda t: jnp.zeros((1, LANES), F32).at[0, :SSM_HEADS].set(t)
    head_of = lambda width: jnp.arange(SSM_HEADS * width, dtype=jnp.int32)[None, :] // width
    expand = lambda width: (jnp.arange(LANES, dtype=jnp.int32)[:, None] == head_of(width)).astype(F32)
    full2 = lambda shape: pl.BlockSpec(shape, lambda b, i: (0, 0))
    out = pl.pallas_call(
        _ssd_kernel, grid=(bsz, seq // q),
        in_specs=[pl.BlockSpec((1, q, dp), lambda b, i: (b, i, 0)),
                  pl.BlockSpec((1, SSM_HALO, dp), lambda b, i: (b, jnp.maximum(i * (q // SSM_HALO) - 1, 0), 0)),
                  full2((SSM_CONV, SSM_CONV_DIM)), full2((1, SSM_CONV_DIM)), full2((1, LANES)), full2((1, LANES)),
                  full2((1, SSM_INNER)), full2((1, SSM_INNER)),
                  full2((LANES, SSM_HEADS * SSM_HEAD)), full2((LANES, SSM_HEADS * LANES))],
        out_specs=pl.BlockSpec((1, q, SSM_INNER), lambda b, i: (b, i, 0)),
        out_shape=SDS((bsz, seq, SSM_INNER), BF16),
        scratch_shapes=[pltpu.VMEM((q + SSM_HALO, SSM_CONV_DIM), F32),
                        pltpu.VMEM((SSM_HEADS // 2, SSM_STATE, LANES), F32)],
        compiler_params=_cp("parallel", "arbitrary"), name="ssd",
    )(pd, pd, conv_w, conv_b.reshape(1, -1), pad_heads(dt_bias), pad_heads(-jnp.exp(a_log)),
      jnp.repeat(d_skip, SSM_HEAD).reshape(1, -1), norm_w.reshape(1, -1), expand(SSM_HEAD), expand(LANES))
    return out.reshape(bsz * seq, SSM_INNER)


def _merge_kernel(pg_ref, ya_ref, ga_ref, bb_ref, bc_ref, bd_ref, gu_ref, gb_ref, wb_ref, o_ref):
    pg = pg_ref[...]
    branches = ((ya_ref[...] * ga_ref[...]).astype(BF16), bb_ref[...], bc_ref[...], bd_ref[...])
    acc = None
    for bi, br in enumerate(branches):
        term = _sigmoid(_dot(pg, gu_ref[bi]) + gb_ref[bi]) * _dot(br, wb_ref[bi])
        acc = term if acc is None else acc + term
    o_ref[...] = acc.astype(o_ref.dtype)


def _merge(pg, ya, ga, bb, bc, bd, gate_up, gate_b, w_branch, tm=512, tn=512):
    m = pg.shape[0]
    nb, kw, d = w_branch.shape
    rows = lambda k: pl.BlockSpec((tm, k), lambda i, j: (i, 0))
    return pl.pallas_call(
        _merge_kernel, grid=(m // tm, d // tn),
        in_specs=[rows(pg.shape[1]), rows(kw), rows(kw), rows(kw), rows(kw), rows(kw),
                  pl.BlockSpec((nb, pg.shape[1], tn), lambda i, j: (0, 0, j)),
                  pl.BlockSpec((nb, 1, tn), lambda i, j: (0, 0, j)),
                  pl.BlockSpec((nb, kw, tn), lambda i, j: (0, 0, j))],
        out_specs=pl.BlockSpec((tm, tn), lambda i, j: (i, j)),
        out_shape=SDS((m, d), BF16), compiler_params=_cp("parallel", "parallel"), name="merge",
    )(pg, ya, ga, bb, bc, bd, gate_up, gate_b.reshape(nb, 1, d), w_branch)


SUBLANES = 8


def _store_slabs(ref, rows):
    w = ref.shape[-1]
    for j in range(SUBLANES):
        ref[:, j, :] = rows[:, j * w:(j + 1) * w]


def _norm_route_kernel(x_ref, g_ref, sc_ref, sh_ref, wr_ref, br_ref, h_ref, ids_ref, wts_ref, hist_ref):
    x = x_ref[0]
    ms = jnp.mean(x * x, axis=-1, keepdims=True)
    h = x * lax.rsqrt(ms + NORM_EPS) * g_ref[...] * (1.0 + sc_ref[0]) + sh_ref[0]
    _store_slabs(h_ref, h)
    lg = _dot_hi(h, wr_ref[...]) + br_ref[...]
    lane = lax.broadcasted_iota(jnp.int32, lg.shape, 1)
    lane_f = lane.astype(F32)
    first = lambda hit: jnp.min(jnp.where(hit, lane_f, float(LANES)), axis=-1, keepdims=True)
    gmask = lane < MOE_GROUPS
    gl = jnp.where(gmask, lg, NEG)
    gmax = jnp.max(gl, axis=-1, keepdims=True)
    gsel = first(gl == gmax)
    gprob = 1.0 / jnp.sum(jnp.where(gmask, jnp.exp(gl - gmax), 0.0), axis=-1, keepdims=True)
    lo = MOE_GROUPS + gsel * MOE_EXPERTS_PER_GROUP
    emask = (lane_f >= lo) & (lane_f < lo + MOE_EXPERTS_PER_GROUP)
    el = jnp.where(emask, lg, NEG)
    v1 = jnp.max(el, axis=-1, keepdims=True)
    i1 = first(el == v1)
    el2 = jnp.where(lane_f == i1, NEG, el)
    v2 = jnp.max(el2, axis=-1, keepdims=True)
    i2 = first((el2 == v2) & emask & (lane_f != i1))
    t = jnp.exp(v2 - v1)
    w1 = gprob / (1.0 + t)
    w2 = gprob * t / (1.0 + t)
    ids_ref[0] = jnp.where(lane == 0, i1, jnp.where(lane == 1, i2, float(MOE_GROUPS))).astype(jnp.int32) - MOE_GROUPS
    wts_ref[0] = jnp.where(lane == 0, w1, jnp.where(lane == 1, w2, 0.0))
    chosen = jnp.where((lane_f == i1) | (lane_f == i2), 1.0, 0.0)
    hist_ref[...] = jnp.broadcast_to(jnp.sum(chosen, axis=0, keepdims=True), hist_ref.shape)


def _norm_route(x, g, scale, shift, rg_w, rg_b, re_w, re_b, tm=256):
    bsz, seq, d = x.shape
    n_log = MOE_GROUPS + MOE_EXPERTS
    wr = jnp.zeros((d, LANES), F32).at[:, :n_log].set(jnp.concatenate([rg_w, re_w], axis=1))
    br = jnp.zeros((1, LANES), F32).at[0, :n_log].set(jnp.concatenate([rg_b, re_b]))
    tok = lambda width: pl.BlockSpec((1, tm, width), lambda b, i: (b, i, 0))
    m, dw, nt = bsz * seq, d // SUBLANES, seq // tm
    h, ids, wts, hist = pl.pallas_call(
        _norm_route_kernel, grid=(bsz, nt),
        in_specs=[tok(d), pl.BlockSpec((1, d), lambda b, i: (0, 0)),
                  pl.BlockSpec((1, 1, d), lambda b, i: (b, 0, 0)), pl.BlockSpec((1, 1, d), lambda b, i: (b, 0, 0)),
                  pl.BlockSpec((d, LANES), lambda b, i: (0, 0)), pl.BlockSpec((1, LANES), lambda b, i: (0, 0))],
        out_specs=[pl.BlockSpec((tm, SUBLANES, dw), lambda b, i: (b * nt + i, 0, 0)), tok(LANES), tok(LANES),
                   pl.BlockSpec((SUBLANES, LANES), lambda b, i: (b * nt + i, 0))],
        out_shape=[SDS((m, SUBLANES, dw), F32), SDS((bsz, seq, LANES), jnp.int32), SDS((bsz, seq, LANES), F32),
                   SDS((bsz * nt * SUBLANES, LANES), F32)],
        compiler_params=_cp("parallel", "parallel"), name="norm_route",
    )(x, g.reshape(1, d), scale.reshape(bsz, 1, d), shift.reshape(bsz, 1, d), wr, br)
    counts = jnp.sum(hist[::SUBLANES, MOE_GROUPS:n_log], axis=0).astype(jnp.int32)
    return h, ids.reshape(m, LANES)[:, :MOE_TOP_K], wts.reshape(m, LANES)[:, :MOE_TOP_K], counts


ROW_COPY_UNROLL = 8


FFN_W_CHUNKS = 8


def _ffn_kernel(layer, be_ref, nb_ref, tok_ref, dst_ref, cnt_ref, h_hbm, wg_hbm, wu_hbm, wd_hbm, sw_ref, y_hbm,
                xbuf, ybuf, xrow, wg_s, wu_s, wd_s, stage_in, stage_out, gsem, ssem, wsem):
    j = pl.program_id(0)
    nb = nb_ref[0]
    cur = j % 2

    def load_expert(e):
        d, ff = wg_s.shape
        rin, rout = d // FFN_W_CHUNKS, ff // FFN_W_CHUNKS
        steps = []
        for src, dst, stage, rows in ((wg_hbm, wg_s, stage_in, rin), (wu_hbm, wu_s, stage_in, rin),
                                      (wd_hbm, wd_s, stage_out, rout)):
            for c in range(FFN_W_CHUNKS):
                steps.append((src.at[layer, e, pl.ds(c * rows, rows)], stage, dst.at[pl.ds(c * rows, rows)]))
        copies = [pltpu.make_async_copy(src, stage.at[i % 2], wsem.at[i % 2]) for i, (src, stage, _) in enumerate(steps)]
        copies[0].start()
        for i, (_, stage, dst) in enumerate(steps):
            if i + 1 < len(steps):
                copies[i + 1].start()
            copies[i].wait()
            dst[...] = stage[i % 2].astype(BF16)

    def gather_copy(buf, s, tok):
        return pltpu.make_async_copy(h_hbm.at[pl.ds(tok, 1)], xbuf.at[buf, pl.ds(s, 1)], gsem.at[buf])

    def scatter_copy(buf, s, dst):
        return pltpu.make_async_copy(ybuf.at[buf, pl.ds(s, 1)], y_hbm.at[pl.ds(dst, 1)], ssem.at[buf])

    def start_gather(blk, buf):
        def body(s, carry):
            gather_copy(buf, s, tok_ref[blk * MOE_ROWS + s]).start()
            return carry
        lax.fori_loop(0, MOE_ROWS, body, 0, unroll=ROW_COPY_UNROLL)

    def wait_gather(buf):
        pltpu.make_async_copy(h_hbm.at[pl.ds(0, MOE_ROWS)], xbuf.at[buf], gsem.at[buf]).wait()

    def start_scatter(blk, buf):
        def body(s, carry):
            scatter_copy(buf, s, dst_ref[blk * MOE_ROWS + s]).start()
            return carry
        full = cnt_ref[blk] == MOE_ROWS

        @pl.when(full)
        def _():
            lax.fori_loop(0, MOE_ROWS, body, 0, unroll=ROW_COPY_UNROLL)

        @pl.when(jnp.logical_not(full))
        def _():
            lax.fori_loop(0, cnt_ref[blk], body, 0)

    def wait_scatter(blk, buf):
        n = cnt_ref[blk]

        @pl.when(n > 0)
        def _():
            pltpu.make_async_copy(ybuf.at[buf, pl.ds(0, n)], y_hbm.at[pl.ds(0, n)], ssem.at[buf]).wait()

    @pl.when(j == 0)
    def _():
        start_gather(0, 0)

    @pl.when(j + 1 < nb)
    def _():
        start_gather(j + 1, 1 - cur)

    @pl.when(j < nb)
    def _():
        e = be_ref[j]

        @pl.when((j == 0) | (e != be_ref[jnp.maximum(j - 1, 0)]))
        def _():
            load_expert(e)

        wait_gather(cur)

        @pl.when(j >= 2)
        def _():
            wait_scatter(j - 2, cur)

        dw = xbuf.shape[-1]
        for c in range(SUBLANES):
            xrow[:, c * dw:(c + 1) * dw] = xbuf[cur, :, c, :]
        x = xrow[...].astype(BF16)
        hidden = (_silu(_dot(x, wg_s[...])) * _dot(x, wu_s[...])).astype(BF16)
        _store_slabs(ybuf.at[cur], _dot(hidden, wd_s[...]) * sw_ref[...])
        start_scatter(j, cur)

    @pl.when(j == nb - 1)
    def _():
        @pl.when(j >= 1)
        def _():
            wait_scatter(j - 1, 1 - cur)
        wait_scatter(j, cur)


def _moe_plan(expert, weight, counts):
    n_tok = expert.shape[0]
    n_assign = n_tok * MOE_TOP_K
    flat_e = expert.reshape(-1)
    flat_w = weight.reshape(-1)
    order = jnp.argsort(flat_e).astype(jnp.int32)
    padded = (counts + MOE_ROWS - 1) // MOE_ROWS * MOE_ROWS
    pad_end = jnp.cumsum(padded)
    pad_start = pad_end - padded
    start = jnp.cumsum(counts) - counts
    n_blocks = n_assign // MOE_ROWS + MOE_EXPERTS
    n_slots = n_blocks * MOE_ROWS
    blk0 = jnp.arange(n_blocks, dtype=jnp.int32) * MOE_ROWS
    block_e = jnp.minimum(jnp.searchsorted(pad_end, blk0, side='right'), MOE_EXPERTS - 1).astype(jnp.int32)
    block_cnt = jnp.clip(counts[block_e] - (blk0 - pad_start[block_e]), 0, MOE_ROWS).astype(jnp.int32)
    slot_e = jnp.repeat(block_e, MOE_ROWS)
    pos = jnp.arange(n_slots, dtype=jnp.int32) - pad_start[slot_e]
    valid = pos < counts[slot_e]
    assign = order[jnp.clip(start[slot_e] + pos, 0, n_assign - 1)]
    tok, k = assign // MOE_TOP_K, assign % MOE_TOP_K
    slot_tok = jnp.where(valid, tok, 0).astype(jnp.int32)
    slot_w = jnp.where(valid, flat_w[assign], 0.0)
    slot_dst = jnp.where(valid, k * n_tok + tok, 0).astype(jnp.int32)
    n_used_blocks = (pad_end[-1:] // MOE_ROWS).astype(jnp.int32)
    return slot_tok, slot_w, block_e, slot_dst, block_cnt, n_used_blocks


def _cast_kernel(x_ref, o_ref):
    o_ref[...] = x_ref[0].astype(o_ref.dtype)


def _cast_bf16(w, layer, block_bytes=2 * 1024 * 1024):
    _, e, r, c = w.shape
    tr = min(r, block_bytes // (4 * c))
    return pl.pallas_call(
        _cast_kernel, grid=(e, r // tr),
        in_specs=[pl.BlockSpec((1, 1, tr, c), lambda i, j: (layer, i, j, 0))],
        out_specs=pl.BlockSpec((1, tr, c), lambda i, j: (i, j, 0)),
        out_shape=SDS((e, r, c), BF16), compiler_params=_cp("parallel", "parallel"), name="cast_bf16")(w)


def _moe_apply(h, expert, weight, counts, w_gate, w_up, w_down, layer):
    n_tok, _, dw = h.shape
    d = dw * SUBLANES
    ff = w_gate.shape[-1]
    slot_tok, slot_w, block_e, slot_dst, block_cnt, n_used_blocks = _moe_plan(expert, weight, counts)
    n_slots = slot_tok.shape[0]
    any_space = pl.BlockSpec(memory_space=pl.ANY)
    return pl.pallas_call(
        functools.partial(_ffn_kernel, layer),
        grid_spec=pltpu.PrefetchScalarGridSpec(
            num_scalar_prefetch=5, grid=(n_slots // MOE_ROWS,),
            in_specs=[any_space, any_space, any_space, any_space,
                      pl.BlockSpec((MOE_ROWS, 1), lambda j, be, nb, *_: (jnp.minimum(j, nb[0] - 1), 0))],
            out_specs=any_space,
            scratch_shapes=[pltpu.VMEM((2, MOE_ROWS, SUBLANES, dw), F32), pltpu.VMEM((2, MOE_ROWS, SUBLANES, dw), F32),
                            pltpu.VMEM((MOE_ROWS, d), F32),
                            pltpu.VMEM((d, ff), BF16), pltpu.VMEM((d, ff), BF16), pltpu.VMEM((ff, d), BF16),
                            pltpu.VMEM((2, d // FFN_W_CHUNKS, ff), F32), pltpu.VMEM((2, ff // FFN_W_CHUNKS, d), F32),
                            pltpu.SemaphoreType.DMA((2,)), pltpu.SemaphoreType.DMA((2,)), pltpu.SemaphoreType.DMA((2,))]),
        out_shape=SDS((MOE_TOP_K * n_tok, SUBLANES, dw), F32), compiler_params=_cp("arbitrary"), name="moe_ffn",
    )(block_e, n_used_blocks, slot_tok, slot_dst, block_cnt, h, w_gate, w_up, w_down, slot_w.reshape(n_slots, 1))


def _moe_combine_kernel(x_ref, y0_ref, y1_ref, g_ref, o_ref):
    dw = y0_ref.shape[-1]
    for j in range(SUBLANES):
        cols = slice(j * dw, (j + 1) * dw)
        o_ref[0, :, cols] = x_ref[0, :, cols] + g_ref[0, :, cols] * (y0_ref[0, 0, :, j, :] + y1_ref[0, 0, :, j, :])


def _moe_combine(x, y2, gate, tm=256):
    bsz, seq, d = x.shape
    dw = d // SUBLANES
    y5 = y2.reshape(MOE_TOP_K, bsz, seq, SUBLANES, dw)
    tok = pl.BlockSpec((1, tm, d), lambda b, i: (b, i, 0))
    slab = lambda k: pl.BlockSpec((1, 1, tm, SUBLANES, dw), lambda b, i: (k, b, i, 0, 0))
    return pl.pallas_call(
        _moe_combine_kernel, grid=(bsz, seq // tm),
        in_specs=[tok, slab(0), slab(1), pl.BlockSpec((1, 1, d), lambda b, i: (b, 0, 0))],
        out_specs=tok, out_shape=SDS((bsz, seq, d), F32),
        compiler_params=_cp("parallel", "parallel"), name="moe_combine",
    )(x, y5, y5, gate.reshape(bsz, 1, d))


PACK_ROWS = 128


def _pack_w_in_kernel(offs, w_ref, a_ref, b0_ref, b1_ref, b2_ref, c_ref, d_ref, g_ref):
    rows = w_ref.shape[0]
    ng, wh = len(ATT_GROUPS), ATT_HEADS * ATT_HEAD
    zeros = lambda n: jnp.zeros((rows, n), BF16)
    cast = lambda lo, hi: w_ref[:, lo:hi].astype(BF16)
    o_w = offs[0] + A_MAIN
    o_a = o_w + RWKV_DECAY_RANK
    o_g = o_a + RWKV_ICLR_RANK
    a_ref[...] = jnp.concatenate(
        [cast(offs[0], o_w), cast(o_w, o_a), zeros(128 - RWKV_DECAY_RANK), cast(o_a, o_g), zeros(128 - RWKV_ICLR_RANK),
         cast(o_g, offs[1]), zeros(256 - RWKV_GATE_RANK)], axis=1)
    for gi, b_ref in enumerate((b0_ref, b1_ref, b2_ref)):
        b_ref[...] = jnp.concatenate(
            [cast(offs[1] + (t * ng + gi) * wh, offs[1] + (t * ng + gi + 1) * wh) for t in range(3)], axis=1)
    c_ref[...] = cast(offs[2], offs[3])
    d_ref[...] = jnp.concatenate([cast(offs[3], offs[4]), zeros(D_PAD - (offs[4] - offs[3]))], axis=1)
    g_ref[...] = cast(offs[4], w_ref.shape[1])


def _pack_w_in(w, layer, offs):
    _, d, n = w.shape
    wb = (offs[2] - offs[1]) // len(ATT_GROUPS)
    widths = (A_PAD, wb, wb, wb, offs[3] - offs[2], D_PAD, n - offs[4])
    return pl.pallas_call(
        functools.partial(_pack_w_in_kernel, offs), grid=(d // PACK_ROWS,),
        in_specs=[pl.BlockSpec((None, PACK_ROWS, n), lambda i: (layer, i, 0))],
        out_specs=[pl.BlockSpec((PACK_ROWS, wd), lambda i: (i, 0)) for wd in widths],
        out_shape=[SDS((d, wd), BF16) for wd in widths], compiler_params=_cp("parallel"), name="pack_w_in",
    )(w)


def kernel(x, c, ada_w, ada_b, ada_table, norm1_g, norm2_g, w_in, rwkv_mu, rwkv_w0, rwkv_w2, rwkv_a0, rwkv_a2,
           rwkv_g2, rwkv_kk, rwkv_ka, rwkv_rk, rwkv_lnx_w, rwkv_lnx_b, pool_w, pool_scale, ssm_conv_w, ssm_conv_b,
           ssm_dt_bias, ssm_a_log, ssm_d, ssm_norm_w, gate_up, gate_b, w_branch, w_out, router_group_w,
           router_group_b, router_expert_w, router_expert_b, exp_w_gate, exp_w_up, exp_w_down, final_g):
    bsz, seq, d = x.shape
    m = bsz * seq
    a_cols = rwkv_mu.shape[1]
    b_cols = 3 * len(ATT_GROUPS) * ATT_HEADS * ATT_HEAD
    d_cols = SSM_INNER + SSM_CONV_DIM + SSM_HEADS
    offs = (0, a_cols, a_cols + b_cols, a_cols + b_cols + BRANCH_WIDTH, a_cols + b_cols + BRANCH_WIDTH + d_cols)
    cond = _cond(c, ada_w, ada_b)
    for l in range(DEPTH):
        shift1, scale1, gate1, shift2, scale2, gate2 = jnp.split(cond + ada_table[l], 6, axis=-1)
        dilations = tuple(dil for _, dil in ATT_GROUPS)
        h, *h_phases = _norm_mod_phases(x, norm1_g[l], scale1, shift1, [dil for dil in dilations if dil > 1])
        h = h.reshape(m, d)
        h_of = {1: h, **{dil: hp.reshape(m, d) for dil, hp in zip([dil for dil in dilations if dil > 1], h_phases)}}
        wa, wb0, wb1, wb2, wc, wd, wg = _pack_w_in(w_in, l, offs)
        pa = _matmul(h, wa, F32).reshape(bsz, seq, A_PAD)
        pbs = [_matmul(h_of[dil], wb, BF16).reshape(bsz, dil, seq // dil, -1)
               for dil, wb in zip(dilations, (wb0, wb1, wb2))]
        pc = _matmul(h, wc, F32).reshape(bsz, seq, BRANCH_WIDTH)
        pd = _matmul(h, wd, F32, tn=640).reshape(bsz, seq, D_PAD)
        pg = _matmul(h, wg, BF16)
        ya, ga = _rwkv_mixer(pa, rwkv_mu[l], rwkv_w0[l], rwkv_w2[l], rwkv_a0[l], rwkv_a2[l], rwkv_g2[l],
                             rwkv_kk[l], rwkv_ka[l], rwkv_rk[l], rwkv_lnx_w[l], rwkv_lnx_b[l])
        ob = _attention_mixer(pbs)
        oc = _pool_mixer(pc, pool_w[l], pool_scale[l])
        od = _mamba_mixer(pd, ssm_conv_w[l], ssm_conv_b[l], ssm_dt_bias[l], ssm_a_log[l], ssm_d[l], ssm_norm_w[l])
        merged = _merge(pg, ya.reshape(m, BRANCH_WIDTH), ga.reshape(m, BRANCH_WIDTH), ob, oc, od,
                        _cast_bf16(gate_up, l), gate_b[l], _cast_bf16(w_branch, l))
        w_out_l = _cast_bf16(w_out.reshape(DEPTH, 1, d, d), l).reshape(d, d)
        x = _matmul_resid(merged, w_out_l, x.reshape(m, d), gate1, seq).reshape(bsz, seq, d)
        h2, expert, weight, counts = _norm_route(x, norm2_g[l], scale2, shift2, router_group_w[l], router_group_b[l],
                                         router_expert_w[l], router_expert_b[l])
        y2 = _moe_apply(h2, expert, weight, counts, exp_w_gate, exp_w_up, exp_w_down, l)
        x = _moe_combine(x, y2, gate2)
    zeros = jnp.zeros((bsz, d), F32)
    return _norm_mod(x, final_g, zeros, zeros, F32)
```

```python
import functools

import jax
import jax.numpy as jnp
from jax import lax
from jax.experimental import pallas as pl
from jax.experimental.pallas import tpu as pltpu

F32 = jnp.float32
BF16 = jnp.bfloat16
HI = lax.Precision.HIGHEST
SDS = jax.ShapeDtypeStruct

DEPTH = 2
NORM_EPS = 1e-6
BRANCH_WIDTH = 1024
RWKV_HEAD = 64
RWKV_HEADS = 16
RWKV_DECAY_RANK = 64
RWKV_ICLR_RANK = 64
RWKV_GATE_RANK = 160
RWKV_GN_EPS = 64e-5
RWKV_CHUNK = 64
ATT_HEAD = 128
ATT_GROUPS = ((128, 1), (512, 4), (2048, 16))
ATT_HEADS = 8
ATT_BLOCK = 128
POOL_WINDOWS = (2, 4, 8, 16)
POOL_GROUP = 256
POOL_HALO = 16
SSM_INNER = 1024
SSM_HEAD = 64
SSM_HEADS = 16
SSM_GROUPS = 4
SSM_STATE = 128
SSM_CONV = 4
SSM_CHUNK = 128
SSM_CONV_DIM = SSM_INNER + 2 * SSM_GROUPS * SSM_STATE
GATE_RANK = 256
MOE_GROUPS = 4
MOE_EXPERTS_PER_GROUP = 8
MOE_EXPERTS = 32
MOE_TOP_K = 2
MOE_FF = 512
MOE_ROWS = 256
LANES = 128
NEG = -1e30
A_MAIN = 3 * BRANCH_WIDTH
A_PAD = A_MAIN + 128 + 128 + 256
PACK_COLS = 512
D_PAD = -(-(SSM_INNER + SSM_CONV_DIM + SSM_HEADS) // PACK_COLS) * PACK_COLS
VMEM_LIMIT = 56 * 1024 * 1024


def _cp(*sem):
    return pltpu.CompilerParams(dimension_semantics=sem, vmem_limit_bytes=VMEM_LIMIT)


def _sigmoid(x):
    return 1.0 / (1.0 + jnp.exp(-x))


def _silu(x):
    return x * _sigmoid(x)


def _softplus(x):
    return jnp.maximum(x, 0.0) + jnp.log(1.0 + jnp.exp(-jnp.abs(x)))


def _dot(a, b):
    return jnp.dot(a, b, preferred_element_type=F32)


def _dot_hi(a, b):
    return jnp.dot(a, b, precision=HI, preferred_element_type=F32)


def _cond_kernel(c_ref, w_ref, b_ref, o_ref, a_ref):
    @pl.when(pl.program_id(0) == 0)
    def _():
        a_ref[...] = _silu(c_ref[...])

    k, tn = w_ref.shape
    w = w_ref[...]
    rows = []
    for r in range(a_ref.shape[0]):
        prod = w * jnp.concatenate([a_ref[r]] * (tn // LANES), axis=1)
        part = jnp.sum(prod.reshape(k // 8, 8, tn), axis=0)
        rows.append(jnp.sum(part, axis=0, keepdims=True))
    rows.append(jnp.zeros((o_ref.shape[0] - len(rows), tn), F32))
    o_ref[...] = jnp.concatenate(rows, axis=0) + b_ref[...]


def _cond(c, ada_w, ada_b):
    bsz, d = c.shape
    n = ada_w.shape[1]
    tn = 512
    c_lanes = jnp.broadcast_to(c[:, :, None], (bsz, d, LANES))
    out = pl.pallas_call(
        _cond_kernel, grid=(n // tn,),
        in_specs=[pl.BlockSpec((bsz, d, LANES), lambda j: (0, 0, 0)),
                  pl.BlockSpec((d, tn), lambda j: (0, j)),
                  pl.BlockSpec((1, tn), lambda j: (0, j))],
        out_specs=pl.BlockSpec((8, tn), lambda j: (0, j)),
        out_shape=SDS((8, n), F32), scratch_shapes=[pltpu.VMEM((bsz, d, LANES), F32)],
        compiler_params=_cp("arbitrary"), name="cond",
    )(c_lanes, ada_w, ada_b.reshape(1, n))
    return out[:bsz]


def _norm_mod_kernel(x_ref, g_ref, sc_ref, sh_ref, o_ref):
    x = x_ref[0]
    ms = jnp.mean(x * x, axis=-1, keepdims=True)
    y = x * lax.rsqrt(ms + NORM_EPS) * g_ref[...]
    o_ref[0] = (y * (1.0 + sc_ref[0]) + sh_ref[0]).astype(o_ref.dtype)


def _norm_mod(x, g, scale, shift, out_dtype, tm=256):
    bsz, seq, d = x.shape
    return pl.pallas_call(
        _norm_mod_kernel, grid=(bsz, seq // tm),
        in_specs=[pl.BlockSpec((1, tm, d), lambda b, i: (b, i, 0)),
                  pl.BlockSpec((1, d), lambda b, i: (0, 0)),
                  pl.BlockSpec((1, 1, d), lambda b, i: (b, 0, 0)),
                  pl.BlockSpec((1, 1, d), lambda b, i: (b, 0, 0))],
        out_specs=pl.BlockSpec((1, tm, d), lambda b, i: (b, i, 0)),
        out_shape=SDS((bsz, seq, d), out_dtype), compiler_params=_cp("parallel", "parallel"), name="norm_mod",
    )(x, g.reshape(1, d), scale.reshape(bsz, 1, d), shift.reshape(bsz, 1, d))


def _norm_mod_phases_kernel(x_ref, g_ref, sc_ref, sh_ref, o_ref, *refs):
    phase_refs, scr = refs[:-1], refs[-1]
    x = x_ref[0]
    tm, d = x.shape
    ms = jnp.mean(x * x, axis=-1, keepdims=True)
    y = x * lax.rsqrt(ms + NORM_EPS) * g_ref[...] * (1.0 + sc_ref[0]) + sh_ref[0]
    o_ref[0] = y.astype(o_ref.dtype)
    for ref in phase_refs:
        dil = ref.shape[1]
        n = tm // dil
        y3 = y.reshape(n, dil, d)
        for ph in range(dil):
            scr[ph * n:(ph + 1) * n, :] = y3[:, ph, :]
        for ph in range(dil):
            ref[0, ph] = scr[ph * n:(ph + 1) * n, :].astype(ref.dtype)


def _norm_mod_phases(x, g, scale, shift, dilations, tm=256):
    bsz, seq, d = x.shape
    tok = pl.BlockSpec((1, tm, d), lambda b, i: (b, i, 0))
    vec = pl.BlockSpec((1, 1, d), lambda b, i: (b, 0, 0))
    return pl.pallas_call(
        _norm_mod_phases_kernel, grid=(bsz, seq // tm),
        in_specs=[tok, pl.BlockSpec((1, d), lambda b, i: (0, 0)), vec, vec],
        out_specs=[tok] + [pl.BlockSpec((1, dil, tm // dil, d), lambda b, i: (b, 0, i, 0)) for dil in dilations],
        out_shape=[SDS((bsz, seq, d), BF16)] + [SDS((bsz, dil, seq // dil, d), BF16) for dil in dilations],
        scratch_shapes=[pltpu.VMEM((tm, d), F32)],
        compiler_params=_cp("parallel", "parallel"), name="norm_mod_phases",
    )(x, g.reshape(1, d), scale.reshape(bsz, 1, d), shift.reshape(bsz, 1, d))


def _mm_kernel(a_ref, b_ref, o_ref):
    o_ref[...] = _dot(a_ref[...], b_ref[...]).astype(o_ref.dtype)


def _matmul(a, b, out_dtype, first_block=0, n_blocks=None, tm=1024, tn=512):
    m, k = a.shape
    n = b.shape[1] if n_blocks is None else n_blocks * tn
    return pl.pallas_call(
        _mm_kernel, grid=(m // tm, n // tn),
        in_specs=[pl.BlockSpec((tm, k), lambda i, j: (i, 0)),
                  pl.BlockSpec((k, tn), lambda i, j: (0, first_block + j))],
        out_specs=pl.BlockSpec((tm, tn), lambda i, j: (i, j)),
        out_shape=SDS((m, n), out_dtype), compiler_params=_cp("parallel", "parallel"), name="matmul",
    )(a, b)


def _mm_resid_kernel(a_ref, b_ref, r_ref, g_ref, o_ref):
    o_ref[...] = r_ref[...] + g_ref[0] * _dot(a_ref[...], b_ref[...])


def _matmul_resid(a, b, resid, gate, seq, tm=1024, tn=512):
    m, k = a.shape
    n = b.shape[1]
    bsz = gate.shape[0]
    return pl.pallas_call(
        _mm_resid_kernel, grid=(m // tm, n // tn),
        in_specs=[pl.BlockSpec((tm, k), lambda i, j: (i, 0)),
                  pl.BlockSpec((k, tn), lambda i, j: (0, j)),
                  pl.BlockSpec((tm, tn), lambda i, j: (i, j)),
                  pl.BlockSpec((1, 1, tn), lambda i, j: (i * tm // seq, 0, j))],
        out_specs=pl.BlockSpec((tm, tn), lambda i, j: (i, j)),
        out_shape=SDS((m, n), F32), compiler_params=_cp("parallel", "parallel"), name="matmul_resid",
    )(a, b, resid, gate.reshape(bsz, 1, n))


def _rwkv_prep_kernel(p_ref, halo_ref, mu_ref, w0_ref, a0_ref, kkw_ref, kaw_ref, w2_ref, a2_ref, g2_ref,
                      r_ref, k_ref, v_ref, kk_ref, a_ref, lw_ref, g_ref):
    x = p_ref[0]
    w = BRANCH_WIDTH
    prev_first = jnp.where(pl.program_id(1) > 0, halo_ref[0][7:8, :], 0.0)
    row = lax.broadcasted_iota(jnp.int32, x.shape, 0)
    prev = jnp.where(row == 0, prev_first, pltpu.roll(x, 1, axis=0))
    p = x + (prev - x) * mu_ref[...]
    r, k, v = p[:, 0:w], p[:, w:2 * w], p[:, 2 * w:3 * w]
    xw, xa, xg = p[:, A_MAIN:A_MAIN + 128], p[:, A_MAIN + 128:A_MAIN + 256], p[:, A_MAIN + 256:A_PAD]
    wlog = -_softplus(-(w0_ref[...] + _dot_hi(jnp.tanh(xw), w2_ref[...]))) - 0.5
    a = _sigmoid(a0_ref[...] + _dot_hi(xa, a2_ref[...]))
    r_ref[0] = r
    k_ref[0] = k * (1.0 + (a - 1.0) * kaw_ref[...])
    v_ref[0] = v
    kk_ref[0] = k * kkw_ref[...]
    a_ref[0] = a
    lw_ref[0] = -jnp.exp(wlog)
    g_ref[0] = _dot_hi(_sigmoid(xg), g2_ref[...])


def _rwkv_prep(pa, mu, w0, a0, kkw, kaw, w2, a2, g2, tm=256):
    bsz, seq, ap = pa.shape
    w = BRANCH_WIDTH
    row = lambda t: t.reshape(1, -1)
    full = lambda shape: pl.BlockSpec(shape, lambda b, i: (0, 0))
    out_spec = pl.BlockSpec((1, tm, w), lambda b, i: (b, i, 0))
    return pl.pallas_call(
        _rwkv_prep_kernel, grid=(bsz, seq // tm),
        in_specs=[pl.BlockSpec((1, tm, ap), lambda b, i: (b, i, 0)),
                  pl.BlockSpec((1, 8, ap), lambda b, i: (b, jnp.maximum(i * (tm // 8) - 1, 0), 0)),
                  full((1, ap)), full((1, w)), full((1, w)), full((1, w)), full((1, w)),
                  full((128, w)), full((128, w)), full((256, w))],
        out_specs=[out_spec] * 7,
        out_shape=[SDS((bsz, seq, w), F32)] * 7,
        compiler_params=_cp("parallel", "parallel"), name="rwkv_prep",
    )(pa, pa, row(mu), row(w0), row(a0), row(kkw), row(kaw), w2, a2, g2)


def _rwkv_rec_kernel(r_ref, k_ref, v_ref, kk_ref, a_ref, lw_ref, rk_ref, lnw_ref, lnb_ref, o_ref, s_ref):
    @pl.when(pl.program_id(1) == 0)
    def _():
        s_ref[...] = jnp.zeros_like(s_ref)

    nh, n = s_ref.shape[0], s_ref.shape[1]
    heads = lambda ref: jnp.stack([ref[0, :, h * n:(h + 1) * n] for h in range(nh)])
    r, k, v, kk, a, lw = (heads(ref) for ref in (r_ref, k_ref, v_ref, kk_ref, a_ref, lw_ref))
    c = r.shape[1]
    kn = kk / jnp.maximum(jnp.sqrt(jnp.sum(kk * kk, axis=-1, keepdims=True)), 1e-12)
    b = kn * a
    row = lax.broadcasted_iota(jnp.int32, (c, c), 0)
    col = lax.broadcasted_iota(jnp.int32, (c, c), 1)
    strict, incl = (row > col)[None], (row >= col)[None]
    bdot = lambda spec: (lambda x, y: jnp.einsum(spec, x.astype(BF16), y.astype(BF16), preferred_element_type=F32))
    nt, nn, tn = bdot('hik,hjk->hij'), bdot('hij,hjk->hik'), bdot('hiv,hik->hvk')
    tril = jnp.broadcast_to((row >= col).astype(BF16)[None], (nh, c, c))
    lw_hi = lw.astype(BF16)
    lw_r1 = lw - lw_hi.astype(F32)
    lw_mid = lw_r1.astype(BF16)
    lw_lo = (lw_r1 - lw_mid.astype(F32)).astype(BF16)
    lc = nn(tril, lw_hi) + (nn(tril, lw_mid) + nn(tril, lw_lo))
    lc_last = lc[:, c - 1:c, :]
    e_neg, e_end = jnp.exp(-lc), jnp.exp(lc_last - lc)
    kt, rt = kn * jnp.exp(lc - lw), r * jnp.exp(lc)
    bt, kd = b * e_neg, k * e_neg
    kr = jnp.concatenate([kt, rt], axis=1)
    sc = nt(kr, jnp.concatenate([bt, kd], axis=1))
    lm = jnp.where(strict, -sc[:, :c, :c], 0.0)
    ak = jnp.where(strict, sc[:, :c, c:], 0.0)
    bb = jnp.where(incl, sc[:, c:, :c], 0.0)
    bk = jnp.where(incl, sc[:, c:, c:], 0.0)
    s0 = s_ref[...]
    kr_s0 = nt(kr, s0)
    abk_v = nn(jnp.concatenate([ak, bk], axis=1), v)
    u = -(kr_s0[:, :c] + abk_v[:, :c])
    pw = lm
    n_doublings = c.bit_length() - 1
    for it in range(n_doublings):
        u = u + nn(pw, u)
        if it + 1 < n_doublings:
            pw = nn(pw, pw)
    y = kr_s0[:, c:] + nn(bb, u) + abk_v[:, c:]
    s_ref[...] = s0 * jnp.exp(lc_last) + tn(jnp.concatenate([u, v], axis=1),
                                            jnp.concatenate([b * e_end, k * e_end], axis=1))
    mean = jnp.mean(y, axis=-1, keepdims=True)
    var = jnp.mean(jnp.square(y - mean), axis=-1, keepdims=True)
    out = (y - mean) * lax.rsqrt(var + RWKV_GN_EPS) * lnw_ref[...] + lnb_ref[...]
    out = out + jnp.sum(r * k * rk_ref[...], axis=-1, keepdims=True) * v
    for h in range(nh):
        o_ref[0, :, h * n:(h + 1) * n] = out[h]


def _rwkv_rec(r, k, v, kk, a, lw, rk, lnw, lnb, chunk=RWKV_CHUNK):
    bsz, seq, w = r.shape
    nh, n = rk.shape
    blk = pl.BlockSpec((1, chunk, w), lambda b, i: (b, i, 0))
    par = pl.BlockSpec((nh, 1, n), lambda b, i: (0, 0, 0))
    return pl.pallas_call(
        _rwkv_rec_kernel, grid=(bsz, seq // chunk),
        in_specs=[blk] * 6 + [par] * 3, out_specs=blk,
        out_shape=SDS((bsz, seq, w), F32),
        scratch_shapes=[pltpu.VMEM((nh, n, n), F32)],
        compiler_params=_cp("parallel", "arbitrary"), name="rwkv_rec",
    )(r, k, v, kk, a, lw, rk.reshape(nh, 1, n), lnw.reshape(nh, 1, n), lnb.reshape(nh, 1, n))


def _pad_rows(w, rows):
    return jnp.zeros((rows, w.shape[1]), w.dtype).at[:w.shape[0]].set(w)


def _rwkv_pack_cols(t):
    lead = t.shape[:-1]
    z = lambda n: jnp.zeros(lead + (n,), t.dtype)
    o1 = A_MAIN + RWKV_DECAY_RANK
    o2 = o1 + RWKV_ICLR_RANK
    return jnp.concatenate([t[..., :A_MAIN], t[..., A_MAIN:o1], z(128 - RWKV_DECAY_RANK), t[..., o1:o2],
                            z(128 - RWKV_ICLR_RANK), t[..., o2:], z(256 - RWKV_GATE_RANK)], axis=-1)


def _rwkv_mixer(pa, mu, w0, w2, a0, a2, g2, kkw, kaw, rk, lnw, lnb):
    bsz, seq, _ = pa.shape
    r, k, v, kk, a, lw, g = _rwkv_prep(pa, _rwkv_pack_cols(mu), w0, a0, kkw, kaw,
                                       _pad_rows(w2, 128), _pad_rows(a2, 128), _pad_rows(g2, 256))
    y = _rwkv_rec(r, k, v, kk, a, lw, rk, lnw.reshape(RWKV_HEADS, RWKV_HEAD), lnb.reshape(RWKV_HEADS, RWKV_HEAD))
    return y, g


def _att_kernel(q_ref, kc_ref, kp_ref, vc_ref, vp_ref, o_ref, lse_ref):
    nt = lambda x, y: lax.dot_general(x, y, (((1,), (1,)), ((), ())), preferred_element_type=F32)
    scale = ATT_HEAD ** -0.5
    qi = lax.broadcasted_iota(jnp.int32, (ATT_BLOCK, ATT_BLOCK), 0)
    ki = lax.broadcasted_iota(jnp.int32, (ATT_BLOCK, ATT_BLOCK), 1)
    cur_ok = ki <= qi
    prev_ok = (ki >= qi) & (pl.program_id(2) > 0)
    lane = lax.broadcasted_iota(jnp.int32, (ATT_BLOCK, LANES), 1)
    lse_all = jnp.zeros((ATT_BLOCK, LANES), F32)
    for h in range(ATT_HEADS):
        cols = slice(h * ATT_HEAD, (h + 1) * ATT_HEAD)
        q = q_ref[0, 0, :, cols]
        s_c = jnp.where(cur_ok, nt(q, kc_ref[0, 0, :, cols]) * scale, NEG)
        s_p = jnp.where(prev_ok, nt(q, kp_ref[0, 0, :, cols]) * scale, NEG)
        m = jnp.maximum(jnp.max(s_c, axis=-1, keepdims=True), jnp.max(s_p, axis=-1, keepdims=True))
        e_c, e_p = jnp.exp(s_c - m), jnp.exp(s_p - m)
        den = jnp.sum(e_c, axis=-1, keepdims=True) + jnp.sum(e_p, axis=-1, keepdims=True)
        o = _dot(e_c.astype(BF16), vc_ref[0, 0, :, cols]) + _dot(e_p.astype(BF16), vp_ref[0, 0, :, cols])
        o_ref[0, 0, :, cols] = o / den
        lse_all = jnp.where(lane == h, m + jnp.log(den), lse_all)
    lse_ref[0, 0] = lse_all


def _att_group(pb, gi):
    bsz, dilation, n_phase, _ = pb.shape
    w = ATT_HEADS * ATT_HEAD
    nblk = n_phase // ATT_BLOCK
    blk = (1, 1, ATT_BLOCK, w)
    cur = lambda t: pl.BlockSpec(blk, lambda b, ph, i: (b, ph, i, t))
    prev = lambda t: pl.BlockSpec(blk, lambda b, ph, i: (b, ph, jnp.maximum(i - 1, 0), t))
    return pl.pallas_call(
        _att_kernel, grid=(bsz, dilation, nblk),
        in_specs=[cur(0), cur(1), prev(1), cur(2), prev(2)],
        out_specs=[pl.BlockSpec(blk, lambda b, ph, i: (b, ph, i, 0)),
                   pl.BlockSpec((1, 1, ATT_BLOCK, LANES), lambda b, ph, i: (b, ph, i, 0))],
        out_shape=[SDS((bsz, dilation, n_phase, w), F32), SDS((bsz, dilation, n_phase, LANES), F32)],
        compiler_params=_cp("parallel", "parallel", "arbitrary"), name=f"att_g{gi}",
    )(pb, pb, pb, pb, pb)


def _att_combine_kernel(o0, o1, o2, l0, l1, l2, e_ref, out_ref):
    tm = out_ref.shape[0]

    def tokens(ref):
        d = ref.shape[1]
        if d == 1:
            return ref[0, 0]
        return jnp.stack([ref[0, ph] for ph in range(d)], axis=1).reshape(tm, ref.shape[-1])

    lses = [tokens(l) for l in (l0, l1, l2)]
    m = jnp.maximum(jnp.maximum(lses[0], lses[1]), lses[2])
    es = [jnp.exp(l - m) for l in lses]
    inv = 1.0 / (es[0] + es[1] + es[2])
    acc = None
    for e, o in zip(es, (o0, o1, o2)):
        term = _dot_hi(e * inv, e_ref[...]) * tokens(o)
        acc = term if acc is None else acc + term
    out_ref[...] = acc.astype(out_ref.dtype)


def _attention_mixer(pbs, tm=256):
    res = [_att_group(pb, gi) for gi, pb in enumerate(pbs)]
    bsz, _, seq, w = res[0][0].shape
    nt = seq // tm
    spec = lambda t: pl.BlockSpec((1, t.shape[1], tm // t.shape[1], t.shape[3]), lambda b, i: (b, 0, i, 0))
    expand = (jnp.arange(LANES, dtype=jnp.int32)[:, None] == jnp.arange(w, dtype=jnp.int32)[None, :] // ATT_HEAD).astype(F32)
    args = [t[0] for t in res] + [t[1] for t in res]
    return pl.pallas_call(
        _att_combine_kernel, grid=(bsz, nt),
        in_specs=[spec(t) for t in args] + [pl.BlockSpec((LANES, w), lambda b, i: (0, 0))],
        out_specs=pl.BlockSpec((tm, w), lambda b, i: (b * nt + i, 0)),
        out_shape=SDS((bsz * seq, w), BF16), compiler_params=_cp("parallel", "parallel"), name="att_combine",
    )(*args, expand)


def _pool_kernel(x_ref, halo_ref, w_ref, sc_ref, o_ref, ext_ref):
    tm = x_ref.shape[1]
    i = pl.program_id(1)
    x = x_ref[0]
    ext_ref[0:POOL_HALO, :] = jnp.where(i > 0, halo_ref[0], 0.0)
    ext_ref[POOL_HALO:, :] = x
    pos = i * tm + lax.broadcasted_iota(jnp.int32, (tm, POOL_GROUP), 0)
    outs = []
    for gi, win in enumerate(POOL_WINDOWS):
        cols = slice(gi * POOL_GROUP, (gi + 1) * POOL_GROUP)
        xg = x[:, cols]
        s = xg
        for j in range(1, win):
            s = s + ext_ref[pl.ds(POOL_HALO - j, tm), cols]
        mixed = s / jnp.minimum(pos + 1, win).astype(F32) - xg
        outs.append(_dot(mixed.astype(BF16), w_ref[gi]))
    o_ref[0] = (jnp.concatenate(outs, axis=-1) * sc_ref[...]).astype(o_ref.dtype)


def _pool_mixer(pc, w_pool, scale, tm=256):
    bsz, seq, w = pc.shape
    out = pl.pallas_call(
        _pool_kernel, grid=(bsz, seq // tm),
        in_specs=[pl.BlockSpec((1, tm, w), lambda b, i: (b, i, 0)),
                  pl.BlockSpec((1, POOL_HALO, w), lambda b, i: (b, jnp.maximum(i * (tm // POOL_HALO) - 1, 0), 0)),
                  pl.BlockSpec(w_pool.shape, lambda b, i: (0, 0, 0)),
                  pl.BlockSpec((1, w), lambda b, i: (0, 0))],
        out_specs=pl.BlockSpec((1, tm, w), lambda b, i: (b, i, 0)),
        out_shape=SDS((bsz, seq, w), BF16),
        scratch_shapes=[pltpu.VMEM((tm + POOL_HALO, w), F32)],
        compiler_params=_cp("parallel", "parallel"), name="pool",
    )(pc, pc, w_pool.astype(BF16), scale.reshape(1, w))
    return out.reshape(bsz * seq, w)


SSM_HALO = 8


def _ssd_kernel(p_ref, halo_ref, cw_ref, cb_ref, dtb_ref, ah_ref, dsk_ref, nw_ref, e64_ref, e128_ref,
                o_ref, ext_ref, h_ref):
    ci = pl.program_id(1)
    q = SSM_CHUNK
    inner = SSM_INNER
    xbc_lo, xbc_hi = inner, inner + SSM_CONV_DIM

    @pl.when(ci == 0)
    def _():
        h_ref[...] = jnp.zeros_like(h_ref)

    z = p_ref[0, :, 0:inner]
    ext_ref[0:SSM_HALO, :] = jnp.where(ci > 0, halo_ref[0, :, xbc_lo:xbc_hi], 0.0)
    ext_ref[SSM_HALO:, :] = p_ref[0, :, xbc_lo:xbc_hi]
    conv = cb_ref[...]
    for j in range(SSM_CONV):
        conv = conv + cw_ref[j:j + 1, :] * ext_ref[pl.ds(SSM_HALO - (SSM_CONV - 1) + j, q), :]
    xbc = _silu(conv)
    xs = xbc[:, 0:inner]
    dt = _softplus(p_ref[0, :, xbc_hi:xbc_hi + LANES] + dtb_ref[...])
    a = dt * ah_ref[...]
    row = lax.broadcasted_iota(jnp.int32, (q, q), 0)
    col = lax.broadcasted_iota(jnp.int32, (q, q), 1)
    causal = row >= col
    a_cum = _dot_hi(causal.astype(F32), a)
    a_cum_t = a_cum.T
    dt_full = _dot_hi(dt, e64_ref[...])
    acum_full = _dot_hi(a_cum, e64_ref[...])
    alast_full = acum_full[q - 1:q, :]
    acum_b = _dot_hi(a_cum, e128_ref[...])
    xdt = xs * dt_full
    x_to_end = xdt * jnp.exp(alast_full - acum_full)
    exp_ac = jnp.exp(acum_full)
    chunk_dec = jnp.exp(alast_full)
    lane = lax.broadcasted_iota(jnp.int32, (q, LANES), 1)
    first_head = lane < SSM_HEAD
    ys = []
    for g in range(SSM_GROUPS):
        bm = xbc[:, inner + g * SSM_STATE:inner + (g + 1) * SSM_STATE]
        cm = xbc[:, inner + (SSM_GROUPS + g) * SSM_STATE:inner + (SSM_GROUPS + g + 1) * SSM_STATE].astype(BF16)
        bt = bm.T.astype(BF16)
        cb = _dot(cm, bt)
        for pr in range(SSM_HEADS // SSM_GROUPS // 2):
            pi = g * (SSM_HEADS // SSM_GROUPS // 2) + pr
            cols = slice(pi * LANES, (pi + 1) * LANES)
            mats = []
            for hd in (2 * pi, 2 * pi + 1):
                seg = acum_b[:, hd * LANES:(hd + 1) * LANES] - a_cum_t[hd:hd + 1, :]
                mats.append((cb * jnp.exp(jnp.where(causal, seg, NEG))).astype(BF16))
            xp = xdt[:, cols]
            x_blockdiag = jnp.concatenate([jnp.where(first_head, xp, 0.0), jnp.where(first_head, 0.0, xp)], axis=0)
            y_diag = _dot(jnp.concatenate(mats, axis=1), x_blockdiag.astype(BF16))
            h_t = h_ref[pi]
            y_off = _dot(cm, h_t.astype(BF16)) * exp_ac[:, cols]
            h_ref[pi] = h_t * chunk_dec[:, cols] + _dot(bt, x_to_end[:, cols].astype(BF16))
            ys.append(y_diag + y_off)
    y = (jnp.concatenate(ys, axis=-1) + xs * dsk_ref[...]) * _silu(z)
    gsize = inner // SSM_GROUPS
    outs = []
    for g in range(SSM_GROUPS):
        yg = y[:, g * gsize:(g + 1) * gsize]
        outs.append(yg * lax.rsqrt(jnp.mean(yg * yg, axis=-1, keepdims=True) + NORM_EPS))
    o_ref[0] = (jnp.concatenate(outs, axis=-1) * nw_ref[...]).astype(o_ref.dtype)


def _mamba_mixer(pd, conv_w, conv_b, dt_bias, a_log, d_skip, norm_w):
    bsz, seq, dp = pd.shape
    q = SSM_CHUNK
    pad_heads = lambda t: jnp.zeros((1, LANES), F32).at[0, :SSM_HEADS].set(t)
    head_of = lambda width: jnp.arange(SSM_HEADS * width, dtype=jnp.int32)[None, :] // width
    expand = lambda width: (jnp.arange(LANES, dtype=jnp.int32)[:, None] == head_of(width)).astype(F32)
    full2 = lambda shape: pl.BlockSpec(shape, lambda b, i: (0, 0))
    out = pl.pallas_call(
        _ssd_kernel, grid=(bsz, seq // q),
        in_specs=[pl.BlockSpec((1, q, dp), lambda b, i: (b, i, 0)),
                  pl.BlockSpec((1, SSM_HALO, dp), lambda b, i: (b, jnp.maximum(i * (q // SSM_HALO) - 1, 0), 0)),
                  full2((SSM_CONV, SSM_CONV_DIM)), full2((1, SSM_CONV_DIM)), full2((1, LANES)), full2((1, LANES)),
                  full2((1, SSM_INNER)), full2((1, SSM_INNER)),
                  full2((LANES, SSM_HEADS * SSM_HEAD)), full2((LANES, SSM_HEADS * LANES))],
        out_specs=pl.BlockSpec((1, q, SSM_INNER), lambda b, i: (b, i, 0)),
        out_shape=SDS((bsz, seq, SSM_INNER), BF16),
        scratch_shapes=[pltpu.VMEM((q + SSM_HALO, SSM_CONV_DIM), F32),
                        pltpu.VMEM((SSM_HEADS // 2, SSM_STATE, LANES), F32)],
        compiler_params=_cp("parallel", "arbitrary"), name="ssd",
    )(pd, pd, conv_w, conv_b.reshape(1, -1), pad_heads(dt_bias), pad_heads(-jnp.exp(a_log)),
      jnp.repeat(d_skip, SSM_HEAD).reshape(1, -1), norm_w.reshape(1, -1), expand(SSM_HEAD), expand(LANES))
    return out.reshape(bsz * seq, SSM_INNER)


def _merge_kernel(pg_ref, ya_ref, ga_ref, bb_ref, bc_ref, bd_ref, gu_ref, gb_ref, wb_ref, o_ref):
    pg = pg_ref[...]
    branches = ((ya_ref[...] * ga_ref[...]).astype(BF16), bb_ref[...], bc_ref[...], bd_ref[...])
    acc = None
    for bi, br in enumerate(branches):
        term = _sigmoid(_dot(pg, gu_ref[bi]) + gb_ref[bi]) * _dot(br, wb_ref[bi])
        acc = term if acc is None else acc + term
    o_ref[...] = acc.astype(o_ref.dtype)


def _merge(pg, ya, ga, bb, bc, bd, gate_up, gate_b, w_branch, tm=512, tn=512):
    m = pg.shape[0]
    nb, kw, d = w_branch.shape
    rows = lambda k: pl.BlockSpec((tm, k), lambda i, j: (i, 0))
    return pl.pallas_call(
        _merge_kernel, grid=(m // tm, d // tn),
        in_specs=[rows(gate_up.shape[1]), rows(kw), rows(kw), rows(kw), rows(kw), rows(kw),
                  pl.BlockSpec((nb, gate_up.shape[1], tn), lambda i, j: (0, 0, j)),
                  pl.BlockSpec((nb, 1, tn), lambda i, j: (0, 0, j)),
                  pl.BlockSpec((nb, kw, tn), lambda i, j: (0, 0, j))],
        out_specs=pl.BlockSpec((tm, tn), lambda i, j: (i, j)),
        out_shape=SDS((m, d), BF16), compiler_params=_cp("parallel", "parallel"), name="merge",
    )(pg, ya, ga, bb, bc, bd, gate_up, gate_b.reshape(nb, 1, d), w_branch)


SUBLANES = 8


def _store_slabs(ref, rows):
    w = ref.shape[-1]
    for j in range(SUBLANES):
        ref[:, j, :] = rows[:, j * w:(j + 1) * w]


def _norm_route_kernel(x_ref, g_ref, sc_ref, sh_ref, wr_ref, br_ref, h_ref, ids_ref, wts_ref, hist_ref):
    x = x_ref[0]
    ms = jnp.mean(x * x, axis=-1, keepdims=True)
    h = x * lax.rsqrt(ms + NORM_EPS) * g_ref[...] * (1.0 + sc_ref[0]) + sh_ref[0]
    _store_slabs(h_ref, h)
    lg = _dot_hi(h, wr_ref[...]) + br_ref[...]
    lane = lax.broadcasted_iota(jnp.int32, lg.shape, 1)
    lane_f = lane.astype(F32)
    first = lambda hit: jnp.min(jnp.where(hit, lane_f, float(LANES)), axis=-1, keepdims=True)
    gmask = lane < MOE_GROUPS
    gl = jnp.where(gmask, lg, NEG)
    gmax = jnp.max(gl, axis=-1, keepdims=True)
    gsel = first(gl == gmax)
    gprob = 1.0 / jnp.sum(jnp.where(gmask, jnp.exp(gl - gmax), 0.0), axis=-1, keepdims=True)
    lo = MOE_GROUPS + gsel * MOE_EXPERTS_PER_GROUP
    emask = (lane_f >= lo) & (lane_f < lo + MOE_EXPERTS_PER_GROUP)
    el = jnp.where(emask, lg, NEG)
    v1 = jnp.max(el, axis=-1, keepdims=True)
    i1 = first(el == v1)
    el2 = jnp.where(lane_f == i1, NEG, el)
    v2 = jnp.max(el2, axis=-1, keepdims=True)
    i2 = first((el2 == v2) & emask & (lane_f != i1))
    t = jnp.exp(v2 - v1)
    w1 = gprob / (1.0 + t)
    w2 = gprob * t / (1.0 + t)
    ids_ref[0] = jnp.where(lane == 0, i1, jnp.where(lane == 1, i2, float(MOE_GROUPS))).astype(jnp.int32) - MOE_GROUPS
    wts_ref[0] = jnp.where(lane == 0, w1, jnp.where(lane == 1, w2, 0.0))
    chosen = jnp.where((lane_f == i1) | (lane_f == i2), 1.0, 0.0)
    hist_ref[...] = jnp.broadcast_to(jnp.sum(chosen, axis=0, keepdims=True), hist_ref.shape)


def _norm_route(x, g, scale, shift, rg_w, rg_b, re_w, re_b, tm=256):
    bsz, seq, d = x.shape
    n_log = MOE_GROUPS + MOE_EXPERTS
    wr = jnp.zeros((d, LANES), F32).at[:, :n_log].set(jnp.concatenate([rg_w, re_w], axis=1))
    br = jnp.zeros((1, LANES), F32).at[0, :n_log].set(jnp.concatenate([rg_b, re_b]))
    tok = lambda width: pl.BlockSpec((1, tm, width), lambda b, i: (b, i, 0))
    m, dw, nt = bsz * seq, d // SUBLANES, seq // tm
    h, ids, wts, hist = pl.pallas_call(
        _norm_route_kernel, grid=(bsz, nt),
        in_specs=[tok(d), pl.BlockSpec((1, d), lambda b, i: (0, 0)),
                  pl.BlockSpec((1, 1, d), lambda b, i: (b, 0, 0)), pl.BlockSpec((1, 1, d), lambda b, i: (b, 0, 0)),
                  pl.BlockSpec((d, LANES), lambda b, i: (0, 0)), pl.BlockSpec((1, LANES), lambda b, i: (0, 0))],
        out_specs=[pl.BlockSpec((tm, SUBLANES, dw), lambda b, i: (b * nt + i, 0, 0)), tok(LANES), tok(LANES),
                   pl.BlockSpec((SUBLANES, LANES), lambda b, i: (b * nt + i, 0))],
        out_shape=[SDS((m, SUBLANES, dw), F32), SDS((bsz, seq, LANES), jnp.int32), SDS((bsz, seq, LANES), F32),
                   SDS((bsz * nt * SUBLANES, LANES), F32)],
        compiler_params=_cp("parallel", "parallel"), name="norm_route",
    )(x, g.reshape(1, d), scale.reshape(bsz, 1, d), shift.reshape(bsz, 1, d), wr, br)
    counts = jnp.sum(hist[::SUBLANES, MOE_GROUPS:n_log], axis=0).astype(jnp.int32)
    return h, ids.reshape(m, LANES)[:, :MOE_TOP_K], wts.reshape(m, LANES)[:, :MOE_TOP_K], counts


ROW_COPY_UNROLL = 8


FFN_W_CHUNKS = 4


def _ffn_kernel(layer, be_ref, nb_ref, tok_ref, dst_ref, cnt_ref, h_hbm, wg_hbm, wu_hbm, wd_hbm, sw_ref, y_hbm,
                xbuf, ybuf, xrow, wg_s, wu_s, wd_s, stage_in, stage_out, gsem, ssem, wsem):
    j = pl.program_id(0)
    nb = nb_ref[0]
    cur = j % 2

    def load_expert(e):
        d, ff = wg_s.shape
        rin, rout = d // FFN_W_CHUNKS, ff // FFN_W_CHUNKS
        steps = []
        for src, dst, stage, rows in ((wg_hbm, wg_s, stage_in, rin), (wu_hbm, wu_s, stage_in, rin),
                                      (wd_hbm, wd_s, stage_out, rout)):
            for c in range(FFN_W_CHUNKS):
                steps.append((src.at[layer, e, pl.ds(c * rows, rows)], stage, dst.at[pl.ds(c * rows, rows)]))
        copies = [pltpu.make_async_copy(src, stage.at[i % 2], wsem.at[i % 2]) for i, (src, stage, _) in enumerate(steps)]
        copies[0].start()
        for i, (_, stage, dst) in enumerate(steps):
            if i + 1 < len(steps):
                copies[i + 1].start()
            copies[i].wait()
            dst[...] = stage[i % 2].astype(BF16)

    def gather_copy(buf, s, tok):
        return pltpu.make_async_copy(h_hbm.at[pl.ds(tok, 1)], xbuf.at[buf, pl.ds(s, 1)], gsem.at[buf])

    def scatter_copy(buf, s, dst):
        return pltpu.make_async_copy(ybuf.at[buf, pl.ds(s, 1)], y_hbm.at[pl.ds(dst, 1)], ssem.at[buf])

    def start_gather(blk, buf):
        def body(s, carry):
            gather_copy(buf, s, tok_ref[blk * MOE_ROWS + s]).start()
            return carry
        lax.fori_loop(0, MOE_ROWS, body, 0, unroll=ROW_COPY_UNROLL)

    def wait_gather(buf):
        pltpu.make_async_copy(h_hbm.at[pl.ds(0, MOE_ROWS)], xbuf.at[buf], gsem.at[buf]).wait()

    def start_scatter(blk, buf):
        def body(s, carry):
            scatter_copy(buf, s, dst_ref[blk * MOE_ROWS + s]).start()
            return carry
        full = cnt_ref[blk] == MOE_ROWS

        @pl.when(full)
        def _():
            lax.fori_loop(0, MOE_ROWS, body, 0, unroll=ROW_COPY_UNROLL)

        @pl.when(jnp.logical_not(full))
        def _():
            lax.fori_loop(0, cnt_ref[blk], body, 0)

    def wait_scatter(blk, buf):
        n = cnt_ref[blk]

        @pl.when(n > 0)
        def _():
            pltpu.make_async_copy(ybuf.at[buf, pl.ds(0, n)], y_hbm.at[pl.ds(0, n)], ssem.at[buf]).wait()

    @pl.when(j == 0)
    def _():
        start_gather(0, 0)

    @pl.when(j + 1 < nb)
    def _():
        start_gather(j + 1, 1 - cur)

    @pl.when(j < nb)
    def _():
        e = be_ref[j]

        @pl.when((j == 0) | (e != be_ref[jnp.maximum(j - 1, 0)]))
        def _():
            load_expert(e)

        wait_gather(cur)

        @pl.when(j >= 2)
        def _():
            wait_scatter(j - 2, cur)

        dw = xbuf.shape[-1]
        for c in range(SUBLANES):
            xrow[:, c * dw:(c + 1) * dw] = xbuf[cur, :, c, :]
        x = xrow[...].astype(BF16)
        hidden = (_silu(_dot(x, wg_s[...])) * _dot(x, wu_s[...])).astype(BF16)
        _store_slabs(ybuf.at[cur], _dot(hidden, wd_s[...]) * sw_ref[...])
        start_scatter(j, cur)

    @pl.when(j == nb - 1)
    def _():
        @pl.when(j >= 1)
        def _():
            wait_scatter(j - 1, 1 - cur)
        wait_scatter(j, cur)


def _moe_plan(expert, weight, counts):
    n_tok = expert.shape[0]
    n_assign = n_tok * MOE_TOP_K
    flat_e = expert.reshape(-1)
    flat_w = weight.reshape(-1)
    order = jnp.argsort(flat_e).astype(jnp.int32)
    padded = (counts + MOE_ROWS - 1) // MOE_ROWS * MOE_ROWS
    pad_end = jnp.cumsum(padded)
    pad_start = pad_end - padded
    start = jnp.cumsum(counts) - counts
    n_blocks = n_assign // MOE_ROWS + MOE_EXPERTS
    n_slots = n_blocks * MOE_ROWS
    blk0 = jnp.arange(n_blocks, dtype=jnp.int32) * MOE_ROWS
    block_e = jnp.minimum(jnp.searchsorted(pad_end, blk0, side='right'), MOE_EXPERTS - 1).astype(jnp.int32)
    block_cnt = jnp.clip(counts[block_e] - (blk0 - pad_start[block_e]), 0, MOE_ROWS).astype(jnp.int32)
    slot_e = jnp.repeat(block_e, MOE_ROWS)
    pos = jnp.arange(n_slots, dtype=jnp.int32) - pad_start[slot_e]
    valid = pos < counts[slot_e]
    assign = order[jnp.clip(start[slot_e] + pos, 0, n_assign - 1)]
    tok, k = assign // MOE_TOP_K, assign % MOE_TOP_K
    slot_tok = jnp.where(valid, tok, 0).astype(jnp.int32)
    slot_w = jnp.where(valid, flat_w[assign], 0.0)
    slot_dst = jnp.where(valid, k * n_tok + tok, 0).astype(jnp.int32)
    n_used_blocks = (pad_end[-1:] // MOE_ROWS).astype(jnp.int32)
    return slot_tok, slot_w, block_e, slot_dst, block_cnt, n_used_blocks


def _cast_kernel(x_ref, o_ref):
    o_ref[...] = x_ref[0].astype(o_ref.dtype)


def _cast_bf16(w, layer, block_bytes=2 * 1024 * 1024):
    _, e, r, c = w.shape
    tr = min(r, block_bytes // (4 * c))
    return pl.pallas_call(
        _cast_kernel, grid=(e, r // tr),
        in_specs=[pl.BlockSpec((1, 1, tr, c), lambda i, j: (layer, i, j, 0))],
        out_specs=pl.BlockSpec((1, tr, c), lambda i, j: (i, j, 0)),
        out_shape=SDS((e, r, c), BF16), compiler_params=_cp("parallel", "parallel"), name="cast_bf16")(w)


def _moe_apply(h, expert, weight, counts, w_gate, w_up, w_down, layer):
    n_tok, _, dw = h.shape
    d = dw * SUBLANES
    ff = w_gate.shape[-1]
    slot_tok, slot_w, block_e, slot_dst, block_cnt, n_used_blocks = _moe_plan(expert, weight, counts)
    n_slots = slot_tok.shape[0]
    any_space = pl.BlockSpec(memory_space=pl.ANY)
    return pl.pallas_call(
        functools.partial(_ffn_kernel, layer),
        grid_spec=pltpu.PrefetchScalarGridSpec(
            num_scalar_prefetch=5, grid=(n_slots // MOE_ROWS,),
            in_specs=[any_space, any_space, any_space, any_space,
                      pl.BlockSpec((MOE_ROWS, 1), lambda j, be, nb, *_: (jnp.minimum(j, nb[0] - 1), 0))],
            out_specs=any_space,
            scratch_shapes=[pltpu.VMEM((2, MOE_ROWS, SUBLANES, dw), F32), pltpu.VMEM((2, MOE_ROWS, SUBLANES, dw), F32),
                            pltpu.VMEM((MOE_ROWS, d), F32),
                            pltpu.VMEM((d, ff), BF16), pltpu.VMEM((d, ff), BF16), pltpu.VMEM((ff, d), BF16),
                            pltpu.VMEM((2, d // FFN_W_CHUNKS, ff), F32), pltpu.VMEM((2, ff // FFN_W_CHUNKS, d), F32),
                            pltpu.SemaphoreType.DMA((2,)), pltpu.SemaphoreType.DMA((2,)), pltpu.SemaphoreType.DMA((2,))]),
        out_shape=SDS((MOE_TOP_K * n_tok, SUBLANES, dw), F32), compiler_params=_cp("arbitrary"), name="moe_ffn",
    )(block_e, n_used_blocks, slot_tok, slot_dst, block_cnt, h, w_gate, w_up, w_down, slot_w.reshape(n_slots, 1))


def _moe_combine_kernel(x_ref, y0_ref, y1_ref, g_ref, o_ref):
    dw = y0_ref.shape[-1]
    for j in range(SUBLANES):
        cols = slice(j * dw, (j + 1) * dw)
        o_ref[0, :, cols] = x_ref[0, :, cols] + g_ref[0, :, cols] * (y0_ref[0, 0, :, j, :] + y1_ref[0, 0, :, j, :])


def _moe_combine(x, y2, gate, tm=256):
    bsz, seq, d = x.shape
    dw = d // SUBLANES
    y5 = y2.reshape(MOE_TOP_K, bsz, seq, SUBLANES, dw)
    tok = pl.BlockSpec((1, tm, d), lambda b, i: (b, i, 0))
    slab = lambda k: pl.BlockSpec((1, 1, tm, SUBLANES, dw), lambda b, i: (k, b, i, 0, 0))
    return pl.pallas_call(
        _moe_combine_kernel, grid=(bsz, seq // tm),
        in_specs=[tok, slab(0), slab(1), pl.BlockSpec((1, 1, d), lambda b, i: (b, 0, 0))],
        out_specs=tok, out_shape=SDS((bsz, seq, d), F32),
        compiler_params=_cp("parallel", "parallel"), name="moe_combine",
    )(x, y5, y5, gate.reshape(bsz, 1, d))


def _pack_plan(offs, n_cols):
    ng, wh = len(ATT_GROUPS), ATT_HEADS * ATT_HEAD
    plan, starts = [], []

    def dense(first, count):
        starts.append(len(plan))
        for c0 in range(0, count, PACK_COLS):
            plan.append([(first + c0, min(PACK_COLS, count - c0), 0)])

    dense(offs[0], A_MAIN)
    o_w = offs[0] + A_MAIN
    o_a = o_w + RWKV_DECAY_RANK
    o_g = o_a + RWKV_ICLR_RANK
    plan.append([(o_w, RWKV_DECAY_RANK, 0), (o_a, RWKV_ICLR_RANK, 128), (o_g, RWKV_GATE_RANK, 256)])
    for gi in range(ng):
        starts.append(len(plan))
        for t in range(3):
            for c0 in range(0, wh, PACK_COLS):
                plan.append([(offs[1] + (t * ng + gi) * wh + c0, PACK_COLS, 0)])
    dense(offs[2], offs[3] - offs[2])
    dense(offs[3], offs[4] - offs[3])
    dense(offs[4], n_cols - offs[4])
    return plan, starts


def _pack_w_in_kernel(layer, plan, w_hbm, o_ref, slab, sem):
    step = pl.program_id(0)

    def copies(j):
        return [pltpu.make_async_copy(w_hbm.at[layer, pl.ds(src, n)], slab.at[j % 2, pl.ds(dst, n)], sem.at[j % 2])
                for src, n, dst in plan[j]]

    def fetch(j):
        covered = sorted((dst, dst + n) for _, n, dst in plan[j])
        edge = 0
        for lo, hi in covered + [(PACK_COLS, PACK_COLS)]:
            if lo > edge:
                slab[j % 2, edge:lo, :] = jnp.zeros((lo - edge, slab.shape[2]), F32)
            edge = hi
        for cp in copies(j):
            cp.start()

    for j in range(len(plan)):
        @pl.when(step == j)
        def _():
            if j == 0:
                fetch(0)
            if j + 1 < len(plan):
                fetch(j + 1)
            for cp in copies(j):
                cp.wait()

    o_ref[...] = slab[step % 2].T.astype(BF16)


def _pack_w_in(w, layer, offs):
    _, d, n = w.shape
    plan, starts = _pack_plan(offs, n)
    packed = pl.pallas_call(
        functools.partial(_pack_w_in_kernel, layer, plan), grid=(len(plan),),
        in_specs=[pl.BlockSpec(memory_space=pl.ANY)],
        out_specs=pl.BlockSpec((d, PACK_COLS), lambda j: (0, j)),
        out_shape=SDS((d, len(plan) * PACK_COLS), BF16),
        scratch_shapes=[pltpu.VMEM((2, PACK_COLS, d), F32), pltpu.SemaphoreType.DMA((2,))],
        compiler_params=_cp("arbitrary"), name="pack_w_in",
    )(jnp.swapaxes(w, 1, 2))
    return packed, starts + [len(plan)]


def kernel(x, c, ada_w, ada_b, ada_table, norm1_g, norm2_g, w_in, rwkv_mu, rwkv_w0, rwkv_w2, rwkv_a0, rwkv_a2,
           rwkv_g2, rwkv_kk, rwkv_ka, rwkv_rk, rwkv_lnx_w, rwkv_lnx_b, pool_w, pool_scale, ssm_conv_w, ssm_conv_b,
           ssm_dt_bias, ssm_a_log, ssm_d, ssm_norm_w, gate_up, gate_b, w_branch, w_out, router_group_w,
           router_group_b, router_expert_w, router_expert_b, exp_w_gate, exp_w_up, exp_w_down, final_g):
    bsz, seq, d = x.shape
    m = bsz * seq
    a_cols = rwkv_mu.shape[1]
    b_cols = 3 * len(ATT_GROUPS) * ATT_HEADS * ATT_HEAD
    d_cols = SSM_INNER + SSM_CONV_DIM + SSM_HEADS
    offs = (0, a_cols, a_cols + b_cols, a_cols + b_cols + BRANCH_WIDTH, a_cols + b_cols + BRANCH_WIDTH + d_cols)
    cond = _cond(c, ada_w, ada_b)
    for l in range(DEPTH):
        shift1, scale1, gate1, shift2, scale2, gate2 = jnp.split(cond + ada_table[l], 6, axis=-1)
        dilations = tuple(dil for _, dil in ATT_GROUPS)
        h, *h_phases = _norm_mod_phases(x, norm1_g[l], scale1, shift1, [dil for dil in dilations if dil > 1])
        h = h.reshape(m, d)
        h_of = {1: h, **{dil: hp.reshape(m, d) for dil, hp in zip([dil for dil in dilations if dil > 1], h_phases)}}
        w_packed, blk = _pack_w_in(w_in, l, offs)
        proj = lambda a, seg, dtype: _matmul(a, w_packed, dtype, blk[seg], blk[seg + 1] - blk[seg], tn=PACK_COLS)
        pa = proj(h, 0, F32).reshape(bsz, seq, A_PAD)
        pbs = [proj(h_of[dil], 1 + gi, BF16).reshape(bsz, dil, seq // dil, -1) for gi, dil in enumerate(dilations)]
        pc = proj(h, 4, F32).reshape(bsz, seq, BRANCH_WIDTH)
        pd = proj(h, 5, F32).reshape(bsz, seq, D_PAD)
        pg = proj(h, 6, BF16)
        ya, ga = _rwkv_mixer(pa, rwkv_mu[l], rwkv_w0[l], rwkv_w2[l], rwkv_a0[l], rwkv_a2[l], rwkv_g2[l],
                             rwkv_kk[l], rwkv_ka[l], rwkv_rk[l], rwkv_lnx_w[l], rwkv_lnx_b[l])
        ob = _attention_mixer(pbs)
        oc = _pool_mixer(pc, pool_w[l], pool_scale[l])
        od = _mamba_mixer(pd, ssm_conv_w[l], ssm_conv_b[l], ssm_dt_bias[l], ssm_a_log[l], ssm_d[l], ssm_norm_w[l])
        merged = _merge(pg, ya.reshape(m, BRANCH_WIDTH), ga.reshape(m, BRANCH_WIDTH), ob, oc, od,
                        _cast_bf16(gate_up, l), gate_b[l], _cast_bf16(w_branch, l))
        w_out_l = _cast_bf16(w_out.reshape(DEPTH, 1, d, d), l).reshape(d, d)
        x = _matmul_resid(merged, w_out_l, x.reshape(m, d), gate1, seq).reshape(bsz, seq, d)
        h2, expert, weight, counts = _norm_route(x, norm2_g[l], scale2, shift2, router_group_w[l], router_group_b[l],
                                         router_expert_w[l], router_expert_b[l])
        y2 = _moe_apply(h2, expert, weight, counts, exp_w_gate, exp_w_up, exp_w_down, l)
        x = _moe_combine(x, y2, gate2)
    zeros = jnp.zeros((bsz, d), F32)
    return _norm_mod(x, final_g, zeros, zeros, F32)
```

```python
import functools

import jax
import jax.numpy as jnp
from jax import lax
from jax.experimental import pallas as pl
from jax.experimental.pallas import tpu as pltpu

F32 = jnp.float32
BF16 = jnp.bfloat16
HI = lax.Precision.HIGHEST
SDS = jax.ShapeDtypeStruct

DEPTH = 2
NORM_EPS = 1e-6
BRANCH_WIDTH = 1024
RWKV_HEAD = 64
RWKV_HEADS = 16
RWKV_DECAY_RANK = 64
RWKV_ICLR_RANK = 64
RWKV_GATE_RANK = 160
RWKV_GN_EPS = 64e-5
RWKV_CHUNK = 64
ATT_HEAD = 128
ATT_GROUPS = ((128, 1), (512, 4), (2048, 16))
ATT_HEADS = 8
ATT_BLOCK = 128
POOL_WINDOWS = (2, 4, 8, 16)
POOL_GROUP = 256
POOL_HALO = 16
SSM_INNER = 1024
SSM_HEAD = 64
SSM_HEADS = 16
SSM_GROUPS = 4
SSM_STATE = 128
SSM_CONV = 4
SSM_CHUNK = 128
SSM_CONV_DIM = SSM_INNER + 2 * SSM_GROUPS * SSM_STATE
GATE_RANK = 256
MOE_GROUPS = 4
MOE_EXPERTS_PER_GROUP = 8
MOE_EXPERTS = 32
MOE_TOP_K = 2
MOE_FF = 512
MOE_ROWS = 256
LANES = 128
NEG = -1e30
A_MAIN = 3 * BRANCH_WIDTH
A_PAD = A_MAIN + 128 + 128 + 256
PACK_COLS = 512
D_PAD = -(-(SSM_INNER + SSM_CONV_DIM + SSM_HEADS) // PACK_COLS) * PACK_COLS
VMEM_LIMIT = 56 * 1024 * 1024


def _cp(*sem):
    return pltpu.CompilerParams(dimension_semantics=sem, vmem_limit_bytes=VMEM_LIMIT)


def _sigmoid(x):
    return 1.0 / (1.0 + jnp.exp(-x))


def _silu(x):
    return x * _sigmoid(x)


def _softplus(x):
    return jnp.maximum(x, 0.0) + jnp.log(1.0 + jnp.exp(-jnp.abs(x)))


def _dot(a, b):
    return jnp.dot(a, b, preferred_element_type=F32)


def _dot_hi(a, b):
    return jnp.dot(a, b, precision=HI, preferred_element_type=F32)


def _cond_kernel(c_ref, w_ref, b_ref, o_ref, a_ref):
    @pl.when(pl.program_id(0) == 0)
    def _():
        a_ref[...] = _silu(c_ref[...])

    k, tn = w_ref.shape
    w = w_ref[...]
    rows = []
    for r in range(a_ref.shape[0]):
        prod = w * jnp.concatenate([a_ref[r]] * (tn // LANES), axis=1)
        part = jnp.sum(prod.reshape(k // 8, 8, tn), axis=0)
        rows.append(jnp.sum(part, axis=0, keepdims=True))
    rows.append(jnp.zeros((o_ref.shape[0] - len(rows), tn), F32))
    o_ref[...] = jnp.concatenate(rows, axis=0) + b_ref[...]


def _cond(c, ada_w, ada_b):
    bsz, d = c.shape
    n = ada_w.shape[1]
    tn = 512
    c_lanes = jnp.broadcast_to(c[:, :, None], (bsz, d, LANES))
    out = pl.pallas_call(
        _cond_kernel, grid=(n // tn,),
        in_specs=[pl.BlockSpec((bsz, d, LANES), lambda j: (0, 0, 0)),
                  pl.BlockSpec((d, tn), lambda j: (0, j)),
                  pl.BlockSpec((1, tn), lambda j: (0, j))],
        out_specs=pl.BlockSpec((8, tn), lambda j: (0, j)),
        out_shape=SDS((8, n), F32), scratch_shapes=[pltpu.VMEM((bsz, d, LANES), F32)],
        compiler_params=_cp("arbitrary"), name="cond",
    )(c_lanes, ada_w, ada_b.reshape(1, n))
    return out[:bsz]


def _norm_mod_kernel(x_ref, g_ref, sc_ref, sh_ref, o_ref):
    x = x_ref[0]
    ms = jnp.mean(x * x, axis=-1, keepdims=True)
    y = x * lax.rsqrt(ms + NORM_EPS) * g_ref[...]
    o_ref[0] = (y * (1.0 + sc_ref[0]) + sh_ref[0]).astype(o_ref.dtype)


def _norm_mod(x, g, scale, shift, out_dtype, tm=256):
    bsz, seq, d = x.shape
    return pl.pallas_call(
        _norm_mod_kernel, grid=(bsz, seq // tm),
        in_specs=[pl.BlockSpec((1, tm, d), lambda b, i: (b, i, 0)),
                  pl.BlockSpec((1, d), lambda b, i: (0, 0)),
                  pl.BlockSpec((1, 1, d), lambda b, i: (b, 0, 0)),
                  pl.BlockSpec((1, 1, d), lambda b, i: (b, 0, 0))],
        out_specs=pl.BlockSpec((1, tm, d), lambda b, i: (b, i, 0)),
        out_shape=SDS((bsz, seq, d), out_dtype), compiler_params=_cp("parallel", "parallel"), name="norm_mod",
    )(x, g.reshape(1, d), scale.reshape(bsz, 1, d), shift.reshape(bsz, 1, d))


def _norm_mod_phases_kernel(x_ref, g_ref, sc_ref, sh_ref, o_ref, *refs):
    phase_refs, scr = refs[:-1], refs[-1]
    x = x_ref[0]
    tm, d = x.shape
    ms = jnp.mean(x * x, axis=-1, keepdims=True)
    y = x * lax.rsqrt(ms + NORM_EPS) * g_ref[...] * (1.0 + sc_ref[0]) + sh_ref[0]
    o_ref[0] = y.astype(o_ref.dtype)
    for ref in phase_refs:
        dil = ref.shape[1]
        n = tm // dil
        y3 = y.reshape(n, dil, d)
        for ph in range(dil):
            scr[ph * n:(ph + 1) * n, :] = y3[:, ph, :]
        for ph in range(dil):
            ref[0, ph] = scr[ph * n:(ph + 1) * n, :].astype(ref.dtype)


def _norm_mod_phases(x, g, scale, shift, dilations, tm=256):
    bsz, seq, d = x.shape
    tok = pl.BlockSpec((1, tm, d), lambda b, i: (b, i, 0))
    vec = pl.BlockSpec((1, 1, d), lambda b, i: (b, 0, 0))
    return pl.pallas_call(
        _norm_mod_phases_kernel, grid=(bsz, seq // tm),
        in_specs=[tok, pl.BlockSpec((1, d), lambda b, i: (0, 0)), vec, vec],
        out_specs=[tok] + [pl.BlockSpec((1, dil, tm // dil, d), lambda b, i: (b, 0, i, 0)) for dil in dilations],
        out_shape=[SDS((bsz, seq, d), BF16)] + [SDS((bsz, dil, seq // dil, d), BF16) for dil in dilations],
        scratch_shapes=[pltpu.VMEM((tm, d), F32)],
        compiler_params=_cp("parallel", "parallel"), name="norm_mod_phases",
    )(x, g.reshape(1, d), scale.reshape(bsz, 1, d), shift.reshape(bsz, 1, d))


def _mm_kernel(a_ref, b_ref, o_ref):
    o_ref[...] = _dot(a_ref[...], b_ref[...]).astype(o_ref.dtype)


def _matmul(a, b, out_dtype, first_block=0, n_blocks=None, tm=1024, tn=512):
    m, k = a.shape
    n = b.shape[1] if n_blocks is None else n_blocks * tn
    return pl.pallas_call(
        _mm_kernel, grid=(m // tm, n // tn),
        in_specs=[pl.BlockSpec((tm, k), lambda i, j: (i, 0)),
                  pl.BlockSpec((k, tn), lambda i, j: (0, first_block + j))],
        out_specs=pl.BlockSpec((tm, tn), lambda i, j: (i, j)),
        out_shape=SDS((m, n), out_dtype), compiler_params=_cp("parallel", "parallel"), name="matmul",
    )(a, b)


def _mm_resid_kernel(a_ref, b_ref, r_ref, g_ref, o_ref):
    o_ref[...] = r_ref[...] + g_ref[0] * _dot(a_ref[...], b_ref[...])


def _matmul_resid(a, b, resid, gate, seq, tm=1024, tn=512):
    m, k = a.shape
    n = b.shape[1]
    bsz = gate.shape[0]
    return pl.pallas_call(
        _mm_resid_kernel, grid=(m // tm, n // tn),
        in_specs=[pl.BlockSpec((tm, k), lambda i, j: (i, 0)),
                  pl.BlockSpec((k, tn), lambda i, j: (0, j)),
                  pl.BlockSpec((tm, tn), lambda i, j: (i, j)),
                  pl.BlockSpec((1, 1, tn), lambda i, j: (i * tm // seq, 0, j))],
        out_specs=pl.BlockSpec((tm, tn), lambda i, j: (i, j)),
        out_shape=SDS((m, n), F32), compiler_params=_cp("parallel", "parallel"), name="matmul_resid",
    )(a, b, resid, gate.reshape(bsz, 1, n))


def _rwkv_prep_kernel(p_ref, halo_ref, mu_ref, w0_ref, a0_ref, kkw_ref, kaw_ref, w2_ref, a2_ref, g2_ref,
                      r_ref, k_ref, v_ref, kk_ref, a_ref, lw_ref, g_ref):
    x = p_ref[0]
    w = BRANCH_WIDTH
    prev_first = jnp.where(pl.program_id(1) > 0, halo_ref[0][7:8, :], 0.0)
    row = lax.broadcasted_iota(jnp.int32, x.shape, 0)
    prev = jnp.where(row == 0, prev_first, pltpu.roll(x, 1, axis=0))
    p = x + (prev - x) * mu_ref[...]
    r, k, v = p[:, 0:w], p[:, w:2 * w], p[:, 2 * w:3 * w]
    xw, xa, xg = p[:, A_MAIN:A_MAIN + 128], p[:, A_MAIN + 128:A_MAIN + 256], p[:, A_MAIN + 256:A_PAD]
    wlog = -_softplus(-(w0_ref[...] + _dot_hi(jnp.tanh(xw), w2_ref[...]))) - 0.5
    a = _sigmoid(a0_ref[...] + _dot_hi(xa, a2_ref[...]))
    r_ref[0] = r
    k_ref[0] = k * (1.0 + (a - 1.0) * kaw_ref[...])
    v_ref[0] = v
    kk_ref[0] = k * kkw_ref[...]
    a_ref[0] = a
    lw_ref[0] = -jnp.exp(wlog)
    g_ref[0] = _dot_hi(_sigmoid(xg), g2_ref[...])


def _rwkv_prep(pa, mu, w0, a0, kkw, kaw, w2, a2, g2, tm=256):
    bsz, seq, ap = pa.shape
    w = BRANCH_WIDTH
    row = lambda t: t.reshape(1, -1)
    full = lambda shape: pl.BlockSpec(shape, lambda b, i: (0, 0))
    out_spec = pl.BlockSpec((1, tm, w), lambda b, i: (b, i, 0))
    return pl.pallas_call(
        _rwkv_prep_kernel, grid=(bsz, seq // tm),
        in_specs=[pl.BlockSpec((1, tm, ap), lambda b, i: (b, i, 0)),
                  pl.BlockSpec((1, 8, ap), lambda b, i: (b, jnp.maximum(i * (tm // 8) - 1, 0), 0)),
                  full((1, ap)), full((1, w)), full((1, w)), full((1, w)), full((1, w)),
                  full((128, w)), full((128, w)), full((256, w))],
        out_specs=[out_spec] * 7,
        out_shape=[SDS((bsz, seq, w), F32)] * 7,
        compiler_params=_cp("parallel", "parallel"), name="rwkv_prep",
    )(pa, pa, row(mu), row(w0), row(a0), row(kkw), row(kaw), w2, a2, g2)


def _rwkv_rec_kernel(r_ref, k_ref, v_ref, kk_ref, a_ref, lw_ref, rk_ref, lnw_ref, lnb_ref, o_ref, s_ref):
    @pl.when(pl.program_id(1) == 0)
    def _():
        s_ref[...] = jnp.zeros_like(s_ref)

    nh, n = s_ref.shape[0], s_ref.shape[1]
    heads = lambda ref: jnp.stack([ref[0, :, h * n:(h + 1) * n] for h in range(nh)])
    r, k, v, kk, a, lw = (heads(ref) for ref in (r_ref, k_ref, v_ref, kk_ref, a_ref, lw_ref))
    c = r.shape[1]
    kn = kk / jnp.maximum(jnp.sqrt(jnp.sum(kk * kk, axis=-1, keepdims=True)), 1e-12)
    b = kn * a
    row = lax.broadcasted_iota(jnp.int32, (c, c), 0)
    col = lax.broadcasted_iota(jnp.int32, (c, c), 1)
    strict, incl = (row > col)[None], (row >= col)[None]
    bdot = lambda spec: (lambda x, y: jnp.einsum(spec, x.astype(BF16), y.astype(BF16), preferred_element_type=F32))
    nt, nn, tn = bdot('hik,hjk->hij'), bdot('hij,hjk->hik'), bdot('hiv,hik->hvk')
    tril = jnp.broadcast_to((row >= col).astype(BF16)[None], (nh, c, c))
    lw_hi = lw.astype(BF16)
    lw_r1 = lw - lw_hi.astype(F32)
    lw_mid = lw_r1.astype(BF16)
    lw_lo = (lw_r1 - lw_mid.astype(F32)).astype(BF16)
    lc = nn(tril, lw_hi) + (nn(tril, lw_mid) + nn(tril, lw_lo))
    lc_last = lc[:, c - 1:c, :]
    e_neg, e_end = jnp.exp(-lc), jnp.exp(lc_last - lc)
    kt, rt = kn * jnp.exp(lc - lw), r * jnp.exp(lc)
    bt, kd = b * e_neg, k * e_neg
    kr = jnp.concatenate([kt, rt], axis=1)
    sc = nt(kr, jnp.concatenate([bt, kd], axis=1))
    lm = jnp.where(strict, -sc[:, :c, :c], 0.0)
    ak = jnp.where(strict, sc[:, :c, c:], 0.0)
    bb = jnp.where(incl, sc[:, c:, :c], 0.0)
    bk = jnp.where(incl, sc[:, c:, c:], 0.0)
    s0 = s_ref[...]
    kr_s0 = nt(kr, s0)
    abk_v = nn(jnp.concatenate([ak, bk], axis=1), v)
    u = -(kr_s0[:, :c] + abk_v[:, :c])
    pw = lm
    n_doublings = c.bit_length() - 1
    for it in range(n_doublings):
        u = u + nn(pw, u)
        if it + 1 < n_doublings:
            pw = nn(pw, pw)
    y = kr_s0[:, c:] + nn(bb, u) + abk_v[:, c:]
    s_ref[...] = s0 * jnp.exp(lc_last) + tn(jnp.concatenate([u, v], axis=1),
                                            jnp.concatenate([b * e_end, k * e_end], axis=1))
    mean = jnp.mean(y, axis=-1, keepdims=True)
    var = jnp.mean(jnp.square(y - mean), axis=-1, keepdims=True)
    out = (y - mean) * lax.rsqrt(var + RWKV_GN_EPS) * lnw_ref[...] + lnb_ref[...]
    out = out + jnp.sum(r * k * rk_ref[...], axis=-1, keepdims=True) * v
    for h in range(nh):
        o_ref[0, :, h * n:(h + 1) * n] = out[h]


def _rwkv_rec(r, k, v, kk, a, lw, rk, lnw, lnb, chunk=RWKV_CHUNK):
    bsz, seq, w = r.shape
    nh, n = rk.shape
    blk = pl.BlockSpec((1, chunk, w), lambda b, i: (b, i, 0))
    par = pl.BlockSpec((nh, 1, n), lambda b, i: (0, 0, 0))
    return pl.pallas_call(
        _rwkv_rec_kernel, grid=(bsz, seq // chunk),
        in_specs=[blk] * 6 + [par] * 3, out_specs=blk,
        out_shape=SDS((bsz, seq, w), F32),
        scratch_shapes=[pltpu.VMEM((nh, n, n), F32)],
        compiler_params=_cp("parallel", "arbitrary"), name="rwkv_rec",
    )(r, k, v, kk, a, lw, rk.reshape(nh, 1, n), lnw.reshape(nh, 1, n), lnb.reshape(nh, 1, n))


def _pad_rows(w, rows):
    return jnp.zeros((rows, w.shape[1]), w.dtype).at[:w.shape[0]].set(w)


def _rwkv_pack_cols(t):
    lead = t.shape[:-1]
    z = lambda n: jnp.zeros(lead + (n,), t.dtype)
    o1 = A_MAIN + RWKV_DECAY_RANK
    o2 = o1 + RWKV_ICLR_RANK
    return jnp.concatenate([t[..., :A_MAIN], t[..., A_MAIN:o1], z(128 - RWKV_DECAY_RANK), t[..., o1:o2],
                            z(128 - RWKV_ICLR_RANK), t[..., o2:], z(256 - RWKV_GATE_RANK)], axis=-1)


def _rwkv_mixer(pa, mu, w0, w2, a0, a2, g2, kkw, kaw, rk, lnw, lnb):
    bsz, seq, _ = pa.shape
    r, k, v, kk, a, lw, g = _rwkv_prep(pa, _rwkv_pack_cols(mu), w0, a0, kkw, kaw,
                                       _pad_rows(w2, 128), _pad_rows(a2, 128), _pad_rows(g2, 256))
    y = _rwkv_rec(r, k, v, kk, a, lw, rk, lnw.reshape(RWKV_HEADS, RWKV_HEAD), lnb.reshape(RWKV_HEADS, RWKV_HEAD))
    return y, g


def _att_kernel(q_ref, kc_ref, kp_ref, vc_ref, vp_ref, o_ref, lse_ref):
    nt = lambda x, y: lax.dot_general(x, y, (((1,), (1,)), ((), ())), preferred_element_type=F32)
    scale = ATT_HEAD ** -0.5
    qi = lax.broadcasted_iota(jnp.int32, (ATT_BLOCK, ATT_BLOCK), 0)
    ki = lax.broadcasted_iota(jnp.int32, (ATT_BLOCK, ATT_BLOCK), 1)
    cur_ok = ki <= qi
    prev_ok = (ki >= qi) & (pl.program_id(2) > 0)
    lane = lax.broadcasted_iota(jnp.int32, (ATT_BLOCK, LANES), 1)
    lse_all = jnp.zeros((ATT_BLOCK, LANES), F32)
    for h in range(ATT_HEADS):
        cols = slice(h * ATT_HEAD, (h + 1) * ATT_HEAD)
        q = q_ref[0, 0, :, cols]
        s_c = jnp.where(cur_ok, nt(q, kc_ref[0, 0, :, cols]) * scale, NEG)
        s_p = jnp.where(prev_ok, nt(q, kp_ref[0, 0, :, cols]) * scale, NEG)
        m = jnp.maximum(jnp.max(s_c, axis=-1, keepdims=True), jnp.max(s_p, axis=-1, keepdims=True))
        e_c, e_p = jnp.exp(s_c - m), jnp.exp(s_p - m)
        den = jnp.sum(e_c, axis=-1, keepdims=True) + jnp.sum(e_p, axis=-1, keepdims=True)
        o = _dot(e_c.astype(BF16), vc_ref[0, 0, :, cols]) + _dot(e_p.astype(BF16), vp_ref[0, 0, :, cols])
        o_ref[0, 0, :, cols] = o / den
        lse_all = jnp.where(lane == h, m + jnp.log(den), lse_all)
    lse_ref[0, 0] = lse_all


def _att_group(pb, gi):
    bsz, dilation, n_phase, _ = pb.shape
    w = ATT_HEADS * ATT_HEAD
    nblk = n_phase // ATT_BLOCK
    blk = (1, 1, ATT_BLOCK, w)
    cur = lambda t: pl.BlockSpec(blk, lambda b, ph, i: (b, ph, i, t))
    prev = lambda t: pl.BlockSpec(blk, lambda b, ph, i: (b, ph, jnp.maximum(i - 1, 0), t))
    return pl.pallas_call(
        _att_kernel, grid=(bsz, dilation, nblk),
        in_specs=[cur(0), cur(1), prev(1), cur(2), prev(2)],
        out_specs=[pl.BlockSpec(blk, lambda b, ph, i: (b, ph, i, 0)),
                   pl.BlockSpec((1, 1, ATT_BLOCK, LANES), lambda b, ph, i: (b, ph, i, 0))],
        out_shape=[SDS((bsz, dilation, n_phase, w), F32), SDS((bsz, dilation, n_phase, LANES), F32)],
        compiler_params=_cp("parallel", "parallel", "arbitrary"), name=f"att_g{gi}",
    )(pb, pb, pb, pb, pb)


def _att_combine_kernel(o0, o1, o2, l0, l1, l2, e_ref, out_ref):
    tm = out_ref.shape[0]

    def tokens(ref):
        d = ref.shape[1]
        if d == 1:
            return ref[0, 0]
        return jnp.stack([ref[0, ph] for ph in range(d)], axis=1).reshape(tm, ref.shape[-1])

    lses = [tokens(l) for l in (l0, l1, l2)]
    m = jnp.maximum(jnp.maximum(lses[0], lses[1]), lses[2])
    es = [jnp.exp(l - m) for l in lses]
    inv = 1.0 / (es[0] + es[1] + es[2])
    acc = None
    for e, o in zip(es, (o0, o1, o2)):
        term = _dot_hi(e * inv, e_ref[...]) * tokens(o)
        acc = term if acc is None else acc + term
    out_ref[...] = acc.astype(out_ref.dtype)


def _attention_mixer(pbs, tm=256):
    res = [_att_group(pb, gi) for gi, pb in enumerate(pbs)]
    bsz, _, seq, w = res[0][0].shape
    nt = seq // tm
    spec = lambda t: pl.BlockSpec((1, t.shape[1], tm // t.shape[1], t.shape[3]), lambda b, i: (b, 0, i, 0))
    expand = (jnp.arange(LANES, dtype=jnp.int32)[:, None] == jnp.arange(w, dtype=jnp.int32)[None, :] // ATT_HEAD).astype(F32)
    args = [t[0] for t in res] + [t[1] for t in res]
    return pl.pallas_call(
        _att_combine_kernel, grid=(bsz, nt),
        in_specs=[spec(t) for t in args] + [pl.BlockSpec((LANES, w), lambda b, i: (0, 0))],
        out_specs=pl.BlockSpec((tm, w), lambda b, i: (b * nt + i, 0)),
        out_shape=SDS((bsz * seq, w), BF16), compiler_params=_cp("parallel", "parallel"), name="att_combine",
    )(*args, expand)


def _pool_kernel(x_ref, halo_ref, w_ref, sc_ref, o_ref, ext_ref):
    tm = x_ref.shape[1]
    i = pl.program_id(1)
    x = x_ref[0]
    ext_ref[0:POOL_HALO, :] = jnp.where(i > 0, halo_ref[0], 0.0)
    ext_ref[POOL_HALO:, :] = x
    pos = i * tm + lax.broadcasted_iota(jnp.int32, (tm, POOL_GROUP), 0)
    outs = []
    for gi, win in enumerate(POOL_WINDOWS):
        cols = slice(gi * POOL_GROUP, (gi + 1) * POOL_GROUP)
        xg = x[:, cols]
        s = xg
        for j in range(1, win):
            s = s + ext_ref[pl.ds(POOL_HALO - j, tm), cols]
        mixed = s / jnp.minimum(pos + 1, win).astype(F32) - xg
        outs.append(_dot(mixed.astype(BF16), w_ref[gi]))
    o_ref[0] = (jnp.concatenate(outs, axis=-1) * sc_ref[...]).astype(o_ref.dtype)


def _pool_mixer(pc, w_pool, scale, tm=256):
    bsz, seq, w = pc.shape
    out = pl.pallas_call(
        _pool_kernel, grid=(bsz, seq // tm),
        in_specs=[pl.BlockSpec((1, tm, w), lambda b, i: (b, i, 0)),
                  pl.BlockSpec((1, POOL_HALO, w), lambda b, i: (b, jnp.maximum(i * (tm // POOL_HALO) - 1, 0), 0)),
                  pl.BlockSpec(w_pool.shape, lambda b, i: (0, 0, 0)),
                  pl.BlockSpec((1, w), lambda b, i: (0, 0))],
        out_specs=pl.BlockSpec((1, tm, w), lambda b, i: (b, i, 0)),
        out_shape=SDS((bsz, seq, w), BF16),
        scratch_shapes=[pltpu.VMEM((tm + POOL_HALO, w), F32)],
        compiler_params=_cp("parallel", "parallel"), name="pool",
    )(pc, pc, w_pool.astype(BF16), scale.reshape(1, w))
    return out.reshape(bsz * seq, w)


SSM_HALO = 8


def _ssd_kernel(p_ref, halo_ref, cw_ref, cb_ref, dtb_ref, ah_ref, dsk_ref, nw_ref, e64_ref, e128_ref,
                o_ref, ext_ref, h_ref):
    ci = pl.program_id(1)
    q = SSM_CHUNK
    inner = SSM_INNER
    xbc_lo, xbc_hi = inner, inner + SSM_CONV_DIM

    @pl.when(ci == 0)
    def _():
        h_ref[...] = jnp.zeros_like(h_ref)

    z = p_ref[0, :, 0:inner]
    ext_ref[0:SSM_HALO, :] = jnp.where(ci > 0, halo_ref[0, :, xbc_lo:xbc_hi], 0.0)
    ext_ref[SSM_HALO:, :] = p_ref[0, :, xbc_lo:xbc_hi]
    conv = cb_ref[...]
    for j in range(SSM_CONV):
        conv = conv + cw_ref[j:j + 1, :] * ext_ref[pl.ds(SSM_HALO - (SSM_CONV - 1) + j, q), :]
    xbc = _silu(conv)
    xs = xbc[:, 0:inner]
    dt = _softplus(p_ref[0, :, xbc_hi:xbc_hi + LANES] + dtb_ref[...])
    a = dt * ah_ref[...]
    row = lax.broadcasted_iota(jnp.int32, (q, q), 0)
    col = lax.broadcasted_iota(jnp.int32, (q, q), 1)
    causal = row >= col
    a_cum = _dot_hi(causal.astype(F32), a)
    a_cum_t = a_cum.T
    dt_full = _dot_hi(dt, e64_ref[...])
    acum_full = _dot_hi(a_cum, e64_ref[...])
    alast_full = acum_full[q - 1:q, :]
    acum_b = _dot_hi(a_cum, e128_ref[...])
    xdt = xs * dt_full
    x_to_end = xdt * jnp.exp(alast_full - acum_full)
    exp_ac = jnp.exp(acum_full)
    chunk_dec = jnp.exp(alast_full)
    lane = lax.broadcasted_iota(jnp.int32, (q, LANES), 1)
    first_head = lane < SSM_HEAD
    ys = []
    for g in range(SSM_GROUPS):
        bm = xbc[:, inner + g * SSM_STATE:inner + (g + 1) * SSM_STATE]
        cm = xbc[:, inner + (SSM_GROUPS + g) * SSM_STATE:inner + (SSM_GROUPS + g + 1) * SSM_STATE].astype(BF16)
        bt = bm.T.astype(BF16)
        cb = _dot(cm, bt)
        for pr in range(SSM_HEADS // SSM_GROUPS // 2):
            pi = g * (SSM_HEADS // SSM_GROUPS // 2) + pr
            cols = slice(pi * LANES, (pi + 1) * LANES)
            mats = []
            for hd in (2 * pi, 2 * pi + 1):
                seg = acum_b[:, hd * LANES:(hd + 1) * LANES] - a_cum_t[hd:hd + 1, :]
                mats.append((cb * jnp.exp(jnp.where(causal, seg, NEG))).astype(BF16))
            xp = xdt[:, cols]
            x_blockdiag = jnp.concatenate([jnp.where(first_head, xp, 0.0), jnp.where(first_head, 0.0, xp)], axis=0)
            y_diag = _dot(jnp.concatenate(mats, axis=1), x_blockdiag.astype(BF16))
            h_t = h_ref[pi]
            y_off = _dot(cm, h_t.astype(BF16)) * exp_ac[:, cols]
            h_ref[pi] = h_t * chunk_dec[:, cols] + _dot(bt, x_to_end[:, cols].astype(BF16))
            ys.append(y_diag + y_off)
    y = (jnp.concatenate(ys, axis=-1) + xs * dsk_ref[...]) * _silu(z)
    gsize = inner // SSM_GROUPS
    outs = []
    for g in range(SSM_GROUPS):
        yg = y[:, g * gsize:(g + 1) * gsize]
        outs.append(yg * lax.rsqrt(jnp.mean(yg * yg, axis=-1, keepdims=True) + NORM_EPS))
    o_ref[0] = (jnp.concatenate(outs, axis=-1) * nw_ref[...]).astype(o_ref.dtype)


def _mamba_mixer(pd, conv_w, conv_b, dt_bias, a_log, d_skip, norm_w):
    bsz, seq, dp = pd.shape
    q = SSM_CHUNK
    pad_heads = lambda t: jnp.zeros((1, LANES), F32).at[0, :SSM_HEADS].set(t)
    head_of = lambda width: jnp.arange(SSM_HEADS * width, dtype=jnp.int32)[None, :] // width
    expand = lambda width: (jnp.arange(LANES, dtype=jnp.int32)[:, None] == head_of(width)).astype(F32)
    full2 = lambda shape: pl.BlockSpec(shape, lambda b, i: (0, 0))
    out = pl.pallas_call(
        _ssd_kernel, grid=(bsz, seq // q),
        in_specs=[pl.BlockSpec((1, q, dp), lambda b, i: (b, i, 0)),
                  pl.BlockSpec((1, SSM_HALO, dp), lambda b, i: (b, jnp.maximum(i * (q // SSM_HALO) - 1, 0), 0)),
                  full2((SSM_CONV, SSM_CONV_DIM)), full2((1, SSM_CONV_DIM)), full2((1, LANES)), full2((1, LANES)),
                  full2((1, SSM_INNER)), full2((1, SSM_INNER)),
                  full2((LANES, SSM_HEADS * SSM_HEAD)), full2((LANES, SSM_HEADS * LANES))],
        out_specs=pl.BlockSpec((1, q, SSM_INNER), lambda b, i: (b, i, 0)),
        out_shape=SDS((bsz, seq, SSM_INNER), BF16),
        scratch_shapes=[pltpu.VMEM((q + SSM_HALO, SSM_CONV_DIM), F32),
                        pltpu.VMEM((SSM_HEADS // 2, SSM_STATE, LANES), F32)],
        compiler_params=_cp("parallel", "arbitrary"), name="ssd",
    )(pd, pd, conv_w, conv_b.reshape(1, -1), pad_heads(dt_bias), pad_heads(-jnp.exp(a_log)),
      jnp.repeat(d_skip, SSM_HEAD).reshape(1, -1), norm_w.reshape(1, -1), expand(SSM_HEAD), expand(LANES))
    return out.reshape(bsz * seq, SSM_INNER)


def _merge_kernel(pg_ref, ya_ref, ga_ref, bb_ref, bc_ref, bd_ref, gu_ref, gb_ref, wb_ref, o_ref):
    pg = pg_ref[...]
    branches = ((ya_ref[...] * ga_ref[...]).astype(BF16), bb_ref[...], bc_ref[...], bd_ref[...])
    acc = None
    for bi, br in enumerate(branches):
        term = _sigmoid(_dot(pg, gu_ref[bi]) + gb_ref[bi]) * _dot(br, wb_ref[bi])
        acc = term if acc is None else acc + term
    o_ref[...] = acc.astype(o_ref.dtype)


def _merge(pg, ya, ga, bb, bc, bd, gate_up, gate_b, w_branch, tm=1024, tn=512):
    m = pg.shape[0]
    nb, kw, d = w_branch.shape
    rows = lambda k: pl.BlockSpec((tm, k), lambda i, j: (i, 0))
    return pl.pallas_call(
        _merge_kernel, grid=(m // tm, d // tn),
        in_specs=[rows(gate_up.shape[1]), rows(kw), rows(kw), rows(kw), rows(kw), rows(kw),
                  pl.BlockSpec((nb, gate_up.shape[1], tn), lambda i, j: (0, 0, j)),
                  pl.BlockSpec((nb, 1, tn), lambda i, j: (0, 0, j)),
                  pl.BlockSpec((nb, kw, tn), lambda i, j: (0, 0, j))],
        out_specs=pl.BlockSpec((tm, tn), lambda i, j: (i, j)),
        out_shape=SDS((m, d), BF16), compiler_params=_cp("parallel", "parallel"), name="merge",
    )(pg, ya, ga, bb, bc, bd, gate_up, gate_b.reshape(nb, 1, d), w_branch)


SUBLANES = 8


def _store_slabs(ref, rows):
    w = ref.shape[-1]
    for j in range(SUBLANES):
        ref[:, j, :] = rows[:, j * w:(j + 1) * w]


def _norm_route_kernel(x_ref, g_ref, sc_ref, sh_ref, wr_ref, br_ref, h_ref, ids_ref, wts_ref, hist_ref):
    x = x_ref[0]
    ms = jnp.mean(x * x, axis=-1, keepdims=True)
    h = x * lax.rsqrt(ms + NORM_EPS) * g_ref[...] * (1.0 + sc_ref[0]) + sh_ref[0]
    _store_slabs(h_ref, h)
    lg = lax.dot_general(h, wr_ref[...], (((1,), (1,)), ((), ())), precision=HI, preferred_element_type=F32) + br_ref[...]
    lane = lax.broadcasted_iota(jnp.int32, lg.shape, 1)
    lane_f = lane.astype(F32)
    first = lambda hit: jnp.min(jnp.where(hit, lane_f, float(LANES)), axis=-1, keepdims=True)
    gmask = lane < MOE_GROUPS
    gl = jnp.where(gmask, lg, NEG)
    gmax = jnp.max(gl, axis=-1, keepdims=True)
    gsel = first(gl == gmax)
    gprob = 1.0 / jnp.sum(jnp.where(gmask, jnp.exp(gl - gmax), 0.0), axis=-1, keepdims=True)
    lo = MOE_GROUPS + gsel * MOE_EXPERTS_PER_GROUP
    emask = (lane_f >= lo) & (lane_f < lo + MOE_EXPERTS_PER_GROUP)
    el = jnp.where(emask, lg, NEG)
    v1 = jnp.max(el, axis=-1, keepdims=True)
    i1 = first(el == v1)
    el2 = jnp.where(lane_f == i1, NEG, el)
    v2 = jnp.max(el2, axis=-1, keepdims=True)
    i2 = first((el2 == v2) & emask & (lane_f != i1))
    t = jnp.exp(v2 - v1)
    w1 = gprob / (1.0 + t)
    w2 = gprob * t / (1.0 + t)
    ids_ref[0] = jnp.where(lane == 0, i1, jnp.where(lane == 1, i2, float(MOE_GROUPS))).astype(jnp.int32) - MOE_GROUPS
    wts_ref[0] = jnp.where(lane == 0, w1, jnp.where(lane == 1, w2, 0.0))
    chosen = jnp.where((lane_f == i1) | (lane_f == i2), 1.0, 0.0)
    hist_ref[...] = jnp.broadcast_to(jnp.sum(chosen, axis=0, keepdims=True), hist_ref.shape)


def _norm_route(x, g, scale, shift, rg_wt, rg_b, re_wt, re_b, tm=256):
    bsz, seq, d = x.shape
    n_log = MOE_GROUPS + MOE_EXPERTS
    wr = jnp.zeros((LANES, d), F32).at[:n_log].set(jnp.concatenate([rg_wt, re_wt], axis=0))
    br = jnp.zeros((1, LANES), F32).at[0, :n_log].set(jnp.concatenate([rg_b, re_b]))
    tok = lambda width: pl.BlockSpec((1, tm, width), lambda b, i: (b, i, 0))
    m, dw, nt = bsz * seq, d // SUBLANES, seq // tm
    h, ids, wts, hist = pl.pallas_call(
        _norm_route_kernel, grid=(bsz, nt),
        in_specs=[tok(d), pl.BlockSpec((1, d), lambda b, i: (0, 0)),
                  pl.BlockSpec((1, 1, d), lambda b, i: (b, 0, 0)), pl.BlockSpec((1, 1, d), lambda b, i: (b, 0, 0)),
                  pl.BlockSpec((LANES, d), lambda b, i: (0, 0)), pl.BlockSpec((1, LANES), lambda b, i: (0, 0))],
        out_specs=[pl.BlockSpec((tm, SUBLANES, dw), lambda b, i: (b * nt + i, 0, 0)), tok(LANES), tok(LANES),
                   pl.BlockSpec((SUBLANES, LANES), lambda b, i: (b * nt + i, 0))],
        out_shape=[SDS((m, SUBLANES, dw), F32), SDS((bsz, seq, LANES), jnp.int32), SDS((bsz, seq, LANES), F32),
                   SDS((bsz * nt * SUBLANES, LANES), F32)],
        compiler_params=_cp("parallel", "parallel"), name="norm_route",
    )(x, g.reshape(1, d), scale.reshape(bsz, 1, d), shift.reshape(bsz, 1, d), wr, br)
    counts = jnp.sum(hist[::SUBLANES, MOE_GROUPS:n_log], axis=0).astype(jnp.int32)
    return h, ids.reshape(m, LANES)[:, :MOE_TOP_K], wts.reshape(m, LANES)[:, :MOE_TOP_K], counts


ROW_COPY_UNROLL = 8


FFN_W_SLOTS = 4
FFN_W_CHUNKS = 4


def _ffn_kernel(layer, be_ref, nb_ref, tok_ref, dst_ref, cnt_ref, h_hbm, wg_hbm, wu_hbm, wd_hbm, sw_ref, y_hbm,
                xbuf, ybuf, xrow, wg_s, wu_s, wd_s, stage_in, stage_out, gsem, ssem, wsem):
    j = pl.program_id(0)
    nb = nb_ref[0]
    cur = j % 2

    def load_expert(e):
        d, ff = wg_s.shape
        rin, rout = d // FFN_W_CHUNKS, ff // FFN_W_CHUNKS
        steps = []
        for src, dst, stage, rows in ((wg_hbm, wg_s, stage_in, rin), (wu_hbm, wu_s, stage_in, rin),
                                      (wd_hbm, wd_s, stage_out, rout)):
            for c in range(FFN_W_CHUNKS):
                steps.append((src.at[layer, e, pl.ds(c * rows, rows)], stage, dst.at[pl.ds(c * rows, rows)]))
        slots = stage_in.shape[0]
        copies = [pltpu.make_async_copy(src, stage.at[i % slots], wsem.at[i % slots])
                  for i, (src, stage, _) in enumerate(steps)]
        for cp in copies[:slots - 1]:
            cp.start()
        for i, (_, stage, dst) in enumerate(steps):
            if i + slots - 1 < len(steps):
                copies[i + slots - 1].start()
            copies[i].wait()
            dst[...] = stage[i % slots].astype(BF16)

    def gather_copy(buf, s, tok):
        return pltpu.make_async_copy(h_hbm.at[pl.ds(tok, 1)], xbuf.at[buf, pl.ds(s, 1)], gsem.at[buf])

    def scatter_copy(buf, s, dst):
        return pltpu.make_async_copy(ybuf.at[buf, pl.ds(s, 1)], y_hbm.at[pl.ds(dst, 1)], ssem.at[buf])

    def start_gather(blk, buf):
        def body(s, carry):
            gather_copy(buf, s, tok_ref[blk * MOE_ROWS + s]).start()
            return carry
        lax.fori_loop(0, MOE_ROWS, body, 0, unroll=ROW_COPY_UNROLL)

    def wait_gather(buf):
        pltpu.make_async_copy(h_hbm.at[pl.ds(0, MOE_ROWS)], xbuf.at[buf], gsem.at[buf]).wait()

    def start_scatter(blk, buf):
        def body(s, carry):
            scatter_copy(buf, s, dst_ref[blk * MOE_ROWS + s]).start()
            return carry
        full = cnt_ref[blk] == MOE_ROWS

        @pl.when(full)
        def _():
            lax.fori_loop(0, MOE_ROWS, body, 0, unroll=ROW_COPY_UNROLL)

        @pl.when(jnp.logical_not(full))
        def _():
            lax.fori_loop(0, cnt_ref[blk], body, 0)

    def wait_scatter(blk, buf):
        n = cnt_ref[blk]

        @pl.when(n > 0)
        def _():
            pltpu.make_async_copy(ybuf.at[buf, pl.ds(0, n)], y_hbm.at[pl.ds(0, n)], ssem.at[buf]).wait()

    @pl.when(j == 0)
    def _():
        start_gather(0, 0)

    @pl.when(j + 1 < nb)
    def _():
        start_gather(j + 1, 1 - cur)

    @pl.when(j < nb)
    def _():
        e = be_ref[j]

        @pl.when((j == 0) | (e != be_ref[jnp.maximum(j - 1, 0)]))
        def _():
            load_expert(e)

        wait_gather(cur)

        @pl.when(j >= 2)
        def _():
            wait_scatter(j - 2, cur)

        dw = xbuf.shape[-1]
        for c in range(SUBLANES):
            xrow[:, c * dw:(c + 1) * dw] = xbuf[cur, :, c, :]
        x = xrow[...].astype(BF16)
        hidden = (_silu(_dot(x, wg_s[...])) * _dot(x, wu_s[...])).astype(BF16)
        _store_slabs(ybuf.at[cur], _dot(hidden, wd_s[...]) * sw_ref[...])
        start_scatter(j, cur)

    @pl.when(j == nb - 1)
    def _():
        @pl.when(j >= 1)
        def _():
            wait_scatter(j - 1, 1 - cur)
        wait_scatter(j, cur)


def _moe_plan(expert, weight, counts):
    n_tok = expert.shape[0]
    n_assign = n_tok * MOE_TOP_K
    flat_e = expert.reshape(-1)
    flat_w = weight.reshape(-1)
    order = jnp.argsort(flat_e).astype(jnp.int32)
    padded = (counts + MOE_ROWS - 1) // MOE_ROWS * MOE_ROWS
    pad_end = jnp.cumsum(padded)
    pad_start = pad_end - padded
    start = jnp.cumsum(counts) - counts
    n_blocks = n_assign // MOE_ROWS + MOE_EXPERTS
    n_slots = n_blocks * MOE_ROWS
    blk0 = jnp.arange(n_blocks, dtype=jnp.int32) * MOE_ROWS
    block_e = jnp.minimum(jnp.searchsorted(pad_end, blk0, side='right'), MOE_EXPERTS - 1).astype(jnp.int32)
    block_cnt = jnp.clip(counts[block_e] - (blk0 - pad_start[block_e]), 0, MOE_ROWS).astype(jnp.int32)
    slot_e = jnp.repeat(block_e, MOE_ROWS)
    pos = jnp.arange(n_slots, dtype=jnp.int32) - pad_start[slot_e]
    valid = pos < counts[slot_e]
    assign = order[jnp.clip(start[slot_e] + pos, 0, n_assign - 1)]
    tok, k = assign // MOE_TOP_K, assign % MOE_TOP_K
    slot_tok = jnp.where(valid, tok, 0).astype(jnp.int32)
    slot_w = jnp.where(valid, flat_w[assign], 0.0)
    slot_dst = jnp.where(valid, k * n_tok + tok, 0).astype(jnp.int32)
    n_used_blocks = (pad_end[-1:] // MOE_ROWS).astype(jnp.int32)
    return slot_tok, slot_w, block_e, slot_dst, block_cnt, n_used_blocks


def _cast_kernel(x_ref, o_ref):
    o_ref[...] = x_ref[0].astype(o_ref.dtype)


def _cast_bf16(w, layer, block_bytes=2 * 1024 * 1024):
    _, e, r, c = w.shape
    tr = min(r, block_bytes // (4 * c))
    return pl.pallas_call(
        _cast_kernel, grid=(e, r // tr),
        in_specs=[pl.BlockSpec((1, 1, tr, c), lambda i, j: (layer, i, j, 0))],
        out_specs=pl.BlockSpec((1, tr, c), lambda i, j: (i, j, 0)),
        out_shape=SDS((e, r, c), BF16), compiler_params=_cp("parallel", "parallel"), name="cast_bf16")(w)


def _moe_apply(h, expert, weight, counts, w_gate, w_up, w_down, layer):
    n_tok, _, dw = h.shape
    d = dw * SUBLANES
    ff = w_gate.shape[-1]
    slot_tok, slot_w, block_e, slot_dst, block_cnt, n_used_blocks = _moe_plan(expert, weight, counts)
    n_slots = slot_tok.shape[0]
    any_space = pl.BlockSpec(memory_space=pl.ANY)
    return pl.pallas_call(
        functools.partial(_ffn_kernel, layer),
        grid_spec=pltpu.PrefetchScalarGridSpec(
            num_scalar_prefetch=5, grid=(n_slots // MOE_ROWS,),
            in_specs=[any_space, any_space, any_space, any_space,
                      pl.BlockSpec((MOE_ROWS, 1), lambda j, be, nb, *_: (jnp.minimum(j, nb[0] - 1), 0))],
            out_specs=any_space,
            scratch_shapes=[pltpu.VMEM((2, MOE_ROWS, SUBLANES, dw), F32), pltpu.VMEM((2, MOE_ROWS, SUBLANES, dw), F32),
                            pltpu.VMEM((MOE_ROWS, d), F32),
                            pltpu.VMEM((d, ff), BF16), pltpu.VMEM((d, ff), BF16), pltpu.VMEM((ff, d), BF16),
                            pltpu.VMEM((FFN_W_SLOTS, d // FFN_W_CHUNKS, ff), F32),
                            pltpu.VMEM((FFN_W_SLOTS, ff // FFN_W_CHUNKS, d), F32),
                            pltpu.SemaphoreType.DMA((2,)), pltpu.SemaphoreType.DMA((2,)),
                            pltpu.SemaphoreType.DMA((FFN_W_SLOTS,))]),
        out_shape=SDS((MOE_TOP_K * n_tok, SUBLANES, dw), F32), compiler_params=_cp("arbitrary"), name="moe_ffn",
    )(block_e, n_used_blocks, slot_tok, slot_dst, block_cnt, h, w_gate, w_up, w_down, slot_w.reshape(n_slots, 1))


def _moe_combine_kernel(x_ref, y0_ref, y1_ref, g_ref, o_ref):
    dw = y0_ref.shape[-1]
    for j in range(SUBLANES):
        cols = slice(j * dw, (j + 1) * dw)
        o_ref[0, :, cols] = x_ref[0, :, cols] + g_ref[0, :, cols] * (y0_ref[0, 0, :, j, :] + y1_ref[0, 0, :, j, :])


def _moe_combine(x, y2, gate, tm=256):
    bsz, seq, d = x.shape
    dw = d // SUBLANES
    y5 = y2.reshape(MOE_TOP_K, bsz, seq, SUBLANES, dw)
    tok = pl.BlockSpec((1, tm, d), lambda b, i: (b, i, 0))
    slab = lambda k: pl.BlockSpec((1, 1, tm, SUBLANES, dw), lambda b, i: (k, b, i, 0, 0))
    return pl.pallas_call(
        _moe_combine_kernel, grid=(bsz, seq // tm),
        in_specs=[tok, slab(0), slab(1), pl.BlockSpec((1, 1, d), lambda b, i: (b, 0, 0))],
        out_specs=tok, out_shape=SDS((bsz, seq, d), F32),
        compiler_params=_cp("parallel", "parallel"), name="moe_combine",
    )(x, y5, y5, gate.reshape(bsz, 1, d))


def _pack_plan(offs, n_cols):
    ng, wh = len(ATT_GROUPS), ATT_HEADS * ATT_HEAD
    plan, starts = [], []

    def dense(first, count):
        starts.append(len(plan))
        for c0 in range(0, count, PACK_COLS):
            plan.append([(first + c0, min(PACK_COLS, count - c0), 0)])

    dense(offs[0], A_MAIN)
    o_w = offs[0] + A_MAIN
    o_a = o_w + RWKV_DECAY_RANK
    o_g = o_a + RWKV_ICLR_RANK
    plan.append([(o_w, RWKV_DECAY_RANK, 0), (o_a, RWKV_ICLR_RANK, 128), (o_g, RWKV_GATE_RANK, 256)])
    for gi in range(ng):
        starts.append(len(plan))
        for t in range(3):
            for c0 in range(0, wh, PACK_COLS):
                plan.append([(offs[1] + (t * ng + gi) * wh + c0, PACK_COLS, 0)])
    dense(offs[2], offs[3] - offs[2])
    dense(offs[3], offs[4] - offs[3])
    dense(offs[4], n_cols - offs[4])
    return plan, starts


def _pack_w_in_kernel(layer, plan, w_hbm, o_ref, slab, sem):
    step = pl.program_id(0)

    def copies(j):
        return [pltpu.make_async_copy(w_hbm.at[layer, pl.ds(src, n)], slab.at[j % 2, pl.ds(dst, n)], sem.at[j % 2])
                for src, n, dst in plan[j]]

    def fetch(j):
        covered = sorted((dst, dst + n) for _, n, dst in plan[j])
        edge = 0
        for lo, hi in covered + [(PACK_COLS, PACK_COLS)]:
            if lo > edge:
                slab[j % 2, edge:lo, :] = jnp.zeros((lo - edge, slab.shape[2]), F32)
            edge = hi
        for cp in copies(j):
            cp.start()

    for j in range(len(plan)):
        @pl.when(step == j)
        def _():
            if j == 0:
                fetch(0)
            if j + 1 < len(plan):
                fetch(j + 1)
            for cp in copies(j):
                cp.wait()

    o_ref[...] = slab[step % 2].T.astype(BF16)


def _pack_w_in(w, layer, offs):
    _, d, n = w.shape
    plan, starts = _pack_plan(offs, n)
    packed = pl.pallas_call(
        functools.partial(_pack_w_in_kernel, layer, plan), grid=(len(plan),),
        in_specs=[pl.BlockSpec(memory_space=pl.ANY)],
        out_specs=pl.BlockSpec((d, PACK_COLS), lambda j: (0, j)),
        out_shape=SDS((d, len(plan) * PACK_COLS), BF16),
        scratch_shapes=[pltpu.VMEM((2, PACK_COLS, d), F32), pltpu.SemaphoreType.DMA((2,))],
        compiler_params=_cp("arbitrary"), name="pack_w_in",
    )(jnp.swapaxes(w, 1, 2))
    return packed, starts + [len(plan)]


def kernel(x, c, ada_w, ada_b, ada_table, norm1_g, norm2_g, w_in, rwkv_mu, rwkv_w0, rwkv_w2, rwkv_a0, rwkv_a2,
           rwkv_g2, rwkv_kk, rwkv_ka, rwkv_rk, rwkv_lnx_w, rwkv_lnx_b, pool_w, pool_scale, ssm_conv_w, ssm_conv_b,
           ssm_dt_bias, ssm_a_log, ssm_d, ssm_norm_w, gate_up, gate_b, w_branch, w_out, router_group_w,
           router_group_b, router_expert_w, router_expert_b, exp_w_gate, exp_w_up, exp_w_down, final_g):
    bsz, seq, d = x.shape
    m = bsz * seq
    a_cols = rwkv_mu.shape[1]
    b_cols = 3 * len(ATT_GROUPS) * ATT_HEADS * ATT_HEAD
    d_cols = SSM_INNER + SSM_CONV_DIM + SSM_HEADS
    offs = (0, a_cols, a_cols + b_cols, a_cols + b_cols + BRANCH_WIDTH, a_cols + b_cols + BRANCH_WIDTH + d_cols)
    cond = _cond(c, ada_w, ada_b)
    for l in range(DEPTH):
        shift1, scale1, gate1, shift2, scale2, gate2 = jnp.split(cond + ada_table[l], 6, axis=-1)
        dilations = tuple(dil for _, dil in ATT_GROUPS)
        h, *h_phases = _norm_mod_phases(x, norm1_g[l], scale1, shift1, [dil for dil in dilations if dil > 1])
        h = h.reshape(m, d)
        h_of = {1: h, **{dil: hp.reshape(m, d) for dil, hp in zip([dil for dil in dilations if dil > 1], h_phases)}}
        w_packed, blk = _pack_w_in(w_in, l, offs)
        proj = lambda a, seg, dtype: _matmul(a, w_packed, dtype, blk[seg], blk[seg + 1] - blk[seg], tn=PACK_COLS)
        pa = proj(h, 0, F32).reshape(bsz, seq, A_PAD)
        pbs = [proj(h_of[dil], 1 + gi, BF16).reshape(bsz, dil, seq // dil, -1) for gi, dil in enumerate(dilations)]
        pc = proj(h, 4, F32).reshape(bsz, seq, BRANCH_WIDTH)
        pd = proj(h, 5, F32).reshape(bsz, seq, D_PAD)
        pg = proj(h, 6, BF16)
        ya, ga = _rwkv_mixer(pa, rwkv_mu[l], rwkv_w0[l], rwkv_w2[l], rwkv_a0[l], rwkv_a2[l], rwkv_g2[l],
                             rwkv_kk[l], rwkv_ka[l], rwkv_rk[l], rwkv_lnx_w[l], rwkv_lnx_b[l])
        ob = _attention_mixer(pbs)
        oc = _pool_mixer(pc, pool_w[l], pool_scale[l])
        od = _mamba_mixer(pd, ssm_conv_w[l], ssm_conv_b[l], ssm_dt_bias[l], ssm_a_log[l], ssm_d[l], ssm_norm_w[l])
        merged = _merge(pg, ya.reshape(m, BRANCH_WIDTH), ga.reshape(m, BRANCH_WIDTH), ob, oc, od,
                        _cast_bf16(gate_up, l), gate_b[l], _cast_bf16(w_branch, l))
        w_out_l = _cast_bf16(w_out.reshape(DEPTH, 1, d, d), l).reshape(d, d)
        x = _matmul_resid(merged, w_out_l, x.reshape(m, d), gate1, seq).reshape(bsz, seq, d)
        h2, expert, weight, counts = _norm_route(x, norm2_g[l], scale2, shift2,
                                                 jnp.swapaxes(router_group_w, 1, 2)[l], router_group_b[l],
                                                 jnp.swapaxes(router_expert_w, 1, 2)[l], router_expert_b[l])
        y2 = _moe_apply(h2, expert, weight, counts, exp_w_gate, exp_w_up, exp_w_down, l)
        x = _moe_combine(x, y2, gate2)
    zeros = jnp.zeros((bsz, d), F32)
    return _norm_mod(x, final_g, zeros, zeros, F32)
```

```python
import functools

import jax
import jax.numpy as jnp
from jax import lax
from jax.experimental import pallas as pl
from jax.experimental.pallas import tpu as pltpu

F32 = jnp.float32
BF16 = jnp.bfloat16
HI = lax.Precision.HIGHEST
SDS = jax.ShapeDtypeStruct

DEPTH = 2
NORM_EPS = 1e-6
BRANCH_WIDTH = 1024
RWKV_HEAD = 64
RWKV_HEADS = 16
RWKV_DECAY_RANK = 64
RWKV_ICLR_RANK = 64
RWKV_GATE_RANK = 160
RWKV_GN_EPS = 64e-5
RWKV_CHUNK = 64
ATT_HEAD = 128
ATT_GROUPS = ((128, 1), (512, 4), (2048, 16))
ATT_HEADS = 8
ATT_BLOCK = 128
POOL_WINDOWS = (2, 4, 8, 16)
POOL_GROUP = 256
POOL_HALO = 16
SSM_INNER = 1024
SSM_HEAD = 64
SSM_HEADS = 16
SSM_GROUPS = 4
SSM_STATE = 128
SSM_CONV = 4
SSM_CHUNK = 128
SSM_CONV_DIM = SSM_INNER + 2 * SSM_GROUPS * SSM_STATE
GATE_RANK = 256
MOE_GROUPS = 4
MOE_EXPERTS_PER_GROUP = 8
MOE_EXPERTS = 32
MOE_TOP_K = 2
MOE_FF = 512
MOE_ROWS = 256
LANES = 128
NEG = -1e30
A_MAIN = 3 * BRANCH_WIDTH
A_PAD = A_MAIN + 128 + 128 + 256
PACK_COLS = 512
D_PAD = -(-(SSM_INNER + SSM_CONV_DIM + SSM_HEADS) // PACK_COLS) * PACK_COLS
VMEM_LIMIT = 56 * 1024 * 1024


def _cp(*sem):
    return pltpu.CompilerParams(dimension_semantics=sem, vmem_limit_bytes=VMEM_LIMIT)


def _sigmoid(x):
    return 1.0 / (1.0 + jnp.exp(-x))


def _silu(x):
    return x * _sigmoid(x)


def _softplus(x):
    return jnp.maximum(x, 0.0) + jnp.log(1.0 + jnp.exp(-jnp.abs(x)))


def _dot(a, b):
    return jnp.dot(a, b, preferred_element_type=F32)


def _dot_hi(a, b):
    return jnp.dot(a, b, precision=HI, preferred_element_type=F32)


def _cond_kernel(c_ref, w_ref, b_ref, o_ref, a_ref):
    @pl.when(pl.program_id(0) == 0)
    def _():
        a_ref[...] = _silu(c_ref[...])

    k, tn = w_ref.shape
    w = w_ref[...]
    rows = []
    for r in range(a_ref.shape[0]):
        prod = w * jnp.concatenate([a_ref[r]] * (tn // LANES), axis=1)
        part = jnp.sum(prod.reshape(k // 8, 8, tn), axis=0)
        rows.append(jnp.sum(part, axis=0, keepdims=True))
    rows.append(jnp.zeros((o_ref.shape[0] - len(rows), tn), F32))
    o_ref[...] = jnp.concatenate(rows, axis=0) + b_ref[...]


def _cond(c, ada_w, ada_b):
    bsz, d = c.shape
    n = ada_w.shape[1]
    tn = 512
    c_lanes = jnp.broadcast_to(c[:, :, None], (bsz, d, LANES))
    out = pl.pallas_call(
        _cond_kernel, grid=(n // tn,),
        in_specs=[pl.BlockSpec((bsz, d, LANES), lambda j: (0, 0, 0)),
                  pl.BlockSpec((d, tn), lambda j: (0, j)),
                  pl.BlockSpec((1, tn), lambda j: (0, j))],
        out_specs=pl.BlockSpec((8, tn), lambda j: (0, j)),
        out_shape=SDS((8, n), F32), scratch_shapes=[pltpu.VMEM((bsz, d, LANES), F32)],
        compiler_params=_cp("arbitrary"), name="cond",
    )(c_lanes, ada_w, ada_b.reshape(1, n))
    return out[:bsz]


def _norm_mod_kernel(x_ref, g_ref, sc_ref, sh_ref, o_ref):
    x = x_ref[0]
    ms = jnp.mean(x * x, axis=-1, keepdims=True)
    y = x * lax.rsqrt(ms + NORM_EPS) * g_ref[...]
    o_ref[0] = (y * (1.0 + sc_ref[0]) + sh_ref[0]).astype(o_ref.dtype)


def _norm_mod(x, g, scale, shift, out_dtype, tm=256):
    bsz, seq, d = x.shape
    return pl.pallas_call(
        _norm_mod_kernel, grid=(bsz, seq // tm),
        in_specs=[pl.BlockSpec((1, tm, d), lambda b, i: (b, i, 0)),
                  pl.BlockSpec((1, d), lambda b, i: (0, 0)),
                  pl.BlockSpec((1, 1, d), lambda b, i: (b, 0, 0)),
                  pl.BlockSpec((1, 1, d), lambda b, i: (b, 0, 0))],
        out_specs=pl.BlockSpec((1, tm, d), lambda b, i: (b, i, 0)),
        out_shape=SDS((bsz, seq, d), out_dtype), compiler_params=_cp("parallel", "parallel"), name="norm_mod",
    )(x, g.reshape(1, d), scale.reshape(bsz, 1, d), shift.reshape(bsz, 1, d))


def _norm_mod_phases_kernel(x_ref, g_ref, sc_ref, sh_ref, o_ref, *refs):
    phase_refs, scr = refs[:-1], refs[-1]
    x = x_ref[0]
    tm, d = x.shape
    ms = jnp.mean(x * x, axis=-1, keepdims=True)
    y = x * lax.rsqrt(ms + NORM_EPS) * g_ref[...] * (1.0 + sc_ref[0]) + sh_ref[0]
    o_ref[0] = y.astype(o_ref.dtype)
    for ref in phase_refs:
        dil = ref.shape[1]
        n = tm // dil
        y3 = y.reshape(n, dil, d)
        for ph in range(dil):
            scr[ph * n:(ph + 1) * n, :] = y3[:, ph, :]
        for ph in range(dil):
            ref[0, ph] = scr[ph * n:(ph + 1) * n, :].astype(ref.dtype)


def _norm_mod_phases(x, g, scale, shift, dilations, tm=256):
    bsz, seq, d = x.shape
    tok = pl.BlockSpec((1, tm, d), lambda b, i: (b, i, 0))
    vec = pl.BlockSpec((1, 1, d), lambda b, i: (b, 0, 0))
    return pl.pallas_call(
        _norm_mod_phases_kernel, grid=(bsz, seq // tm),
        in_specs=[tok, pl.BlockSpec((1, d), lambda b, i: (0, 0)), vec, vec],
        out_specs=[tok] + [pl.BlockSpec((1, dil, tm // dil, d), lambda b, i: (b, 0, i, 0)) for dil in dilations],
        out_shape=[SDS((bsz, seq, d), BF16)] + [SDS((bsz, dil, seq // dil, d), BF16) for dil in dilations],
        scratch_shapes=[pltpu.VMEM((tm, d), F32)],
        compiler_params=_cp("parallel", "parallel"), name="norm_mod_phases",
    )(x, g.reshape(1, d), scale.reshape(bsz, 1, d), shift.reshape(bsz, 1, d))


def _mm_kernel(a_ref, b_ref, o_ref):
    o_ref[...] = _dot(a_ref[...], b_ref[...]).astype(o_ref.dtype)


def _matmul(a, b, out_dtype, first_block=0, n_blocks=None, tm=1024, tn=512):
    m, k = a.shape
    n = b.shape[1] if n_blocks is None else n_blocks * tn
    return pl.pallas_call(
        _mm_kernel, grid=(m // tm, n // tn),
        in_specs=[pl.BlockSpec((tm, k), lambda i, j: (i, 0)),
                  pl.BlockSpec((k, tn), lambda i, j: (0, first_block + j))],
        out_specs=pl.BlockSpec((tm, tn), lambda i, j: (i, j)),
        out_shape=SDS((m, n), out_dtype), compiler_params=_cp("parallel", "parallel"), name="matmul",
    )(a, b)


def _mm_resid_kernel(a_ref, b_ref, r_ref, g_ref, o_ref):
    o_ref[...] = r_ref[...] + g_ref[0] * _dot(a_ref[...], b_ref[...])


def _matmul_resid(a, b, resid, gate, seq, tm=1024, tn=512):
    m, k = a.shape
    n = b.shape[1]
    bsz = gate.shape[0]
    return pl.pallas_call(
        _mm_resid_kernel, grid=(m // tm, n // tn),
        in_specs=[pl.BlockSpec((tm, k), lambda i, j: (i, 0)),
                  pl.BlockSpec((k, tn), lambda i, j: (0, j)),
                  pl.BlockSpec((tm, tn), lambda i, j: (i, j)),
                  pl.BlockSpec((1, 1, tn), lambda i, j: (i * tm // seq, 0, j))],
        out_specs=pl.BlockSpec((tm, tn), lambda i, j: (i, j)),
        out_shape=SDS((m, n), F32), compiler_params=_cp("parallel", "parallel"), name="matmul_resid",
    )(a, b, resid, gate.reshape(bsz, 1, n))


def _rwkv_prep_kernel(p_ref, halo_ref, mu_ref, w0_ref, a0_ref, kkw_ref, kaw_ref, w2_ref, a2_ref, g2_ref,
                      r_ref, k_ref, v_ref, kk_ref, a_ref, lw_ref, g_ref):
    x = p_ref[0]
    w = BRANCH_WIDTH
    prev_first = jnp.where(pl.program_id(1) > 0, halo_ref[0][7:8, :], 0.0)
    row = lax.broadcasted_iota(jnp.int32, x.shape, 0)
    prev = jnp.where(row == 0, prev_first, pltpu.roll(x, 1, axis=0))
    p = x + (prev - x) * mu_ref[...]
    r, k, v = p[:, 0:w], p[:, w:2 * w], p[:, 2 * w:3 * w]
    xw, xa, xg = p[:, A_MAIN:A_MAIN + 128], p[:, A_MAIN + 128:A_MAIN + 256], p[:, A_MAIN + 256:A_PAD]
    wlog = -_softplus(-(w0_ref[...] + _dot_hi(jnp.tanh(xw), w2_ref[...]))) - 0.5
    a = _sigmoid(a0_ref[...] + _dot_hi(xa, a2_ref[...]))
    r_ref[0] = r
    k_ref[0] = k * (1.0 + (a - 1.0) * kaw_ref[...])
    v_ref[0] = v
    kk_ref[0] = k * kkw_ref[...]
    a_ref[0] = a
    lw_ref[0] = -jnp.exp(wlog)
    g_ref[0] = _dot_hi(_sigmoid(xg), g2_ref[...])


def _rwkv_prep(pa, mu, w0, a0, kkw, kaw, w2, a2, g2, tm=256):
    bsz, seq, ap = pa.shape
    w = BRANCH_WIDTH
    row = lambda t: t.reshape(1, -1)
    full = lambda shape: pl.BlockSpec(shape, lambda b, i: (0, 0))
    out_spec = pl.BlockSpec((1, tm, w), lambda b, i: (b, i, 0))
    return pl.pallas_call(
        _rwkv_prep_kernel, grid=(bsz, seq // tm),
        in_specs=[pl.BlockSpec((1, tm, ap), lambda b, i: (b, i, 0)),
                  pl.BlockSpec((1, 8, ap), lambda b, i: (b, jnp.maximum(i * (tm // 8) - 1, 0), 0)),
                  full((1, ap)), full((1, w)), full((1, w)), full((1, w)), full((1, w)),
                  full((128, w)), full((128, w)), full((256, w))],
        out_specs=[out_spec] * 7,
        out_shape=[SDS((bsz, seq, w), F32)] * 7,
        compiler_params=_cp("parallel", "parallel"), name="rwkv_prep",
    )(pa, pa, row(mu), row(w0), row(a0), row(kkw), row(kaw), w2, a2, g2)


def _rwkv_rec_kernel(r_ref, k_ref, v_ref, kk_ref, a_ref, lw_ref, rk_ref, lnw_ref, lnb_ref, o_ref, s_ref):
    @pl.when(pl.program_id(1) == 0)
    def _():
        s_ref[...] = jnp.zeros_like(s_ref)

    nh, n = s_ref.shape[0], s_ref.shape[1]
    heads = lambda ref: jnp.stack([ref[0, :, h * n:(h + 1) * n] for h in range(nh)])
    r, k, v, kk, a, lw = (heads(ref) for ref in (r_ref, k_ref, v_ref, kk_ref, a_ref, lw_ref))
    c = r.shape[1]
    kn = kk / jnp.maximum(jnp.sqrt(jnp.sum(kk * kk, axis=-1, keepdims=True)), 1e-12)
    b = kn * a
    row = lax.broadcasted_iota(jnp.int32, (c, c), 0)
    col = lax.broadcasted_iota(jnp.int32, (c, c), 1)
    strict, incl = (row > col)[None], (row >= col)[None]
    bdot = lambda spec: (lambda x, y: jnp.einsum(spec, x.astype(BF16), y.astype(BF16), preferred_element_type=F32))
    nt, nn, tn = bdot('hik,hjk->hij'), bdot('hij,hjk->hik'), bdot('hiv,hik->hvk')
    tril = jnp.broadcast_to((row >= col).astype(BF16)[None], (nh, c, c))
    lw_hi = lw.astype(BF16)
    lw_r1 = lw - lw_hi.astype(F32)
    lw_mid = lw_r1.astype(BF16)
    lw_lo = (lw_r1 - lw_mid.astype(F32)).astype(BF16)
    lc = nn(tril, lw_hi) + (nn(tril, lw_mid) + nn(tril, lw_lo))
    lc_last = lc[:, c - 1:c, :]
    e_neg, e_end = jnp.exp(-lc), jnp.exp(lc_last - lc)
    kt, rt = kn * jnp.exp(lc - lw), r * jnp.exp(lc)
    bt, kd = b * e_neg, k * e_neg
    kr = jnp.concatenate([kt, rt], axis=1)
    sc = nt(kr, jnp.concatenate([bt, kd], axis=1))
    lm = jnp.where(strict, -sc[:, :c, :c], 0.0)
    ak = jnp.where(strict, sc[:, :c, c:], 0.0)
    bb = jnp.where(incl, sc[:, c:, :c], 0.0)
    bk = jnp.where(incl, sc[:, c:, c:], 0.0)
    s0 = s_ref[...]
    kr_s0 = nt(kr, s0)
    abk_v = nn(jnp.concatenate([ak, bk], axis=1), v)
    u = -(kr_s0[:, :c] + abk_v[:, :c])
    pw = lm
    n_doublings = c.bit_length() - 1
    for it in range(n_doublings):
        u = u + nn(pw, u)
        if it + 1 < n_doublings:
            pw = nn(pw, pw)
    y = kr_s0[:, c:] + nn(bb, u) + abk_v[:, c:]
    s_ref[...] = s0 * jnp.exp(lc_last) + tn(jnp.concatenate([u, v], axis=1),
                                            jnp.concatenate([b * e_end, k * e_end], axis=1))
    mean = jnp.mean(y, axis=-1, keepdims=True)
    var = jnp.mean(jnp.square(y - mean), axis=-1, keepdims=True)
    out = (y - mean) * lax.rsqrt(var + RWKV_GN_EPS) * lnw_ref[...] + lnb_ref[...]
    out = out + jnp.sum(r * k * rk_ref[...], axis=-1, keepdims=True) * v
    for h in range(nh):
        o_ref[0, :, h * n:(h + 1) * n] = out[h]


def _rwkv_rec(r, k, v, kk, a, lw, rk, lnw, lnb, chunk=RWKV_CHUNK):
    bsz, seq, w = r.shape
    nh, n = rk.shape
    blk = pl.BlockSpec((1, chunk, w), lambda b, i: (b, i, 0))
    par = pl.BlockSpec((nh, 1, n), lambda b, i: (0, 0, 0))
    return pl.pallas_call(
        _rwkv_rec_kernel, grid=(bsz, seq // chunk),
        in_specs=[blk] * 6 + [par] * 3, out_specs=blk,
        out_shape=SDS((bsz, seq, w), F32),
        scratch_shapes=[pltpu.VMEM((nh, n, n), F32)],
        compiler_params=_cp("parallel", "arbitrary"), name="rwkv_rec",
    )(r, k, v, kk, a, lw, rk.reshape(nh, 1, n), lnw.reshape(nh, 1, n), lnb.reshape(nh, 1, n))


def _pad_rows(w, rows):
    return jnp.zeros((rows, w.shape[1]), w.dtype).at[:w.shape[0]].set(w)


def _rwkv_pack_cols(t):
    lead = t.shape[:-1]
    z = lambda n: jnp.zeros(lead + (n,), t.dtype)
    o1 = A_MAIN + RWKV_DECAY_RANK
    o2 = o1 + RWKV_ICLR_RANK
    return jnp.concatenate([t[..., :A_MAIN], t[..., A_MAIN:o1], z(128 - RWKV_DECAY_RANK), t[..., o1:o2],
                            z(128 - RWKV_ICLR_RANK), t[..., o2:], z(256 - RWKV_GATE_RANK)], axis=-1)


def _rwkv_mixer(pa, mu, w0, w2, a0, a2, g2, kkw, kaw, rk, lnw, lnb):
    bsz, seq, _ = pa.shape
    r, k, v, kk, a, lw, g = _rwkv_prep(pa, _rwkv_pack_cols(mu), w0, a0, kkw, kaw,
                                       _pad_rows(w2, 128), _pad_rows(a2, 128), _pad_rows(g2, 256))
    y = _rwkv_rec(r, k, v, kk, a, lw, rk, lnw.reshape(RWKV_HEADS, RWKV_HEAD), lnb.reshape(RWKV_HEADS, RWKV_HEAD))
    return y, g


def _att_kernel(q_ref, kc_ref, kp_ref, vc_ref, vp_ref, o_ref, lse_ref):
    nt = lambda x, y: lax.dot_general(x, y, (((1,), (1,)), ((), ())), preferred_element_type=F32)
    scale = ATT_HEAD ** -0.5
    qi = lax.broadcasted_iota(jnp.int32, (ATT_BLOCK, ATT_BLOCK), 0)
    ki = lax.broadcasted_iota(jnp.int32, (ATT_BLOCK, ATT_BLOCK), 1)
    cur_ok = ki <= qi
    prev_ok = (ki >= qi) & (pl.program_id(2) > 0)
    lane = lax.broadcasted_iota(jnp.int32, (ATT_BLOCK, LANES), 1)
    lse_all = jnp.zeros((ATT_BLOCK, LANES), F32)
    for h in range(ATT_HEADS):
        cols = slice(h * ATT_HEAD, (h + 1) * ATT_HEAD)
        q = q_ref[0, 0, :, cols]
        s_c = jnp.where(cur_ok, nt(q, kc_ref[0, 0, :, cols]) * scale, NEG)
        s_p = jnp.where(prev_ok, nt(q, kp_ref[0, 0, :, cols]) * scale, NEG)
        m = jnp.maximum(jnp.max(s_c, axis=-1, keepdims=True), jnp.max(s_p, axis=-1, keepdims=True))
        e_c, e_p = jnp.exp(s_c - m), jnp.exp(s_p - m)
        den = jnp.sum(e_c, axis=-1, keepdims=True) + jnp.sum(e_p, axis=-1, keepdims=True)
        o = _dot(e_c.astype(BF16), vc_ref[0, 0, :, cols]) + _dot(e_p.astype(BF16), vp_ref[0, 0, :, cols])
        o_ref[0, 0, :, cols] = o / den
        lse_all = jnp.where(lane == h, m + jnp.log(den), lse_all)
    lse_ref[0, 0] = lse_all


def _att_group(pb, gi):
    bsz, dilation, n_phase, _ = pb.shape
    w = ATT_HEADS * ATT_HEAD
    nblk = n_phase // ATT_BLOCK
    blk = (1, 1, ATT_BLOCK, w)
    cur = lambda t: pl.BlockSpec(blk, lambda b, ph, i: (b, ph, i, t))
    prev = lambda t: pl.BlockSpec(blk, lambda b, ph, i: (b, ph, jnp.maximum(i - 1, 0), t))
    return pl.pallas_call(
        _att_kernel, grid=(bsz, dilation, nblk),
        in_specs=[cur(0), cur(1), prev(1), cur(2), prev(2)],
        out_specs=[pl.BlockSpec(blk, lambda b, ph, i: (b, ph, i, 0)),
                   pl.BlockSpec((1, 1, ATT_BLOCK, LANES), lambda b, ph, i: (b, ph, i, 0))],
        out_shape=[SDS((bsz, dilation, n_phase, w), F32), SDS((bsz, dilation, n_phase, LANES), F32)],
        compiler_params=_cp("parallel", "parallel", "arbitrary"), name=f"att_g{gi}",
    )(pb, pb, pb, pb, pb)


def _att_combine_kernel(o0, o1, o2, l0, l1, l2, e_ref, out_ref):
    tm = out_ref.shape[0]

    def tokens(ref):
        d = ref.shape[1]
        if d == 1:
            return ref[0, 0]
        return jnp.stack([ref[0, ph] for ph in range(d)], axis=1).reshape(tm, ref.shape[-1])

    lses = [tokens(l) for l in (l0, l1, l2)]
    m = jnp.maximum(jnp.maximum(lses[0], lses[1]), lses[2])
    es = [jnp.exp(l - m) for l in lses]
    inv = 1.0 / (es[0] + es[1] + es[2])
    acc = None
    for e, o in zip(es, (o0, o1, o2)):
        term = _dot_hi(e * inv, e_ref[...]) * tokens(o)
        acc = term if acc is None else acc + term
    out_ref[...] = acc.astype(out_ref.dtype)


def _attention_mixer(pbs, tm=256):
    res = [_att_group(pb, gi) for gi, pb in enumerate(pbs)]
    bsz, _, seq, w = res[0][0].shape
    nt = seq // tm
    spec = lambda t: pl.BlockSpec((1, t.shape[1], tm // t.shape[1], t.shape[3]), lambda b, i: (b, 0, i, 0))
    expand = (jnp.arange(LANES, dtype=jnp.int32)[:, None] == jnp.arange(w, dtype=jnp.int32)[None, :] // ATT_HEAD).astype(F32)
    args = [t[0] for t in res] + [t[1] for t in res]
    return pl.pallas_call(
        _att_combine_kernel, grid=(bsz, nt),
        in_specs=[spec(t) for t in args] + [pl.BlockSpec((LANES, w), lambda b, i: (0, 0))],
        out_specs=pl.BlockSpec((tm, w), lambda b, i: (b * nt + i, 0)),
        out_shape=SDS((bsz * seq, w), BF16), compiler_params=_cp("parallel", "parallel"), name="att_combine",
    )(*args, expand)


def _pool_kernel(x_ref, halo_ref, w_ref, sc_ref, o_ref, ext_ref):
    tm = x_ref.shape[1]
    i = pl.program_id(1)
    x = x_ref[0]
    ext_ref[0:POOL_HALO, :] = jnp.where(i > 0, halo_ref[0], 0.0)
    ext_ref[POOL_HALO:, :] = x
    pos = i * tm + lax.broadcasted_iota(jnp.int32, (tm, POOL_GROUP), 0)
    outs = []
    for gi, win in enumerate(POOL_WINDOWS):
        cols = slice(gi * POOL_GROUP, (gi + 1) * POOL_GROUP)
        xg = x[:, cols]
        s = xg
        for j in range(1, win):
            s = s + ext_ref[pl.ds(POOL_HALO - j, tm), cols]
        mixed = s / jnp.minimum(pos + 1, win).astype(F32) - xg
        outs.append(_dot(mixed.astype(BF16), w_ref[gi]))
    o_ref[0] = (jnp.concatenate(outs, axis=-1) * sc_ref[...]).astype(o_ref.dtype)


def _pool_mixer(pc, w_pool, scale, tm=256):
    bsz, seq, w = pc.shape
    out = pl.pallas_call(
        _pool_kernel, grid=(bsz, seq // tm),
        in_specs=[pl.BlockSpec((1, tm, w), lambda b, i: (b, i, 0)),
                  pl.BlockSpec((1, POOL_HALO, w), lambda b, i: (b, jnp.maximum(i * (tm // POOL_HALO) - 1, 0), 0)),
                  pl.BlockSpec(w_pool.shape, lambda b, i: (0, 0, 0)),
                  pl.BlockSpec((1, w), lambda b, i: (0, 0))],
        out_specs=pl.BlockSpec((1, tm, w), lambda b, i: (b, i, 0)),
        out_shape=SDS((bsz, seq, w), BF16),
        scratch_shapes=[pltpu.VMEM((tm + POOL_HALO, w), F32)],
        compiler_params=_cp("parallel", "parallel"), name="pool",
    )(pc, pc, w_pool.astype(BF16), scale.reshape(1, w))
    return out.reshape(bsz * seq, w)


SSM_HALO = 8


def _ssd_kernel(p_ref, halo_ref, cw_ref, cb_ref, dtb_ref, ah_ref, dsk_ref, nw_ref, e64_ref, e128_ref,
                o_ref, ext_ref, h_ref):
    ci = pl.program_id(1)
    q = SSM_CHUNK
    inner = SSM_INNER
    xbc_lo, xbc_hi = inner, inner + SSM_CONV_DIM

    @pl.when(ci == 0)
    def _():
        h_ref[...] = jnp.zeros_like(h_ref)

    z = p_ref[0, :, 0:inner]
    ext_ref[0:SSM_HALO, :] = jnp.where(ci > 0, halo_ref[0, :, xbc_lo:xbc_hi], 0.0)
    ext_ref[SSM_HALO:, :] = p_ref[0, :, xbc_lo:xbc_hi]
    conv = cb_ref[...]
    for j in range(SSM_CONV):
        conv = conv + cw_ref[j:j + 1, :] * ext_ref[pl.ds(SSM_HALO - (SSM_CONV - 1) + j, q), :]
    xbc = _silu(conv)
    xs = xbc[:, 0:inner]
    dt = _softplus(p_ref[0, :, xbc_hi:xbc_hi + LANES] + dtb_ref[...])
    a = dt * ah_ref[...]
    row = lax.broadcasted_iota(jnp.int32, (q, q), 0)
    col = lax.broadcasted_iota(jnp.int32, (q, q), 1)
    causal = row >= col
    a_cum = _dot_hi(causal.astype(F32), a)
    a_cum_t = a_cum.T
    dt_full = _dot_hi(dt, e64_ref[...])
    acum_full = _dot_hi(a_cum, e64_ref[...])
    alast_full = acum_full[q - 1:q, :]
    acum_b = _dot_hi(a_cum, e128_ref[...])
    xdt = xs * dt_full
    x_to_end = xdt * jnp.exp(alast_full - acum_full)
    exp_ac = jnp.exp(acum_full)
    chunk_dec = jnp.exp(alast_full)
    lane = lax.broadcasted_iota(jnp.int32, (q, LANES), 1)
    first_head = lane < SSM_HEAD
    ys = []
    for g in range(SSM_GROUPS):
        bm = xbc[:, inner + g * SSM_STATE:inner + (g + 1) * SSM_STATE]
        cm = xbc[:, inner + (SSM_GROUPS + g) * SSM_STATE:inner + (SSM_GROUPS + g + 1) * SSM_STATE].astype(BF16)
        bt = bm.T.astype(BF16)
        cb = _dot(cm, bt)
        for pr in range(SSM_HEADS // SSM_GROUPS // 2):
            pi = g * (SSM_HEADS // SSM_GROUPS // 2) + pr
            cols = slice(pi * LANES, (pi + 1) * LANES)
            mats = []
            for hd in (2 * pi, 2 * pi + 1):
                seg = acum_b[:, hd * LANES:(hd + 1) * LANES] - a_cum_t[hd:hd + 1, :]
                mats.append((cb * jnp.exp(jnp.where(causal, seg, NEG))).astype(BF16))
            xp = xdt[:, cols]
            x_blockdiag = jnp.concatenate([jnp.where(first_head, xp, 0.0), jnp.where(first_head, 0.0, xp)], axis=0)
            y_diag = _dot(jnp.concatenate(mats, axis=1), x_blockdiag.astype(BF16))
            h_t = h_ref[pi]
            y_off = _dot(cm, h_t.astype(BF16)) * exp_ac[:, cols]
            h_ref[pi] = h_t * chunk_dec[:, cols] + _dot(bt, x_to_end[:, cols].astype(BF16))
            ys.append(y_diag + y_off)
    y = (jnp.concatenate(ys, axis=-1) + xs * dsk_ref[...]) * _silu(z)
    gsize = inner // SSM_GROUPS
    outs = []
    for g in range(SSM_GROUPS):
        yg = y[:, g * gsize:(g + 1) * gsize]
        outs.append(yg * lax.rsqrt(jnp.mean(yg * yg, axis=-1, keepdims=True) + NORM_EPS))
    o_ref[0] = (jnp.concatenate(outs, axis=-1) * nw_ref[...]).astype(o_ref.dtype)


def _mamba_mixer(pd, conv_w, conv_b, dt_bias, a_log, d_skip, norm_w):
    bsz, seq, dp = pd.shape
    q = SSM_CHUNK
    pad_heads = lambda t: jnp.zeros((1, LANES), F32).at[0, :SSM_HEADS].set(t)
    head_of = lambda width: jnp.arange(SSM_HEADS * width, dtype=jnp.int32)[None, :] // width
    expand = lambda width: (jnp.arange(LANES, dtype=jnp.int32)[:, None] == head_of(width)).astype(F32)
    full2 = lambda shape: pl.BlockSpec(shape, lambda b, i: (0, 0))
    out = pl.pallas_call(
        _ssd_kernel, grid=(bsz, seq // q),
        in_specs=[pl.BlockSpec((1, q, dp), lambda b, i: (b, i, 0)),
                  pl.BlockSpec((1, SSM_HALO, dp), lambda b, i: (b, jnp.maximum(i * (q // SSM_HALO) - 1, 0), 0)),
                  full2((SSM_CONV, SSM_CONV_DIM)), full2((1, SSM_CONV_DIM)), full2((1, LANES)), full2((1, LANES)),
                  full2((1, SSM_INNER)), full2((1, SSM_INNER)),
                  full2((LANES, SSM_HEADS * SSM_HEAD)), full2((LANES, SSM_HEADS * LANES))],
        out_specs=pl.BlockSpec((1, q, SSM_INNER), lambda b, i: (b, i, 0)),
        out_shape=SDS((bsz, seq, SSM_INNER), BF16),
        scratch_shapes=[pltpu.VMEM((q + SSM_HALO, SSM_CONV_DIM), F32),
                        pltpu.VMEM((SSM_HEADS // 2, SSM_STATE, LANES), F32)],
        compiler_params=_cp("parallel", "arbitrary"), name="ssd",
    )(pd, pd, conv_w, conv_b.reshape(1, -1), pad_heads(dt_bias), pad_heads(-jnp.exp(a_log)),
      jnp.repeat(d_skip, SSM_HEAD).reshape(1, -1), norm_w.reshape(1, -1), expand(SSM_HEAD), expand(LANES))
    return out.reshape(bsz * seq, SSM_INNER)


def _merge_kernel(pg_ref, ya_ref, ga_ref, bb_ref, bc_ref, bd_ref, gu_ref, gb_ref, wb_ref, o_ref):
    pg = pg_ref[...]
    branches = ((ya_ref[...] * ga_ref[...]).astype(BF16), bb_ref[...], bc_ref[...], bd_ref[...])
    acc = None
    for bi, br in enumerate(branches):
        term = _sigmoid(_dot(pg, gu_ref[bi]) + gb_ref[bi]) * _dot(br, wb_ref[bi])
        acc = term if acc is None else acc + term
    o_ref[...] = acc.astype(o_ref.dtype)


def _merge(pg, ya, ga, bb, bc, bd, gate_up, gate_b, w_branch, tm=1024, tn=512):
    m = pg.shape[0]
    nb, kw, d = w_branch.shape
    rows = lambda k: pl.BlockSpec((tm, k), lambda i, j: (i, 0))
    return pl.pallas_call(
        _merge_kernel, grid=(m // tm, d // tn),
        in_specs=[rows(gate_up.shape[1]), rows(kw), rows(kw), rows(kw), rows(kw), rows(kw),
                  pl.BlockSpec((nb, gate_up.shape[1], tn), lambda i, j: (0, 0, j)),
                  pl.BlockSpec((nb, 1, tn), lambda i, j: (0, 0, j)),
                  pl.BlockSpec((nb, kw, tn), lambda i, j: (0, 0, j))],
        out_specs=pl.BlockSpec((tm, tn), lambda i, j: (i, j)),
        out_shape=SDS((m, d), BF16), compiler_params=_cp("parallel", "parallel"), name="merge",
    )(pg, ya, ga, bb, bc, bd, gate_up, gate_b.reshape(nb, 1, d), w_branch)


SUBLANES = 8


def _store_slabs(ref, rows):
    w = ref.shape[-1]
    for j in range(SUBLANES):
        ref[:, j, :] = rows[:, j * w:(j + 1) * w]


def _norm_route_kernel(x_ref, g_ref, sc_ref, sh_ref, wr_ref, br_ref, h_ref, ids_ref, wts_ref, hist_ref):
    x = x_ref[0]
    ms = jnp.mean(x * x, axis=-1, keepdims=True)
    h = x * lax.rsqrt(ms + NORM_EPS) * g_ref[...] * (1.0 + sc_ref[0]) + sh_ref[0]
    _store_slabs(h_ref, h)
    lg = lax.dot_general(h, wr_ref[...], (((1,), (1,)), ((), ())), precision=HI, preferred_element_type=F32) + br_ref[...]
    lane = lax.broadcasted_iota(jnp.int32, lg.shape, 1)
    lane_f = lane.astype(F32)
    first = lambda hit: jnp.min(jnp.where(hit, lane_f, float(LANES)), axis=-1, keepdims=True)
    gmask = lane < MOE_GROUPS
    gl = jnp.where(gmask, lg, NEG)
    gmax = jnp.max(gl, axis=-1, keepdims=True)
    gsel = first(gl == gmax)
    gprob = 1.0 / jnp.sum(jnp.where(gmask, jnp.exp(gl - gmax), 0.0), axis=-1, keepdims=True)
    lo = MOE_GROUPS + gsel * MOE_EXPERTS_PER_GROUP
    emask = (lane_f >= lo) & (lane_f < lo + MOE_EXPERTS_PER_GROUP)
    el = jnp.where(emask, lg, NEG)
    v1 = jnp.max(el, axis=-1, keepdims=True)
    i1 = first(el == v1)
    el2 = jnp.where(lane_f == i1, NEG, el)
    v2 = jnp.max(el2, axis=-1, keepdims=True)
    i2 = first((el2 == v2) & emask & (lane_f != i1))
    t = jnp.exp(v2 - v1)
    w1 = gprob / (1.0 + t)
    w2 = gprob * t / (1.0 + t)
    ids_ref[0] = jnp.where(lane == 0, i1, jnp.where(lane == 1, i2, float(MOE_GROUPS))).astype(jnp.int32) - MOE_GROUPS
    wts_ref[0] = jnp.where(lane == 0, w1, jnp.where(lane == 1, w2, 0.0))
    chosen = jnp.where((lane_f == i1) | (lane_f == i2), 1.0, 0.0)
    hist_ref[...] = jnp.broadcast_to(jnp.sum(chosen, axis=0, keepdims=True), hist_ref.shape)


def _norm_route(x, g, scale, shift, rg_wt, rg_b, re_wt, re_b, tm=256):
    bsz, seq, d = x.shape
    n_log = MOE_GROUPS + MOE_EXPERTS
    wr = jnp.zeros((LANES, d), F32).at[:n_log].set(jnp.concatenate([rg_wt, re_wt], axis=0))
    br = jnp.zeros((1, LANES), F32).at[0, :n_log].set(jnp.concatenate([rg_b, re_b]))
    tok = lambda width: pl.BlockSpec((1, tm, width), lambda b, i: (b, i, 0))
    m, dw, nt = bsz * seq, d // SUBLANES, seq // tm
    h, ids, wts, hist = pl.pallas_call(
        _norm_route_kernel, grid=(bsz, nt),
        in_specs=[tok(d), pl.BlockSpec((1, d), lambda b, i: (0, 0)),
                  pl.BlockSpec((1, 1, d), lambda b, i: (b, 0, 0)), pl.BlockSpec((1, 1, d), lambda b, i: (b, 0, 0)),
                  pl.BlockSpec((LANES, d), lambda b, i: (0, 0)), pl.BlockSpec((1, LANES), lambda b, i: (0, 0))],
        out_specs=[pl.BlockSpec((tm, SUBLANES, dw), lambda b, i: (b * nt + i, 0, 0)), tok(LANES), tok(LANES),
                   pl.BlockSpec((SUBLANES, LANES), lambda b, i: (b * nt + i, 0))],
        out_shape=[SDS((m, SUBLANES, dw), F32), SDS((bsz, seq, LANES), jnp.int32), SDS((bsz, seq, LANES), F32),
                   SDS((bsz * nt * SUBLANES, LANES), F32)],
        compiler_params=_cp("parallel", "parallel"), name="norm_route",
    )(x, g.reshape(1, d), scale.reshape(bsz, 1, d), shift.reshape(bsz, 1, d), wr, br)
    counts = jnp.sum(hist[::SUBLANES, MOE_GROUPS:n_log], axis=0).astype(jnp.int32)
    return h, ids.reshape(m, LANES)[:, :MOE_TOP_K], wts.reshape(m, LANES)[:, :MOE_TOP_K], counts


ROW_COPY_UNROLL = 8


FFN_W_SLOTS = 4
FFN_W_CHUNKS = 4


def _ffn_kernel(layer, be_ref, nb_ref, tok_ref, dst_ref, cnt_ref, h_hbm, wg_hbm, wu_hbm, wd_hbm, sw_ref, y_hbm,
                xbuf, ybuf, xrow, wg_s, wu_s, wd_s, stage_in, stage_out, gsem, ssem, wsem):
    j = pl.program_id(0)
    nb = nb_ref[0]
    cur = j % 2

    slots = stage_in.shape[0]

    def weight_steps(e):
        d, ff = wg_s.shape
        rin, rout = d // FFN_W_CHUNKS, ff // FFN_W_CHUNKS
        steps = []
        for src, dst, stage, rows in ((wg_hbm, wg_s, stage_in, rin), (wu_hbm, wu_s, stage_in, rin),
                                      (wd_hbm, wd_s, stage_out, rout)):
            for c in range(FFN_W_CHUNKS):
                i = len(steps)
                copy = pltpu.make_async_copy(src.at[layer, e, pl.ds(c * rows, rows)], stage.at[i % slots],
                                             wsem.at[i % slots])
                steps.append((copy, stage, dst.at[pl.ds(c * rows, rows)]))
        return steps

    def prefetch_expert(e):
        for copy, _, _ in weight_steps(e)[:slots - 1]:
            copy.start()

    def load_expert(e):
        steps = weight_steps(e)
        for i, (copy, stage, dst) in enumerate(steps):
            if i + slots - 1 < len(steps):
                steps[i + slots - 1][0].start()
            copy.wait()
            dst[...] = stage[i % slots].astype(BF16)

    def gather_copy(buf, s, tok):
        return pltpu.make_async_copy(h_hbm.at[pl.ds(tok, 1)], xbuf.at[buf, pl.ds(s, 1)], gsem.at[buf])

    def scatter_copy(buf, s, dst):
        return pltpu.make_async_copy(ybuf.at[buf, pl.ds(s, 1)], y_hbm.at[pl.ds(dst, 1)], ssem.at[buf])

    def start_gather(blk, buf):
        def body(s, carry):
            gather_copy(buf, s, tok_ref[blk * MOE_ROWS + s]).start()
            return carry
        lax.fori_loop(0, MOE_ROWS, body, 0, unroll=ROW_COPY_UNROLL)

    def wait_gather(buf):
        pltpu.make_async_copy(h_hbm.at[pl.ds(0, MOE_ROWS)], xbuf.at[buf], gsem.at[buf]).wait()

    def start_scatter(blk, buf):
        def body(s, carry):
            scatter_copy(buf, s, dst_ref[blk * MOE_ROWS + s]).start()
            return carry
        full = cnt_ref[blk] == MOE_ROWS

        @pl.when(full)
        def _():
            lax.fori_loop(0, MOE_ROWS, body, 0, unroll=ROW_COPY_UNROLL)

        @pl.when(jnp.logical_not(full))
        def _():
            lax.fori_loop(0, cnt_ref[blk], body, 0)

    def wait_scatter(blk, buf):
        n = cnt_ref[blk]

        @pl.when(n > 0)
        def _():
            pltpu.make_async_copy(ybuf.at[buf, pl.ds(0, n)], y_hbm.at[pl.ds(0, n)], ssem.at[buf]).wait()

    @pl.when(j == 0)
    def _():
        start_gather(0, 0)

    @pl.when(j + 1 < nb)
    def _():
        start_gather(j + 1, 1 - cur)

    @pl.when(j < nb)
    def _():
        e = be_ref[j]

        @pl.when(j == 0)
        def _():
            prefetch_expert(e)

        @pl.when((j == 0) | (e != be_ref[jnp.maximum(j - 1, 0)]))
        def _():
            load_expert(e)

        e_next = be_ref[jnp.minimum(j + 1, nb - 1)]

        @pl.when((j + 1 < nb) & (e_next != e))
        def _():
            prefetch_expert(e_next)

        wait_gather(cur)

        @pl.when(j >= 2)
        def _():
            wait_scatter(j - 2, cur)

        dw = xbuf.shape[-1]
        for c in range(SUBLANES):
            xrow[:, c * dw:(c + 1) * dw] = xbuf[cur, :, c, :]
        x = xrow[...].astype(BF16)
        hidden = (_silu(_dot(x, wg_s[...])) * _dot(x, wu_s[...])).astype(BF16)
        _store_slabs(ybuf.at[cur], _dot(hidden, wd_s[...]) * sw_ref[...])
        start_scatter(j, cur)

    @pl.when(j == nb - 1)
    def _():
        @pl.when(j >= 1)
        def _():
            wait_scatter(j - 1, 1 - cur)
        wait_scatter(j, cur)


def _moe_plan(expert, weight, counts):
    n_tok = expert.shape[0]
    n_assign = n_tok * MOE_TOP_K
    flat_e = expert.reshape(-1)
    flat_w = weight.reshape(-1)
    order = jnp.argsort(flat_e).astype(jnp.int32)
    padded = (counts + MOE_ROWS - 1) // MOE_ROWS * MOE_ROWS
    pad_end = jnp.cumsum(padded)
    pad_start = pad_end - padded
    start = jnp.cumsum(counts) - counts
    n_blocks = n_assign // MOE_ROWS + MOE_EXPERTS
    n_slots = n_blocks * MOE_ROWS
    blk0 = jnp.arange(n_blocks, dtype=jnp.int32) * MOE_ROWS
    block_e = jnp.minimum(jnp.searchsorted(pad_end, blk0, side='right'), MOE_EXPERTS - 1).astype(jnp.int32)
    block_cnt = jnp.clip(counts[block_e] - (blk0 - pad_start[block_e]), 0, MOE_ROWS).astype(jnp.int32)
    slot_e = jnp.repeat(block_e, MOE_ROWS)
    pos = jnp.arange(n_slots, dtype=jnp.int32) - pad_start[slot_e]
    valid = pos < counts[slot_e]
    assign = order[jnp.clip(start[slot_e] + pos, 0, n_assign - 1)]
    tok, k = assign // MOE_TOP_K, assign % MOE_TOP_K
    slot_tok = jnp.where(valid, tok, 0).astype(jnp.int32)
    slot_w = jnp.where(valid, flat_w[assign], 0.0)
    slot_dst = jnp.where(valid, k * n_tok + tok, 0).astype(jnp.int32)
    n_used_blocks = (pad_end[-1:] // MOE_ROWS).astype(jnp.int32)
    return slot_tok, slot_w, block_e, slot_dst, block_cnt, n_used_blocks


def _cast_kernel(x_ref, o_ref):
    o_ref[...] = x_ref[0].astype(o_ref.dtype)


def _cast_bf16(w, layer, block_bytes=2 * 1024 * 1024):
    _, e, r, c = w.shape
    tr = min(r, block_bytes // (4 * c))
    return pl.pallas_call(
        _cast_kernel, grid=(e, r // tr),
        in_specs=[pl.BlockSpec((1, 1, tr, c), lambda i, j: (layer, i, j, 0))],
        out_specs=pl.BlockSpec((1, tr, c), lambda i, j: (i, j, 0)),
        out_shape=SDS((e, r, c), BF16), compiler_params=_cp("parallel", "parallel"), name="cast_bf16")(w)


def _moe_apply(h, expert, weight, counts, w_gate, w_up, w_down, layer):
    n_tok, _, dw = h.shape
    d = dw * SUBLANES
    ff = w_gate.shape[-1]
    slot_tok, slot_w, block_e, slot_dst, block_cnt, n_used_blocks = _moe_plan(expert, weight, counts)
    n_slots = slot_tok.shape[0]
    any_space = pl.BlockSpec(memory_space=pl.ANY)
    return pl.pallas_call(
        functools.partial(_ffn_kernel, layer),
        grid_spec=pltpu.PrefetchScalarGridSpec(
            num_scalar_prefetch=5, grid=(n_slots // MOE_ROWS,),
            in_specs=[any_space, any_space, any_space, any_space,
                      pl.BlockSpec((MOE_ROWS, 1), lambda j, be, nb, *_: (jnp.minimum(j, nb[0] - 1), 0))],
            out_specs=any_space,
            scratch_shapes=[pltpu.VMEM((2, MOE_ROWS, SUBLANES, dw), F32), pltpu.VMEM((2, MOE_ROWS, SUBLANES, dw), F32),
                            pltpu.VMEM((MOE_ROWS, d), F32),
                            pltpu.VMEM((d, ff), BF16), pltpu.VMEM((d, ff), BF16), pltpu.VMEM((ff, d), BF16),
                            pltpu.VMEM((FFN_W_SLOTS, d // FFN_W_CHUNKS, ff), F32),
                            pltpu.VMEM((FFN_W_SLOTS, ff // FFN_W_CHUNKS, d), F32),
                            pltpu.SemaphoreType.DMA((2,)), pltpu.SemaphoreType.DMA((2,)),
                            pltpu.SemaphoreType.DMA((FFN_W_SLOTS,))]),
        out_shape=SDS((MOE_TOP_K * n_tok, SUBLANES, dw), F32), compiler_params=_cp("arbitrary"), name="moe_ffn",
    )(block_e, n_used_blocks, slot_tok, slot_dst, block_cnt, h, w_gate, w_up, w_down, slot_w.reshape(n_slots, 1))


def _moe_combine_kernel(x_ref, y0_ref, y1_ref, g_ref, o_ref):
    dw = y0_ref.shape[-1]
    for j in range(SUBLANES):
        cols = slice(j * dw, (j + 1) * dw)
        o_ref[0, :, cols] = x_ref[0, :, cols] + g_ref[0, :, cols] * (y0_ref[0, 0, :, j, :] + y1_ref[0, 0, :, j, :])


def _moe_combine(x, y2, gate, tm=256):
    bsz, seq, d = x.shape
    dw = d // SUBLANES
    y5 = y2.reshape(MOE_TOP_K, bsz, seq, SUBLANES, dw)
    tok = pl.BlockSpec((1, tm, d), lambda b, i: (b, i, 0))
    slab = lambda k: pl.BlockSpec((1, 1, tm, SUBLANES, dw), lambda b, i: (k, b, i, 0, 0))
    return pl.pallas_call(
        _moe_combine_kernel, grid=(bsz, seq // tm),
        in_specs=[tok, slab(0), slab(1), pl.BlockSpec((1, 1, d), lambda b, i: (b, 0, 0))],
        out_specs=tok, out_shape=SDS((bsz, seq, d), F32),
        compiler_params=_cp("parallel", "parallel"), name="moe_combine",
    )(x, y5, y5, gate.reshape(bsz, 1, d))


def _pack_plan(offs, n_cols):
    ng, wh = len(ATT_GROUPS), ATT_HEADS * ATT_HEAD
    plan, starts = [], []

    def dense(first, count):
        starts.append(len(plan))
        for c0 in range(0, count, PACK_COLS):
            plan.append([(first + c0, min(PACK_COLS, count - c0), 0)])

    dense(offs[0], A_MAIN)
    o_w = offs[0] + A_MAIN
    o_a = o_w + RWKV_DECAY_RANK
    o_g = o_a + RWKV_ICLR_RANK
    plan.append([(o_w, RWKV_DECAY_RANK, 0), (o_a, RWKV_ICLR_RANK, 128), (o_g, RWKV_GATE_RANK, 256)])
    for gi in range(ng):
        starts.append(len(plan))
        for t in range(3):
            for c0 in range(0, wh, PACK_COLS):
                plan.append([(offs[1] + (t * ng + gi) * wh + c0, PACK_COLS, 0)])
    dense(offs[2], offs[3] - offs[2])
    dense(offs[3], offs[4] - offs[3])
    dense(offs[4], n_cols - offs[4])
    return plan, starts


def _pack_w_in_kernel(layer, plan, w_hbm, o_ref, slab, sem):
    step = pl.program_id(0)

    def copies(j):
        return [pltpu.make_async_copy(w_hbm.at[layer, pl.ds(src, n)], slab.at[j % 2, pl.ds(dst, n)], sem.at[j % 2])
                for src, n, dst in plan[j]]

    def fetch(j):
        covered = sorted((dst, dst + n) for _, n, dst in plan[j])
        edge = 0
        for lo, hi in covered + [(PACK_COLS, PACK_COLS)]:
            if lo > edge:
                slab[j % 2, edge:lo, :] = jnp.zeros((lo - edge, slab.shape[2]), F32)
            edge = hi
        for cp in copies(j):
            cp.start()

    for j in range(len(plan)):
        @pl.when(step == j)
        def _():
            if j == 0:
                fetch(0)
            if j + 1 < len(plan):
                fetch(j + 1)
            for cp in copies(j):
                cp.wait()

    o_ref[...] = slab[step % 2].T.astype(BF16)


def _pack_w_in(w, layer, offs):
    _, d, n = w.shape
    plan, starts = _pack_plan(offs, n)
    packed = pl.pallas_call(
        functools.partial(_pack_w_in_kernel, layer, plan), grid=(len(plan),),
        in_specs=[pl.BlockSpec(memory_space=pl.ANY)],
        out_specs=pl.BlockSpec((d, PACK_COLS), lambda j: (0, j)),
        out_shape=SDS((d, len(plan) * PACK_COLS), BF16),
        scratch_shapes=[pltpu.VMEM((2, PACK_COLS, d), F32), pltpu.SemaphoreType.DMA((2,))],
        compiler_params=_cp("arbitrary"), name="pack_w_in",
    )(jnp.swapaxes(w, 1, 2))
    return packed, starts + [len(plan)]


def kernel(x, c, ada_w, ada_b, ada_table, norm1_g, norm2_g, w_in, rwkv_mu, rwkv_w0, rwkv_w2, rwkv_a0, rwkv_a2,
           rwkv_g2, rwkv_kk, rwkv_ka, rwkv_rk, rwkv_lnx_w, rwkv_lnx_b, pool_w, pool_scale, ssm_conv_w, ssm_conv_b,
           ssm_dt_bias, ssm_a_log, ssm_d, ssm_norm_w, gate_up, gate_b, w_branch, w_out, router_group_w,
           router_group_b, router_expert_w, router_expert_b, exp_w_gate, exp_w_up, exp_w_down, final_g):
    bsz, seq, d = x.shape
    m = bsz * seq
    a_cols = rwkv_mu.shape[1]
    b_cols = 3 * len(ATT_GROUPS) * ATT_HEADS * ATT_HEAD
    d_cols = SSM_INNER + SSM_CONV_DIM + SSM_HEADS
    offs = (0, a_cols, a_cols + b_cols, a_cols + b_cols + BRANCH_WIDTH, a_cols + b_cols + BRANCH_WIDTH + d_cols)
    cond = _cond(c, ada_w, ada_b)
    for l in range(DEPTH):
        shift1, scale1, gate1, shift2, scale2, gate2 = jnp.split(cond + ada_table[l], 6, axis=-1)
        dilations = tuple(dil for _, dil in ATT_GROUPS)
        h, *h_phases = _norm_mod_phases(x, norm1_g[l], scale1, shift1, [dil for dil in dilations if dil > 1])
        h = h.reshape(m, d)
        h_of = {1: h, **{dil: hp.reshape(m, d) for dil, hp in zip([dil for dil in dilations if dil > 1], h_phases)}}
        w_packed, blk = _pack_w_in(w_in, l, offs)
        proj = lambda a, seg, dtype: _matmul(a, w_packed, dtype, blk[seg], blk[seg + 1] - blk[seg], tn=PACK_COLS)
        pa = proj(h, 0, F32).reshape(bsz, seq, A_PAD)
        pbs = [proj(h_of[dil], 1 + gi, BF16).reshape(bsz, dil, seq // dil, -1) for gi, dil in enumerate(dilations)]
        pc = proj(h, 4, F32).reshape(bsz, seq, BRANCH_WIDTH)
        pd = proj(h, 5, F32).reshape(bsz, seq, D_PAD)
        pg = proj(h, 6, BF16)
        ya, ga = _rwkv_mixer(pa, rwkv_mu[l], rwkv_w0[l], rwkv_w2[l], rwkv_a0[l], rwkv_a2[l], rwkv_g2[l],
                             rwkv_kk[l], rwkv_ka[l], rwkv_rk[l], rwkv_lnx_w[l], rwkv_lnx_b[l])
        ob = _attention_mixer(pbs)
        oc = _pool_mixer(pc, pool_w[l], pool_scale[l])
        od = _mamba_mixer(pd, ssm_conv_w[l], ssm_conv_b[l], ssm_dt_bias[l], ssm_a_log[l], ssm_d[l], ssm_norm_w[l])
        merged = _merge(pg, ya.reshape(m, BRANCH_WIDTH), ga.reshape(m, BRANCH_WIDTH), ob, oc, od,
                        _cast_bf16(gate_up, l), gate_b[l], _cast_bf16(w_branch, l))
        w_out_l = _cast_bf16(w_out.reshape(DEPTH, 1, d, d), l).reshape(d, d)
        x = _matmul_resid(merged, w_out_l, x.reshape(m, d), gate1, seq).reshape(bsz, seq, d)
        h2, expert, weight, counts = _norm_route(x, norm2_g[l], scale2, shift2,
                                                 jnp.swapaxes(router_group_w, 1, 2)[l], router_group_b[l],
                                                 jnp.swapaxes(router_expert_w, 1, 2)[l], router_expert_b[l])
        y2 = _moe_apply(h2, expert, weight, counts, exp_w_gate, exp_w_up, exp_w_down, l)
        x = _moe_combine(x, y2, gate2)
    zeros = jnp.zeros((bsz, d), F32)
    return _norm_mod(x, final_g, zeros, zeros, F32)
```

```python
import functools

import jax
import jax.numpy as jnp
from jax import lax
from jax.experimental import pallas as pl
from jax.experimental.pallas import tpu as pltpu

F32 = jnp.float32
BF16 = jnp.bfloat16
HI = lax.Precision.HIGHEST
SDS = jax.ShapeDtypeStruct

DEPTH = 2
NORM_EPS = 1e-6
BRANCH_WIDTH = 1024
RWKV_HEAD = 64
RWKV_HEADS = 16
RWKV_DECAY_RANK = 64
RWKV_ICLR_RANK = 64
RWKV_GATE_RANK = 160
RWKV_GN_EPS = 64e-5
RWKV_CHUNK = 64
ATT_HEAD = 128
ATT_GROUPS = ((128, 1), (512, 4), (2048, 16))
ATT_HEADS = 8
ATT_BLOCK = 128
POOL_WINDOWS = (2, 4, 8, 16)
POOL_GROUP = 256
POOL_HALO = 16
SSM_INNER = 1024
SSM_HEAD = 64
SSM_HEADS = 16
SSM_GROUPS = 4
SSM_STATE = 128
SSM_CONV = 4
SSM_CHUNK = 128
SSM_CONV_DIM = SSM_INNER + 2 * SSM_GROUPS * SSM_STATE
GATE_RANK = 256
MOE_GROUPS = 4
MOE_EXPERTS_PER_GROUP = 8
MOE_EXPERTS = 32
MOE_TOP_K = 2
MOE_FF = 512
MOE_ROWS = 256
LANES = 128
NEG = -1e30
A_MAIN = 3 * BRANCH_WIDTH
LOW_RANK_PAD = LANES
GATE_RANK_PAD = 2 * LANES
A_PAD = A_MAIN + 2 * LOW_RANK_PAD + GATE_RANK_PAD
PACK_COLS = 512
D_PAD = -(-(SSM_INNER + SSM_CONV_DIM + SSM_HEADS) // PACK_COLS) * PACK_COLS
VMEM_LIMIT = 56 * 1024 * 1024


def _cp(*sem):
    return pltpu.CompilerParams(dimension_semantics=sem, vmem_limit_bytes=VMEM_LIMIT)


def _sigmoid(x):
    return 1.0 / (1.0 + jnp.exp(-x))


def _silu(x):
    return x * _sigmoid(x)


def _softplus(x):
    return jnp.maximum(x, 0.0) + jnp.log(1.0 + jnp.exp(-jnp.abs(x)))


def _dot(a, b):
    return jnp.dot(a, b, preferred_element_type=F32)


def _dot_hi(a, b):
    return jnp.dot(a, b, precision=HI, preferred_element_type=F32)


def _cond_kernel(c_ref, w_ref, b_ref, o_ref, a_ref):
    @pl.when(pl.program_id(0) == 0)
    def _():
        a_ref[...] = _silu(c_ref[...])

    k, tn = w_ref.shape
    w = w_ref[...]
    rows = []
    for r in range(a_ref.shape[0]):
        prod = w * jnp.concatenate([a_ref[r]] * (tn // LANES), axis=1)
        part = jnp.sum(prod.reshape(k // 8, 8, tn), axis=0)
        rows.append(jnp.sum(part, axis=0, keepdims=True))
    rows.append(jnp.zeros((o_ref.shape[0] - len(rows), tn), F32))
    o_ref[...] = jnp.concatenate(rows, axis=0) + b_ref[...]


def _cond(c, ada_w, ada_b):
    bsz, d = c.shape
    n = ada_w.shape[1]
    tn = 512
    c_lanes = jnp.broadcast_to(c[:, :, None], (bsz, d, LANES))
    out = pl.pallas_call(
        _cond_kernel, grid=(n // tn,),
        in_specs=[pl.BlockSpec((bsz, d, LANES), lambda j: (0, 0, 0)),
                  pl.BlockSpec((d, tn), lambda j: (0, j)),
                  pl.BlockSpec((1, tn), lambda j: (0, j))],
        out_specs=pl.BlockSpec((8, tn), lambda j: (0, j)),
        out_shape=SDS((8, n), F32), scratch_shapes=[pltpu.VMEM((bsz, d, LANES), F32)],
        compiler_params=_cp("arbitrary"), name="cond",
    )(c_lanes, ada_w, ada_b.reshape(1, n))
    return out[:bsz]


def _norm_mod_phases_kernel(x_ref, g_ref, sc_ref, sh_ref, o_ref, *refs):
    phase_refs, scr = refs[:-1], refs[-1]
    x = x_ref[0]
    tm, d = x.shape
    ms = jnp.mean(x * x, axis=-1, keepdims=True)
    y = x * lax.rsqrt(ms + NORM_EPS) * g_ref[...] * (1.0 + sc_ref[0]) + sh_ref[0]
    o_ref[0] = y.astype(o_ref.dtype)
    for ref in phase_refs:
        dil = ref.shape[1]
        n = tm // dil
        y3 = y.reshape(n, dil, d)
        for ph in range(dil):
            scr[ph * n:(ph + 1) * n, :] = y3[:, ph, :]
        for ph in range(dil):
            ref[0, ph] = scr[ph * n:(ph + 1) * n, :].astype(ref.dtype)


def _norm_mod_phases(x, g, scale, shift, dilations, tm=256):
    bsz, seq, d = x.shape
    tok = pl.BlockSpec((1, tm, d), lambda b, i: (b, i, 0))
    vec = pl.BlockSpec((1, 1, d), lambda b, i: (b, 0, 0))
    return pl.pallas_call(
        _norm_mod_phases_kernel, grid=(bsz, seq // tm),
        in_specs=[tok, pl.BlockSpec((1, d), lambda b, i: (0, 0)), vec, vec],
        out_specs=[tok] + [pl.BlockSpec((1, dil, tm // dil, d), lambda b, i: (b, 0, i, 0)) for dil in dilations],
        out_shape=[SDS((bsz, seq, d), BF16)] + [SDS((bsz, dil, seq // dil, d), BF16) for dil in dilations],
        scratch_shapes=[pltpu.VMEM((tm, d), F32)],
        compiler_params=_cp("parallel", "parallel"), name="norm_mod_phases",
    )(x, g.reshape(1, d), scale.reshape(bsz, 1, d), shift.reshape(bsz, 1, d))


def _mm_kernel(a_ref, b_ref, o_ref):
    o_ref[...] = _dot(a_ref[...], b_ref[...]).astype(o_ref.dtype)


def _matmul(a, b, out_dtype, first_block=0, n_blocks=None, tm=1024, tn=512):
    m, k = a.shape
    n = b.shape[1] if n_blocks is None else n_blocks * tn
    return pl.pallas_call(
        _mm_kernel, grid=(m // tm, n // tn),
        in_specs=[pl.BlockSpec((tm, k), lambda i, j: (i, 0)),
                  pl.BlockSpec((k, tn), lambda i, j: (0, first_block + j))],
        out_specs=pl.BlockSpec((tm, tn), lambda i, j: (i, j)),
        out_shape=SDS((m, n), out_dtype), compiler_params=_cp("parallel", "parallel"), name="matmul",
    )(a, b)


def _mm_resid_kernel(a_ref, b_ref, r_ref, g_ref, o_ref):
    o_ref[...] = r_ref[...] + g_ref[0] * _dot(a_ref[...], b_ref[...])


def _matmul_resid(a, b, resid, gate, seq, tm=1024, tn=512):
    m, k = a.shape
    n = b.shape[1]
    bsz = gate.shape[0]
    return pl.pallas_call(
        _mm_resid_kernel, grid=(m // tm, n // tn),
        in_specs=[pl.BlockSpec((tm, k), lambda i, j: (i, 0)),
                  pl.BlockSpec((k, tn), lambda i, j: (0, j)),
                  pl.BlockSpec((tm, tn), lambda i, j: (i, j)),
                  pl.BlockSpec((1, 1, tn), lambda i, j: (i * tm // seq, 0, j))],
        out_specs=pl.BlockSpec((tm, tn), lambda i, j: (i, j)),
        out_shape=SDS((m, n), F32), compiler_params=_cp("parallel", "parallel"), name="matmul_resid",
    )(a, b, resid, gate.reshape(bsz, 1, n))


def _rwkv_prep_kernel(p_ref, halo_ref, mu_ref, w0_ref, a0_ref, kkw_ref, kaw_ref, w2_ref, a2_ref, g2_ref,
                      r_ref, k_ref, v_ref, kk_ref, a_ref, lw_ref, g_ref):
    x = p_ref[0]
    w = BRANCH_WIDTH
    prev_first = jnp.where(pl.program_id(1) > 0, halo_ref[0][7:8, :], 0.0)
    row = lax.broadcasted_iota(jnp.int32, x.shape, 0)
    prev = jnp.where(row == 0, prev_first, pltpu.roll(x, 1, axis=0))
    p = x + (prev - x) * mu_ref[...]
    r, k, v = p[:, 0:w], p[:, w:2 * w], p[:, 2 * w:3 * w]
    o_a, o_g = A_MAIN + LOW_RANK_PAD, A_MAIN + 2 * LOW_RANK_PAD
    xw, xa, xg = p[:, A_MAIN:o_a], p[:, o_a:o_g], p[:, o_g:A_PAD]
    wlog = -_softplus(-(w0_ref[...] + _dot_hi(jnp.tanh(xw), w2_ref[...]))) - 0.5
    a = _sigmoid(a0_ref[...] + _dot_hi(xa, a2_ref[...]))
    r_ref[0] = r
    k_ref[0] = k * (1.0 + (a - 1.0) * kaw_ref[...])
    v_ref[0] = v
    kk_ref[0] = k * kkw_ref[...]
    a_ref[0] = a
    lw_ref[0] = -jnp.exp(wlog)
    g_ref[0] = _dot_hi(_sigmoid(xg), g2_ref[...])


def _rwkv_prep(pa, mu, w0, a0, kkw, kaw, w2, a2, g2, tm=256):
    bsz, seq, ap = pa.shape
    w = BRANCH_WIDTH
    row = lambda t: t.reshape(1, -1)
    full = lambda shape: pl.BlockSpec(shape, lambda b, i: (0, 0))
    out_spec = pl.BlockSpec((1, tm, w), lambda b, i: (b, i, 0))
    return pl.pallas_call(
        _rwkv_prep_kernel, grid=(bsz, seq // tm),
        in_specs=[pl.BlockSpec((1, tm, ap), lambda b, i: (b, i, 0)),
                  pl.BlockSpec((1, 8, ap), lambda b, i: (b, jnp.maximum(i * (tm // 8) - 1, 0), 0)),
                  full((1, ap)), full((1, w)), full((1, w)), full((1, w)), full((1, w)),
                  full((LOW_RANK_PAD, w)), full((LOW_RANK_PAD, w)), full((GATE_RANK_PAD, w))],
        out_specs=[out_spec] * 7,
        out_shape=[SDS((bsz, seq, w), F32)] * 7,
        compiler_params=_cp("parallel", "parallel"), name="rwkv_prep",
    )(pa, pa, row(mu), row(w0), row(a0), row(kkw), row(kaw), w2, a2, g2)


def _rwkv_rec_kernel(r_ref, k_ref, v_ref, kk_ref, a_ref, lw_ref, rk_ref, lnw_ref, lnb_ref, o_ref, s_ref):
    @pl.when(pl.program_id(1) == 0)
    def _():
        s_ref[...] = jnp.zeros_like(s_ref)

    nh, n = s_ref.shape[0], s_ref.shape[1]
    heads = lambda ref: jnp.stack([ref[0, :, h * n:(h + 1) * n] for h in range(nh)])
    r, k, v, kk, a, lw = (heads(ref) for ref in (r_ref, k_ref, v_ref, kk_ref, a_ref, lw_ref))
    c = r.shape[1]
    kn = kk / jnp.maximum(jnp.sqrt(jnp.sum(kk * kk, axis=-1, keepdims=True)), 1e-12)
    b = kn * a
    row = lax.broadcasted_iota(jnp.int32, (c, c), 0)
    col = lax.broadcasted_iota(jnp.int32, (c, c), 1)
    strict, incl = (row > col)[None], (row >= col)[None]
    bdot = lambda spec: (lambda x, y: jnp.einsum(spec, x.astype(BF16), y.astype(BF16), preferred_element_type=F32))
    nt, nn, tn = bdot('hik,hjk->hij'), bdot('hij,hjk->hik'), bdot('hiv,hik->hvk')
    tril = jnp.broadcast_to((row >= col).astype(BF16)[None], (nh, c, c))
    lw_hi = lw.astype(BF16)
    lw_r1 = lw - lw_hi.astype(F32)
    lw_mid = lw_r1.astype(BF16)
    lw_lo = (lw_r1 - lw_mid.astype(F32)).astype(BF16)
    lc = nn(tril, lw_hi) + (nn(tril, lw_mid) + nn(tril, lw_lo))
    lc_last = lc[:, c - 1:c, :]
    e_neg, e_end = jnp.exp(-lc), jnp.exp(lc_last - lc)
    kt, rt = kn * jnp.exp(lc - lw), r * jnp.exp(lc)
    bt, kd = b * e_neg, k * e_neg
    kr = jnp.concatenate([kt, rt], axis=1)
    sc = nt(kr, jnp.concatenate([bt, kd], axis=1))
    lm = jnp.where(strict, -sc[:, :c, :c], 0.0)
    ak = jnp.where(strict, sc[:, :c, c:], 0.0)
    bb = jnp.where(incl, sc[:, c:, :c], 0.0)
    bk = jnp.where(incl, sc[:, c:, c:], 0.0)
    s0 = s_ref[...]
    kr_s0 = nt(kr, s0)
    abk_v = nn(jnp.concatenate([ak, bk], axis=1), v)
    u = -(kr_s0[:, :c] + abk_v[:, :c])
    pw = lm
    n_doublings = c.bit_length() - 1
    for it in range(n_doublings):
        u = u + nn(pw, u)
        if it + 1 < n_doublings:
            pw = nn(pw, pw)
    y = kr_s0[:, c:] + nn(bb, u) + abk_v[:, c:]
    s_ref[...] = s0 * jnp.exp(lc_last) + tn(jnp.concatenate([u, v], axis=1),
                                            jnp.concatenate([b * e_end, k * e_end], axis=1))
    mean = jnp.mean(y, axis=-1, keepdims=True)
    var = jnp.mean(jnp.square(y - mean), axis=-1, keepdims=True)
    out = (y - mean) * lax.rsqrt(var + RWKV_GN_EPS) * lnw_ref[...] + lnb_ref[...]
    out = out + jnp.sum(r * k * rk_ref[...], axis=-1, keepdims=True) * v
    for h in range(nh):
        o_ref[0, :, h * n:(h + 1) * n] = out[h]


def _rwkv_rec(r, k, v, kk, a, lw, rk, lnw, lnb, chunk=RWKV_CHUNK):
    bsz, seq, w = r.shape
    nh, n = rk.shape
    blk = pl.BlockSpec((1, chunk, w), lambda b, i: (b, i, 0))
    par = pl.BlockSpec((nh, 1, n), lambda b, i: (0, 0, 0))
    return pl.pallas_call(
        _rwkv_rec_kernel, grid=(bsz, seq // chunk),
        in_specs=[blk] * 6 + [par] * 3, out_specs=blk,
        out_shape=SDS((bsz, seq, w), F32),
        scratch_shapes=[pltpu.VMEM((nh, n, n), F32)],
        compiler_params=_cp("parallel", "arbitrary"), name="rwkv_rec",
    )(r, k, v, kk, a, lw, rk.reshape(nh, 1, n), lnw.reshape(nh, 1, n), lnb.reshape(nh, 1, n))


def _pad_rows(w, rows):
    return jnp.zeros((rows, w.shape[1]), w.dtype).at[:w.shape[0]].set(w)


def _rwkv_pack_cols(t):
    lead = t.shape[:-1]
    z = lambda n: jnp.zeros(lead + (n,), t.dtype)
    o1 = A_MAIN + RWKV_DECAY_RANK
    o2 = o1 + RWKV_ICLR_RANK
    return jnp.concatenate([t[..., :A_MAIN], t[..., A_MAIN:o1], z(LOW_RANK_PAD - RWKV_DECAY_RANK), t[..., o1:o2],
                            z(LOW_RANK_PAD - RWKV_ICLR_RANK), t[..., o2:], z(GATE_RANK_PAD - RWKV_GATE_RANK)], axis=-1)


def _rwkv_mixer(pa, mu, w0, w2, a0, a2, g2, kkw, kaw, rk, lnw, lnb):
    bsz, seq, _ = pa.shape
    r, k, v, kk, a, lw, g = _rwkv_prep(pa, _rwkv_pack_cols(mu), w0, a0, kkw, kaw,
                                       _pad_rows(w2, LOW_RANK_PAD), _pad_rows(a2, LOW_RANK_PAD),
                                       _pad_rows(g2, GATE_RANK_PAD))
    y = _rwkv_rec(r, k, v, kk, a, lw, rk, lnw.reshape(RWKV_HEADS, RWKV_HEAD), lnb.reshape(RWKV_HEADS, RWKV_HEAD))
    return y, g


def _att_kernel(q_ref, kc_ref, kp_ref, vc_ref, vp_ref, o_ref, lse_ref):
    nt = lambda x, y: lax.dot_general(x, y, (((1,), (1,)), ((), ())), preferred_element_type=F32)
    scale = ATT_HEAD ** -0.5
    qi = lax.broadcasted_iota(jnp.int32, (ATT_BLOCK, ATT_BLOCK), 0)
    ki = lax.broadcasted_iota(jnp.int32, (ATT_BLOCK, ATT_BLOCK), 1)
    cur_ok = ki <= qi
    prev_ok = (ki >= qi) & (pl.program_id(2) > 0)
    lane = lax.broadcasted_iota(jnp.int32, (ATT_BLOCK, LANES), 1)
    lse_all = jnp.zeros((ATT_BLOCK, LANES), F32)
    for h in range(ATT_HEADS):
        cols = slice(h * ATT_HEAD, (h + 1) * ATT_HEAD)
        q = q_ref[0, 0, :, cols]
        s_c = jnp.where(cur_ok, nt(q, kc_ref[0, 0, :, cols]) * scale, NEG)
        s_p = jnp.where(prev_ok, nt(q, kp_ref[0, 0, :, cols]) * scale, NEG)
        m = jnp.maximum(jnp.max(s_c, axis=-1, keepdims=True), jnp.max(s_p, axis=-1, keepdims=True))
        e_c, e_p = jnp.exp(s_c - m), jnp.exp(s_p - m)
        den = jnp.sum(e_c, axis=-1, keepdims=True) + jnp.sum(e_p, axis=-1, keepdims=True)
        o = _dot(e_c.astype(BF16), vc_ref[0, 0, :, cols]) + _dot(e_p.astype(BF16), vp_ref[0, 0, :, cols])
        o_ref[0, 0, :, cols] = o / den
        lse_all = jnp.where(lane == h, m + jnp.log(den), lse_all)
    lse_ref[0, 0] = lse_all


def _att_group(pb, gi):
    bsz, dilation, n_phase, _ = pb.shape
    w = ATT_HEADS * ATT_HEAD
    nblk = n_phase // ATT_BLOCK
    blk = (1, 1, ATT_BLOCK, w)
    cur = lambda t: pl.BlockSpec(blk, lambda b, ph, i: (b, ph, i, t))
    prev = lambda t: pl.BlockSpec(blk, lambda b, ph, i: (b, ph, jnp.maximum(i - 1, 0), t))
    return pl.pallas_call(
        _att_kernel, grid=(bsz, dilation, nblk),
        in_specs=[cur(0), cur(1), prev(1), cur(2), prev(2)],
        out_specs=[pl.BlockSpec(blk, lambda b, ph, i: (b, ph, i, 0)),
                   pl.BlockSpec((1, 1, ATT_BLOCK, LANES), lambda b, ph, i: (b, ph, i, 0))],
        out_shape=[SDS((bsz, dilation, n_phase, w), F32), SDS((bsz, dilation, n_phase, LANES), F32)],
        compiler_params=_cp("parallel", "parallel", "arbitrary"), name=f"att_g{gi}",
    )(pb, pb, pb, pb, pb)


def _att_combine_kernel(o0, o1, o2, l0, l1, l2, e_ref, out_ref):
    tm = out_ref.shape[0]

    def tokens(ref):
        d = ref.shape[1]
        if d == 1:
            return ref[0, 0]
        return jnp.stack([ref[0, ph] for ph in range(d)], axis=1).reshape(tm, ref.shape[-1])

    lses = [tokens(l) for l in (l0, l1, l2)]
    m = jnp.maximum(jnp.maximum(lses[0], lses[1]), lses[2])
    es = [jnp.exp(l - m) for l in lses]
    inv = 1.0 / (es[0] + es[1] + es[2])
    acc = None
    for e, o in zip(es, (o0, o1, o2)):
        term = _dot_hi(e * inv, e_ref[...]) * tokens(o)
        acc = term if acc is None else acc + term
    out_ref[...] = acc.astype(out_ref.dtype)


def _attention_mixer(pbs, tm=256):
    res = [_att_group(pb, gi) for gi, pb in enumerate(pbs)]
    bsz, _, seq, w = res[0][0].shape
    nt = seq // tm
    spec = lambda t: pl.BlockSpec((1, t.shape[1], tm // t.shape[1], t.shape[3]), lambda b, i: (b, 0, i, 0))
    expand = (jnp.arange(LANES, dtype=jnp.int32)[:, None] == jnp.arange(w, dtype=jnp.int32)[None, :] // ATT_HEAD).astype(F32)
    args = [t[0] for t in res] + [t[1] for t in res]
    return pl.pallas_call(
        _att_combine_kernel, grid=(bsz, nt),
        in_specs=[spec(t) for t in args] + [pl.BlockSpec((LANES, w), lambda b, i: (0, 0))],
        out_specs=pl.BlockSpec((tm, w), lambda b, i: (b * nt + i, 0)),
        out_shape=SDS((bsz * seq, w), BF16), compiler_params=_cp("parallel", "parallel"), name="att_combine",
    )(*args, expand)


def _pool_kernel(x_ref, halo_ref, w_ref, sc_ref, o_ref, ext_ref):
    tm = x_ref.shape[1]
    i = pl.program_id(1)
    x = x_ref[0]
    ext_ref[0:POOL_HALO, :] = jnp.where(i > 0, halo_ref[0], 0.0)
    ext_ref[POOL_HALO:, :] = x
    pos = i * tm + lax.broadcasted_iota(jnp.int32, (tm, POOL_GROUP), 0)
    outs = []
    for gi, win in enumerate(POOL_WINDOWS):
        cols = slice(gi * POOL_GROUP, (gi + 1) * POOL_GROUP)
        xg = x[:, cols]
        s = xg
        for j in range(1, win):
            s = s + ext_ref[pl.ds(POOL_HALO - j, tm), cols]
        mixed = s / jnp.minimum(pos + 1, win).astype(F32) - xg
        outs.append(_dot(mixed.astype(BF16), w_ref[gi]))
    o_ref[0] = (jnp.concatenate(outs, axis=-1) * sc_ref[...]).astype(o_ref.dtype)


def _pool_mixer(pc, w_pool, scale, tm=256):
    bsz, seq, w = pc.shape
    out = pl.pallas_call(
        _pool_kernel, grid=(bsz, seq // tm),
        in_specs=[pl.BlockSpec((1, tm, w), lambda b, i: (b, i, 0)),
                  pl.BlockSpec((1, POOL_HALO, w), lambda b, i: (b, jnp.maximum(i * (tm // POOL_HALO) - 1, 0), 0)),
                  pl.BlockSpec(w_pool.shape, lambda b, i: (0, 0, 0)),
                  pl.BlockSpec((1, w), lambda b, i: (0, 0))],
        out_specs=pl.BlockSpec((1, tm, w), lambda b, i: (b, i, 0)),
        out_shape=SDS((bsz, seq, w), BF16),
        scratch_shapes=[pltpu.VMEM((tm + POOL_HALO, w), F32)],
        compiler_params=_cp("parallel", "parallel"), name="pool",
    )(pc, pc, w_pool.astype(BF16), scale.reshape(1, w))
    return out.reshape(bsz * seq, w)


SSM_HALO = 8


def _ssd_kernel(p_ref, halo_ref, cw_ref, cb_ref, dtb_ref, ah_ref, dsk_ref, nw_ref, e64_ref, e128_ref,
                o_ref, ext_ref, h_ref):
    ci = pl.program_id(1)
    q = SSM_CHUNK
    inner = SSM_INNER
    xbc_lo, xbc_hi = inner, inner + SSM_CONV_DIM

    @pl.when(ci == 0)
    def _():
        h_ref[...] = jnp.zeros_like(h_ref)

    z = p_ref[0, :, 0:inner]
    ext_ref[0:SSM_HALO, :] = jnp.where(ci > 0, halo_ref[0, :, xbc_lo:xbc_hi], 0.0)
    ext_ref[SSM_HALO:, :] = p_ref[0, :, xbc_lo:xbc_hi]
    conv = cb_ref[...]
    for j in range(SSM_CONV):
        conv = conv + cw_ref[j:j + 1, :] * ext_ref[pl.ds(SSM_HALO - (SSM_CONV - 1) + j, q), :]
    xbc = _silu(conv)
    xs = xbc[:, 0:inner]
    dt = _softplus(p_ref[0, :, xbc_hi:xbc_hi + LANES] + dtb_ref[...])
    a = dt * ah_ref[...]
    row = lax.broadcasted_iota(jnp.int32, (q, q), 0)
    col = lax.broadcasted_iota(jnp.int32, (q, q), 1)
    causal = row >= col
    a_cum = _dot_hi(causal.astype(F32), a)
    a_cum_t = a_cum.T
    dt_full = _dot_hi(dt, e64_ref[...])
    acum_full = _dot_hi(a_cum, e64_ref[...])
    alast_full = acum_full[q - 1:q, :]
    acum_b = _dot_hi(a_cum, e128_ref[...])
    xdt = xs * dt_full
    x_to_end = xdt * jnp.exp(alast_full - acum_full)
    exp_ac = jnp.exp(acum_full)
    chunk_dec = jnp.exp(alast_full)
    lane = lax.broadcasted_iota(jnp.int32, (q, LANES), 1)
    first_head = lane < SSM_HEAD
    ys = []
    for g in range(SSM_GROUPS):
        bm = xbc[:, inner + g * SSM_STATE:inner + (g + 1) * SSM_STATE]
        cm = xbc[:, inner + (SSM_GROUPS + g) * SSM_STATE:inner + (SSM_GROUPS + g + 1) * SSM_STATE].astype(BF16)
        bt = bm.T.astype(BF16)
        cb = _dot(cm, bt)
        for pr in range(SSM_HEADS // SSM_GROUPS // 2):
            pi = g * (SSM_HEADS // SSM_GROUPS // 2) + pr
            cols = slice(pi * LANES, (pi + 1) * LANES)
            mats = []
            for hd in (2 * pi, 2 * pi + 1):
                seg = acum_b[:, hd * LANES:(hd + 1) * LANES] - a_cum_t[hd:hd + 1, :]
                mats.append((cb * jnp.exp(jnp.where(causal, seg, NEG))).astype(BF16))
            xp = xdt[:, cols]
            x_blockdiag = jnp.concatenate([jnp.where(first_head, xp, 0.0), jnp.where(first_head, 0.0, xp)], axis=0)
            y_diag = _dot(jnp.concatenate(mats, axis=1), x_blockdiag.astype(BF16))
            h_t = h_ref[pi]
            y_off = _dot(cm, h_t.astype(BF16)) * exp_ac[:, cols]
            h_ref[pi] = h_t * chunk_dec[:, cols] + _dot(bt, x_to_end[:, cols].astype(BF16))
            ys.append(y_diag + y_off)
    y = (jnp.concatenate(ys, axis=-1) + xs * dsk_ref[...]) * _silu(z)
    gsize = inner // SSM_GROUPS
    outs = []
    for g in range(SSM_GROUPS):
        yg = y[:, g * gsize:(g + 1) * gsize]
        outs.append(yg * lax.rsqrt(jnp.mean(yg * yg, axis=-1, keepdims=True) + NORM_EPS))
    o_ref[0] = (jnp.concatenate(outs, axis=-1) * nw_ref[...]).astype(o_ref.dtype)


def _mamba_mixer(pd, conv_w, conv_b, dt_bias, a_log, d_skip, norm_w):
    bsz, seq, dp = pd.shape
    q = SSM_CHUNK
    pad_heads = lambda t: jnp.zeros((1, LANES), F32).at[0, :SSM_HEADS].set(t)
    head_of = lambda width: jnp.arange(SSM_HEADS * width, dtype=jnp.int32)[None, :] // width
    expand = lambda width: (jnp.arange(LANES, dtype=jnp.int32)[:, None] == head_of(width)).astype(F32)
    full2 = lambda shape: pl.BlockSpec(shape, lambda b, i: (0, 0))
    out = pl.pallas_call(
        _ssd_kernel, grid=(bsz, seq // q),
        in_specs=[pl.BlockSpec((1, q, dp), lambda b, i: (b, i, 0)),
                  pl.BlockSpec((1, SSM_HALO, dp), lambda b, i: (b, jnp.maximum(i * (q // SSM_HALO) - 1, 0), 0)),
                  full2((SSM_CONV, SSM_CONV_DIM)), full2((1, SSM_CONV_DIM)), full2((1, LANES)), full2((1, LANES)),
                  full2((1, SSM_INNER)), full2((1, SSM_INNER)),
                  full2((LANES, SSM_HEADS * SSM_HEAD)), full2((LANES, SSM_HEADS * LANES))],
        out_specs=pl.BlockSpec((1, q, SSM_INNER), lambda b, i: (b, i, 0)),
        out_shape=SDS((bsz, seq, SSM_INNER), BF16),
        scratch_shapes=[pltpu.VMEM((q + SSM_HALO, SSM_CONV_DIM), F32),
                        pltpu.VMEM((SSM_HEADS // 2, SSM_STATE, LANES), F32)],
        compiler_params=_cp("parallel", "arbitrary"), name="ssd",
    )(pd, pd, conv_w, conv_b.reshape(1, -1), pad_heads(dt_bias), pad_heads(-jnp.exp(a_log)),
      jnp.repeat(d_skip, SSM_HEAD).reshape(1, -1), norm_w.reshape(1, -1), expand(SSM_HEAD), expand(LANES))
    return out.reshape(bsz * seq, SSM_INNER)


def _merge_kernel(pg_ref, ya_ref, ga_ref, bb_ref, bc_ref, bd_ref, gu_ref, gb_ref, wb_ref, o_ref):
    pg = pg_ref[...]
    branches = ((ya_ref[...] * ga_ref[...]).astype(BF16), bb_ref[...], bc_ref[...], bd_ref[...])
    acc = None
    for bi, br in enumerate(branches):
        term = _sigmoid(_dot(pg, gu_ref[bi]) + gb_ref[bi]) * _dot(br, wb_ref[bi])
        acc = term if acc is None else acc + term
    o_ref[...] = acc.astype(o_ref.dtype)


def _merge(pg, ya, ga, bb, bc, bd, gate_up, gate_b, w_branch, tm=1024, tn=512):
    m = pg.shape[0]
    nb, kw, d = w_branch.shape
    rows = lambda k: pl.BlockSpec((tm, k), lambda i, j: (i, 0))
    return pl.pallas_call(
        _merge_kernel, grid=(m // tm, d // tn),
        in_specs=[rows(gate_up.shape[1]), rows(kw), rows(kw), rows(kw), rows(kw), rows(kw),
                  pl.BlockSpec((nb, gate_up.shape[1], tn), lambda i, j: (0, 0, j)),
                  pl.BlockSpec((nb, 1, tn), lambda i, j: (0, 0, j)),
                  pl.BlockSpec((nb, kw, tn), lambda i, j: (0, 0, j))],
        out_specs=pl.BlockSpec((tm, tn), lambda i, j: (i, j)),
        out_shape=SDS((m, d), BF16), compiler_params=_cp("parallel", "parallel"), name="merge",
    )(pg, ya, ga, bb, bc, bd, gate_up, gate_b.reshape(nb, 1, d), w_branch)


SUBLANES = 8


def _store_slabs(ref, rows):
    w = ref.shape[-1]
    for j in range(SUBLANES):
        ref[:, j, :] = rows[:, j * w:(j + 1) * w]


def _norm_route_kernel(x_ref, g_ref, sc_ref, sh_ref, wr_ref, br_ref, h_ref, ids_ref, wts_ref, hist_ref):
    x = x_ref[0]
    ms = jnp.mean(x * x, axis=-1, keepdims=True)
    h = x * lax.rsqrt(ms + NORM_EPS) * g_ref[...] * (1.0 + sc_ref[0]) + sh_ref[0]
    _store_slabs(h_ref, h)
    lg = lax.dot_general(h, wr_ref[...], (((1,), (1,)), ((), ())), precision=HI, preferred_element_type=F32) + br_ref[...]
    lane = lax.broadcasted_iota(jnp.int32, lg.shape, 1)
    lane_f = lane.astype(F32)
    first = lambda hit: jnp.min(jnp.where(hit, lane_f, float(LANES)), axis=-1, keepdims=True)
    gmask = lane < MOE_GROUPS
    gl = jnp.where(gmask, lg, NEG)
    gmax = jnp.max(gl, axis=-1, keepdims=True)
    gsel = first(gl == gmax)
    gprob = 1.0 / jnp.sum(jnp.where(gmask, jnp.exp(gl - gmax), 0.0), axis=-1, keepdims=True)
    lo = MOE_GROUPS + gsel * MOE_EXPERTS_PER_GROUP
    emask = (lane_f >= lo) & (lane_f < lo + MOE_EXPERTS_PER_GROUP)
    el = jnp.where(emask, lg, NEG)
    v1 = jnp.max(el, axis=-1, keepdims=True)
    i1 = first(el == v1)
    el2 = jnp.where(lane_f == i1, NEG, el)
    v2 = jnp.max(el2, axis=-1, keepdims=True)
    i2 = first((el2 == v2) & emask & (lane_f != i1))
    t = jnp.exp(v2 - v1)
    w1 = gprob / (1.0 + t)
    w2 = gprob * t / (1.0 + t)
    ids_ref[0] = jnp.where(lane == 0, i1, jnp.where(lane == 1, i2, float(MOE_GROUPS))).astype(jnp.int32) - MOE_GROUPS
    wts_ref[0] = jnp.where(lane == 0, w1, jnp.where(lane == 1, w2, 0.0))
    chosen = jnp.where((lane_f == i1) | (lane_f == i2), 1.0, 0.0)
    hist_ref[...] = jnp.broadcast_to(jnp.sum(chosen, axis=0, keepdims=True), hist_ref.shape)


def _norm_route(x, g, scale, shift, rg_wt, rg_b, re_wt, re_b, tm=256):
    bsz, seq, d = x.shape
    n_log = MOE_GROUPS + MOE_EXPERTS
    wr = jnp.zeros((LANES, d), F32).at[:n_log].set(jnp.concatenate([rg_wt, re_wt], axis=0))
    br = jnp.zeros((1, LANES), F32).at[0, :n_log].set(jnp.concatenate([rg_b, re_b]))
    tok = lambda width: pl.BlockSpec((1, tm, width), lambda b, i: (b, i, 0))
    m, dw, nt = bsz * seq, d // SUBLANES, seq // tm
    h, ids, wts, hist = pl.pallas_call(
        _norm_route_kernel, grid=(bsz, nt),
        in_specs=[tok(d), pl.BlockSpec((1, d), lambda b, i: (0, 0)),
                  pl.BlockSpec((1, 1, d), lambda b, i: (b, 0, 0)), pl.BlockSpec((1, 1, d), lambda b, i: (b, 0, 0)),
                  pl.BlockSpec((LANES, d), lambda b, i: (0, 0)), pl.BlockSpec((1, LANES), lambda b, i: (0, 0))],
        out_specs=[pl.BlockSpec((tm, SUBLANES, dw), lambda b, i: (b * nt + i, 0, 0)), tok(LANES), tok(LANES),
                   pl.BlockSpec((SUBLANES, LANES), lambda b, i: (b * nt + i, 0))],
        out_shape=[SDS((m, SUBLANES, dw), F32), SDS((bsz, seq, LANES), jnp.int32), SDS((bsz, seq, LANES), F32),
                   SDS((bsz * nt * SUBLANES, LANES), F32)],
        compiler_params=_cp("parallel", "parallel"), name="norm_route",
    )(x, g.reshape(1, d), scale.reshape(bsz, 1, d), shift.reshape(bsz, 1, d), wr, br)
    counts = jnp.sum(hist[::SUBLANES, MOE_GROUPS:n_log], axis=0).astype(jnp.int32)
    return h, ids.reshape(m, LANES)[:, :MOE_TOP_K], wts.reshape(m, LANES)[:, :MOE_TOP_K], counts


ROW_COPY_UNROLL = 8


FFN_W_SLOTS = 4
FFN_W_CHUNKS = 4


def _ffn_kernel(layer, be_ref, nb_ref, tok_ref, dst_ref, cnt_ref, h_hbm, wg_hbm, wu_hbm, wd_hbm, sw_ref, y_hbm,
                xbuf, ybuf, xrow, wg_s, wu_s, wd_s, stage_in, stage_out, gsem, ssem, wsem):
    j = pl.program_id(0)
    nb = nb_ref[0]
    cur = j % 2

    slots = stage_in.shape[0]

    def weight_steps(e):
        d, ff = wg_s.shape
        rin, rout = d // FFN_W_CHUNKS, ff // FFN_W_CHUNKS
        steps = []
        for src, dst, stage, rows in ((wg_hbm, wg_s, stage_in, rin), (wu_hbm, wu_s, stage_in, rin),
                                      (wd_hbm, wd_s, stage_out, rout)):
            for c in range(FFN_W_CHUNKS):
                i = len(steps)
                copy = pltpu.make_async_copy(src.at[layer, e, pl.ds(c * rows, rows)], stage.at[i % slots],
                                             wsem.at[i % slots])
                steps.append((copy, stage, dst.at[pl.ds(c * rows, rows)]))
        return steps

    def prefetch_expert(e):
        for copy, _, _ in weight_steps(e)[:slots - 1]:
            copy.start()

    def load_expert(e):
        steps = weight_steps(e)
        for i, (copy, stage, dst) in enumerate(steps):
            if i + slots - 1 < len(steps):
                steps[i + slots - 1][0].start()
            copy.wait()
            dst[...] = stage[i % slots].astype(BF16)

    def gather_copy(buf, s, tok):
        return pltpu.make_async_copy(h_hbm.at[pl.ds(tok, 1)], xbuf.at[buf, pl.ds(s, 1)], gsem.at[buf])

    def scatter_copy(buf, s, dst):
        return pltpu.make_async_copy(ybuf.at[buf, pl.ds(s, 1)], y_hbm.at[pl.ds(dst, 1)], ssem.at[buf])

    def start_gather(blk, buf):
        def body(s, carry):
            gather_copy(buf, s, tok_ref[blk * MOE_ROWS + s]).start()
            return carry
        lax.fori_loop(0, MOE_ROWS, body, 0, unroll=ROW_COPY_UNROLL)

    def wait_gather(buf):
        pltpu.make_async_copy(h_hbm.at[pl.ds(0, MOE_ROWS)], xbuf.at[buf], gsem.at[buf]).wait()

    def start_scatter(blk, buf):
        def body(s, carry):
            scatter_copy(buf, s, dst_ref[blk * MOE_ROWS + s]).start()
            return carry
        full = cnt_ref[blk] == MOE_ROWS

        @pl.when(full)
        def _():
            lax.fori_loop(0, MOE_ROWS, body, 0, unroll=ROW_COPY_UNROLL)

        @pl.when(jnp.logical_not(full))
        def _():
            lax.fori_loop(0, cnt_ref[blk], body, 0)

    def wait_scatter(blk, buf):
        n = cnt_ref[blk]

        @pl.when(n > 0)
        def _():
            pltpu.make_async_copy(ybuf.at[buf, pl.ds(0, n)], y_hbm.at[pl.ds(0, n)], ssem.at[buf]).wait()

    @pl.when(j == 0)
    def _():
        start_gather(0, 0)

    @pl.when(j + 1 < nb)
    def _():
        start_gather(j + 1, 1 - cur)

    @pl.when(j < nb)
    def _():
        e = be_ref[j]

        @pl.when(j == 0)
        def _():
            prefetch_expert(e)

        @pl.when((j == 0) | (e != be_ref[jnp.maximum(j - 1, 0)]))
        def _():
            load_expert(e)

        e_next = be_ref[jnp.minimum(j + 1, nb - 1)]

        @pl.when((j + 1 < nb) & (e_next != e))
        def _():
            prefetch_expert(e_next)

        wait_gather(cur)

        @pl.when(j >= 2)
        def _():
            wait_scatter(j - 2, cur)

        dw = xbuf.shape[-1]
        for c in range(SUBLANES):
            xrow[:, c * dw:(c + 1) * dw] = xbuf[cur, :, c, :]
        x = xrow[...].astype(BF16)
        hidden = (_silu(_dot(x, wg_s[...])) * _dot(x, wu_s[...])).astype(BF16)
        _store_slabs(ybuf.at[cur], _dot(hidden, wd_s[...]) * sw_ref[...])
        start_scatter(j, cur)

    @pl.when(j == nb - 1)
    def _():
        @pl.when(j >= 1)
        def _():
            wait_scatter(j - 1, 1 - cur)
        wait_scatter(j, cur)


def _moe_plan(expert, weight, counts):
    n_tok = expert.shape[0]
    n_assign = n_tok * MOE_TOP_K
    flat_e = expert.reshape(-1)
    flat_w = weight.reshape(-1)
    order = jnp.argsort(flat_e).astype(jnp.int32)
    padded = (counts + MOE_ROWS - 1) // MOE_ROWS * MOE_ROWS
    pad_end = jnp.cumsum(padded)
    pad_start = pad_end - padded
    start = jnp.cumsum(counts) - counts
    n_blocks = n_assign // MOE_ROWS + MOE_EXPERTS
    n_slots = n_blocks * MOE_ROWS
    blk0 = jnp.arange(n_blocks, dtype=jnp.int32) * MOE_ROWS
    block_e = jnp.minimum(jnp.searchsorted(pad_end, blk0, side='right'), MOE_EXPERTS - 1).astype(jnp.int32)
    block_cnt = jnp.clip(counts[block_e] - (blk0 - pad_start[block_e]), 0, MOE_ROWS).astype(jnp.int32)
    slot_e = jnp.repeat(block_e, MOE_ROWS)
    pos = jnp.arange(n_slots, dtype=jnp.int32) - pad_start[slot_e]
    valid = pos < counts[slot_e]
    assign = order[jnp.clip(start[slot_e] + pos, 0, n_assign - 1)]
    tok, k = assign // MOE_TOP_K, assign % MOE_TOP_K
    slot_tok = jnp.where(valid, tok, 0).astype(jnp.int32)
    slot_w = jnp.where(valid, flat_w[assign], 0.0)
    slot_dst = jnp.where(valid, k * n_tok + tok, 0).astype(jnp.int32)
    n_used_blocks = (pad_end[-1:] // MOE_ROWS).astype(jnp.int32)
    return slot_tok, slot_w, block_e, slot_dst, block_cnt, n_used_blocks


def _cast_kernel(x_ref, o_ref):
    o_ref[...] = x_ref[0].astype(o_ref.dtype)


def _cast_bf16(w, layer, block_bytes=2 * 1024 * 1024):
    _, e, r, c = w.shape
    tr = min(r, block_bytes // (4 * c))
    return pl.pallas_call(
        _cast_kernel, grid=(e, r // tr),
        in_specs=[pl.BlockSpec((1, 1, tr, c), lambda i, j: (layer, i, j, 0))],
        out_specs=pl.BlockSpec((1, tr, c), lambda i, j: (i, j, 0)),
        out_shape=SDS((e, r, c), BF16), compiler_params=_cp("parallel", "parallel"), name="cast_bf16")(w)


def _moe_apply(h, expert, weight, counts, w_gate, w_up, w_down, layer):
    n_tok, _, dw = h.shape
    d = dw * SUBLANES
    ff = w_gate.shape[-1]
    slot_tok, slot_w, block_e, slot_dst, block_cnt, n_used_blocks = _moe_plan(expert, weight, counts)
    n_slots = slot_tok.shape[0]
    any_space = pl.BlockSpec(memory_space=pl.ANY)
    return pl.pallas_call(
        functools.partial(_ffn_kernel, layer),
        grid_spec=pltpu.PrefetchScalarGridSpec(
            num_scalar_prefetch=5, grid=(n_slots // MOE_ROWS,),
            in_specs=[any_space, any_space, any_space, any_space,
                      pl.BlockSpec((MOE_ROWS, 1), lambda j, be, nb, *_: (jnp.minimum(j, nb[0] - 1), 0))],
            out_specs=any_space,
            scratch_shapes=[pltpu.VMEM((2, MOE_ROWS, SUBLANES, dw), F32), pltpu.VMEM((2, MOE_ROWS, SUBLANES, dw), F32),
                            pltpu.VMEM((MOE_ROWS, d), F32),
                            pltpu.VMEM((d, ff), BF16), pltpu.VMEM((d, ff), BF16), pltpu.VMEM((ff, d), BF16),
                            pltpu.VMEM((FFN_W_SLOTS, d // FFN_W_CHUNKS, ff), F32),
                            pltpu.VMEM((FFN_W_SLOTS, ff // FFN_W_CHUNKS, d), F32),
                            pltpu.SemaphoreType.DMA((2,)), pltpu.SemaphoreType.DMA((2,)),
                            pltpu.SemaphoreType.DMA((FFN_W_SLOTS,))]),
        out_shape=SDS((MOE_TOP_K * n_tok, SUBLANES, dw), F32), compiler_params=_cp("arbitrary"), name="moe_ffn",
    )(block_e, n_used_blocks, slot_tok, slot_dst, block_cnt, h, w_gate, w_up, w_down, slot_w.reshape(n_slots, 1))


def _moe_combine_kernel(x_ref, y0_ref, y1_ref, g_ref, *rest):
    o_ref = rest[-1]
    dw = y0_ref.shape[-1]
    for j in range(SUBLANES):
        cols = slice(j * dw, (j + 1) * dw)
        o_ref[0, :, cols] = x_ref[0, :, cols] + g_ref[0, :, cols] * (y0_ref[0, 0, :, j, :] + y1_ref[0, 0, :, j, :])
    if len(rest) == 2:
        x = o_ref[0]
        o_ref[0] = x * lax.rsqrt(jnp.mean(x * x, axis=-1, keepdims=True) + NORM_EPS) * rest[0][...]


def _moe_combine(x, y2, gate, norm_g=None, tm=256):
    bsz, seq, d = x.shape
    dw = d // SUBLANES
    y5 = y2.reshape(MOE_TOP_K, bsz, seq, SUBLANES, dw)
    tok = pl.BlockSpec((1, tm, d), lambda b, i: (b, i, 0))
    slab = lambda k: pl.BlockSpec((1, 1, tm, SUBLANES, dw), lambda b, i: (k, b, i, 0, 0))
    extra_specs = [] if norm_g is None else [pl.BlockSpec((1, d), lambda b, i: (0, 0))]
    extra_args = [] if norm_g is None else [norm_g.reshape(1, d)]
    return pl.pallas_call(
        _moe_combine_kernel, grid=(bsz, seq // tm),
        in_specs=[tok, slab(0), slab(1), pl.BlockSpec((1, 1, d), lambda b, i: (b, 0, 0))] + extra_specs,
        out_specs=tok, out_shape=SDS((bsz, seq, d), F32),
        compiler_params=_cp("parallel", "parallel"), name="moe_combine",
    )(x, y5, y5, gate.reshape(bsz, 1, d), *extra_args)


def _pack_plan(offs, n_cols):
    ng, wh = len(ATT_GROUPS), ATT_HEADS * ATT_HEAD
    plan, starts = [], []

    def dense(first, count):
        starts.append(len(plan))
        for c0 in range(0, count, PACK_COLS):
            plan.append([(first + c0, min(PACK_COLS, count - c0), 0)])

    dense(offs[0], A_MAIN)
    o_w = offs[0] + A_MAIN
    o_a = o_w + RWKV_DECAY_RANK
    o_g = o_a + RWKV_ICLR_RANK
    plan.append([(o_w, RWKV_DECAY_RANK, 0), (o_a, RWKV_ICLR_RANK, LOW_RANK_PAD), (o_g, RWKV_GATE_RANK, 2 * LOW_RANK_PAD)])
    for gi in range(ng):
        starts.append(len(plan))
        for t in range(3):
            for c0 in range(0, wh, PACK_COLS):
                plan.append([(offs[1] + (t * ng + gi) * wh + c0, PACK_COLS, 0)])
    dense(offs[2], offs[3] - offs[2])
    dense(offs[3], offs[4] - offs[3])
    dense(offs[4], n_cols - offs[4])
    return plan, starts


def _pack_w_in_kernel(layer, plan, w_hbm, o_ref, slab, sem):
    step = pl.program_id(0)

    def copies(j):
        return [pltpu.make_async_copy(w_hbm.at[layer, pl.ds(src, n)], slab.at[j % 2, pl.ds(dst, n)], sem.at[j % 2])
                for src, n, dst in plan[j]]

    def fetch(j):
        covered = sorted((dst, dst + n) for _, n, dst in plan[j])
        edge = 0
        for lo, hi in covered + [(PACK_COLS, PACK_COLS)]:
            if lo > edge:
                slab[j % 2, edge:lo, :] = jnp.zeros((lo - edge, slab.shape[2]), F32)
            edge = hi
        for cp in copies(j):
            cp.start()

    for j in range(len(plan)):
        @pl.when(step == j)
        def _():
            if j == 0:
                fetch(0)
            if j + 1 < len(plan):
                fetch(j + 1)
            for cp in copies(j):
                cp.wait()

    o_ref[...] = slab[step % 2].T.astype(BF16)


def _pack_w_in(w, layer, offs):
    _, d, n = w.shape
    plan, starts = _pack_plan(offs, n)
    packed = pl.pallas_call(
        functools.partial(_pack_w_in_kernel, layer, plan), grid=(len(plan),),
        in_specs=[pl.BlockSpec(memory_space=pl.ANY)],
        out_specs=pl.BlockSpec((d, PACK_COLS), lambda j: (0, j)),
        out_shape=SDS((d, len(plan) * PACK_COLS), BF16),
        scratch_shapes=[pltpu.VMEM((2, PACK_COLS, d), F32), pltpu.SemaphoreType.DMA((2,))],
        compiler_params=_cp("arbitrary"), name="pack_w_in",
    )(jnp.swapaxes(w, 1, 2))
    return packed, starts + [len(plan)]


def kernel(x, c, ada_w, ada_b, ada_table, norm1_g, norm2_g, w_in, rwkv_mu, rwkv_w0, rwkv_w2, rwkv_a0, rwkv_a2,
           rwkv_g2, rwkv_kk, rwkv_ka, rwkv_rk, rwkv_lnx_w, rwkv_lnx_b, pool_w, pool_scale, ssm_conv_w, ssm_conv_b,
           ssm_dt_bias, ssm_a_log, ssm_d, ssm_norm_w, gate_up, gate_b, w_branch, w_out, router_group_w,
           router_group_b, router_expert_w, router_expert_b, exp_w_gate, exp_w_up, exp_w_down, final_g):
    bsz, seq, d = x.shape
    m = bsz * seq
    a_cols = rwkv_mu.shape[1]
    b_cols = 3 * len(ATT_GROUPS) * ATT_HEADS * ATT_HEAD
    d_cols = SSM_INNER + SSM_CONV_DIM + SSM_HEADS
    offs = (0, a_cols, a_cols + b_cols, a_cols + b_cols + BRANCH_WIDTH, a_cols + b_cols + BRANCH_WIDTH + d_cols)
    cond = _cond(c, ada_w, ada_b)
    for l in range(DEPTH):
        shift1, scale1, gate1, shift2, scale2, gate2 = jnp.split(cond + ada_table[l], 6, axis=-1)
        dilations = tuple(dil for _, dil in ATT_GROUPS)
        strided = [dil for dil in dilations if dil > 1]
        h, *h_phases = _norm_mod_phases(x, norm1_g[l], scale1, shift1, strided)
        h = h.reshape(m, d)
        h_by_dilation = {dil: hp.reshape(m, d) for dil, hp in zip(strided, h_phases)}
        w_packed, blk = _pack_w_in(w_in, l, offs)
        proj = lambda a, seg, dtype: _matmul(a, w_packed, dtype, blk[seg], blk[seg + 1] - blk[seg], tn=PACK_COLS)
        pa = proj(h, 0, F32).reshape(bsz, seq, A_PAD)
        pbs = [proj(h_by_dilation.get(dil, h), 1 + gi, BF16).reshape(bsz, dil, seq // dil, -1)
               for gi, dil in enumerate(dilations)]
        pc = proj(h, 4, F32).reshape(bsz, seq, BRANCH_WIDTH)
        pd = proj(h, 5, F32).reshape(bsz, seq, D_PAD)
        pg = proj(h, 6, BF16)
        ya, ga = _rwkv_mixer(pa, rwkv_mu[l], rwkv_w0[l], rwkv_w2[l], rwkv_a0[l], rwkv_a2[l], rwkv_g2[l],
                             rwkv_kk[l], rwkv_ka[l], rwkv_rk[l], rwkv_lnx_w[l], rwkv_lnx_b[l])
        ob = _attention_mixer(pbs)
        oc = _pool_mixer(pc, pool_w[l], pool_scale[l])
        od = _mamba_mixer(pd, ssm_conv_w[l], ssm_conv_b[l], ssm_dt_bias[l], ssm_a_log[l], ssm_d[l], ssm_norm_w[l])
        merged = _merge(pg, ya.reshape(m, BRANCH_WIDTH), ga.reshape(m, BRANCH_WIDTH), ob, oc, od,
                        _cast_bf16(gate_up, l), gate_b[l], _cast_bf16(w_branch, l))
        w_out_l = _cast_bf16(w_out.reshape(DEPTH, 1, d, d), l).reshape(d, d)
        x = _matmul_resid(merged, w_out_l, x.reshape(m, d), gate1, seq).reshape(bsz, seq, d)
        h2, expert, weight, counts = _norm_route(x, norm2_g[l], scale2, shift2,
                                                 jnp.swapaxes(router_group_w, 1, 2)[l], router_group_b[l],
                                                 jnp.swapaxes(router_expert_w, 1, 2)[l], router_expert_b[l])
        y2 = _moe_apply(h2, expert, weight, counts, exp_w_gate, exp_w_up, exp_w_down, l)
        x = _moe_combine(x, y2, gate2, final_g if l == DEPTH - 1 else None)
    return x
```

```python
import functools

import jax
import jax.numpy as jnp
from jax import lax
from jax.experimental import pallas as pl
from jax.experimental.pallas import tpu as pltpu

F32 = jnp.float32
BF16 = jnp.bfloat16
HI = lax.Precision.HIGHEST
SDS = jax.ShapeDtypeStruct

DEPTH = 2
NORM_EPS = 1e-6
BRANCH_WIDTH = 1024
RWKV_HEAD = 64
RWKV_HEADS = 16
RWKV_DECAY_RANK = 64
RWKV_ICLR_RANK = 64
RWKV_GATE_RANK = 160
RWKV_GN_EPS = 64e-5
RWKV_CHUNK = 64
ATT_HEAD = 128
ATT_GROUPS = ((128, 1), (512, 4), (2048, 16))
ATT_HEADS = 8
ATT_BLOCK = 128
POOL_WINDOWS = (2, 4, 8, 16)
POOL_GROUP = 256
POOL_HALO = 16
SSM_INNER = 1024
SSM_HEAD = 64
SSM_HEADS = 16
SSM_GROUPS = 4
SSM_STATE = 128
SSM_CONV = 4
SSM_CHUNK = 128
SSM_CONV_DIM = SSM_INNER + 2 * SSM_GROUPS * SSM_STATE
GATE_RANK = 256
MOE_GROUPS = 4
MOE_EXPERTS_PER_GROUP = 8
MOE_EXPERTS = 32
MOE_TOP_K = 2
MOE_FF = 512
MOE_ROWS = 256
LANES = 128
NEG = -1e30
A_MAIN = 3 * BRANCH_WIDTH
LOW_RANK_PAD = LANES
GATE_RANK_PAD = 2 * LANES
A_PAD = A_MAIN + 2 * LOW_RANK_PAD + GATE_RANK_PAD
PACK_COLS = 512
D_PAD = -(-(SSM_INNER + SSM_CONV_DIM + SSM_HEADS) // PACK_COLS) * PACK_COLS
VMEM_LIMIT = 56 * 1024 * 1024


def _cp(*sem):
    return pltpu.CompilerParams(dimension_semantics=sem, vmem_limit_bytes=VMEM_LIMIT)


def _sigmoid(x):
    return 1.0 / (1.0 + jnp.exp(-x))


def _silu(x):
    return x * _sigmoid(x)


def _softplus(x):
    return jnp.maximum(x, 0.0) + jnp.log(1.0 + jnp.exp(-jnp.abs(x)))


def _dot(a, b):
    return jnp.dot(a, b, preferred_element_type=F32)


def _dot_hi(a, b):
    return jnp.dot(a, b, precision=HI, preferred_element_type=F32)


def _cond_kernel(c_ref, w_ref, b_ref, o_ref, a_ref):
    @pl.when(pl.program_id(0) == 0)
    def _():
        a_ref[...] = _silu(c_ref[...])

    k, tn = w_ref.shape
    w = w_ref[...]
    rows = []
    for r in range(a_ref.shape[0]):
        prod = w * jnp.concatenate([a_ref[r]] * (tn // LANES), axis=1)
        part = jnp.sum(prod.reshape(k // 8, 8, tn), axis=0)
        rows.append(jnp.sum(part, axis=0, keepdims=True))
    rows.append(jnp.zeros((o_ref.shape[0] - len(rows), tn), F32))
    o_ref[...] = jnp.concatenate(rows, axis=0) + b_ref[...]


def _cond(c, ada_w, ada_b):
    bsz, d = c.shape
    n = ada_w.shape[1]
    tn = 512
    c_lanes = jnp.broadcast_to(c[:, :, None], (bsz, d, LANES))
    out = pl.pallas_call(
        _cond_kernel, grid=(n // tn,),
        in_specs=[pl.BlockSpec((bsz, d, LANES), lambda j: (0, 0, 0)),
                  pl.BlockSpec((d, tn), lambda j: (0, j)),
                  pl.BlockSpec((1, tn), lambda j: (0, j))],
        out_specs=pl.BlockSpec((8, tn), lambda j: (0, j)),
        out_shape=SDS((8, n), F32), scratch_shapes=[pltpu.VMEM((bsz, d, LANES), F32)],
        compiler_params=_cp("arbitrary"), name="cond",
    )(c_lanes, ada_w, ada_b.reshape(1, n))
    return out[:bsz]


def _norm_mod_phases_kernel(x_ref, g_ref, sc_ref, sh_ref, o_ref, *refs):
    phase_refs, scr = refs[:-1], refs[-1]
    x = x_ref[0]
    tm, d = x.shape
    ms = jnp.mean(x * x, axis=-1, keepdims=True)
    y = x * lax.rsqrt(ms + NORM_EPS) * g_ref[...] * (1.0 + sc_ref[0]) + sh_ref[0]
    o_ref[0] = y.astype(o_ref.dtype)
    for ref in phase_refs:
        dil = ref.shape[1]
        n = tm // dil
        y3 = y.reshape(n, dil, d)
        for ph in range(dil):
            scr[ph * n:(ph + 1) * n, :] = y3[:, ph, :]
        for ph in range(dil):
            ref[0, ph] = scr[ph * n:(ph + 1) * n, :].astype(ref.dtype)


def _norm_mod_phases(x, g, scale, shift, dilations, tm=256):
    bsz, seq, d = x.shape
    tok = pl.BlockSpec((1, tm, d), lambda b, i: (b, i, 0))
    vec = pl.BlockSpec((1, 1, d), lambda b, i: (b, 0, 0))
    return pl.pallas_call(
        _norm_mod_phases_kernel, grid=(bsz, seq // tm),
        in_specs=[tok, pl.BlockSpec((1, d), lambda b, i: (0, 0)), vec, vec],
        out_specs=[tok] + [pl.BlockSpec((1, dil, tm // dil, d), lambda b, i: (b, 0, i, 0)) for dil in dilations],
        out_shape=[SDS((bsz, seq, d), BF16)] + [SDS((bsz, dil, seq // dil, d), BF16) for dil in dilations],
        scratch_shapes=[pltpu.VMEM((tm, d), F32)],
        compiler_params=_cp("parallel", "parallel"), name="norm_mod_phases",
    )(x, g.reshape(1, d), scale.reshape(bsz, 1, d), shift.reshape(bsz, 1, d))


def _mm_kernel(a_ref, b_ref, o_ref):
    o_ref[...] = _dot(a_ref[...], b_ref[...]).astype(o_ref.dtype)


def _matmul(a, b, out_dtype, first_block=0, n_blocks=None, tm=1024, tn=512):
    m, k = a.shape
    n = b.shape[1] if n_blocks is None else n_blocks * tn
    return pl.pallas_call(
        _mm_kernel, grid=(m // tm, n // tn),
        in_specs=[pl.BlockSpec((tm, k), lambda i, j: (i, 0)),
                  pl.BlockSpec((k, tn), lambda i, j: (0, first_block + j))],
        out_specs=pl.BlockSpec((tm, tn), lambda i, j: (i, j)),
        out_shape=SDS((m, n), out_dtype), compiler_params=_cp("parallel", "parallel"), name="matmul",
    )(a, b)


def _mm_resid_kernel(a_ref, b_ref, r_ref, g_ref, o_ref):
    o_ref[...] = r_ref[...] + g_ref[0] * _dot(a_ref[...], b_ref[...])


def _matmul_resid(a, b, resid, gate, seq, tm=1024, tn=512):
    m, k = a.shape
    n = b.shape[1]
    bsz = gate.shape[0]
    return pl.pallas_call(
        _mm_resid_kernel, grid=(m // tm, n // tn),
        in_specs=[pl.BlockSpec((tm, k), lambda i, j: (i, 0)),
                  pl.BlockSpec((k, tn), lambda i, j: (0, j)),
                  pl.BlockSpec((tm, tn), lambda i, j: (i, j)),
                  pl.BlockSpec((1, 1, tn), lambda i, j: (i * tm // seq, 0, j))],
        out_specs=pl.BlockSpec((tm, tn), lambda i, j: (i, j)),
        out_shape=SDS((m, n), F32), compiler_params=_cp("parallel", "parallel"), name="matmul_resid",
    )(a, b, resid, gate.reshape(bsz, 1, n))


def _rwkv_prep_kernel(p_ref, halo_ref, mu_ref, w0_ref, a0_ref, kkw_ref, kaw_ref, w2_ref, a2_ref, g2_ref,
                      r_ref, k_ref, v_ref, kk_ref, a_ref, lw_ref, g_ref):
    x = p_ref[0]
    w = BRANCH_WIDTH
    prev_first = jnp.where(pl.program_id(1) > 0, halo_ref[0][7:8, :], 0.0)
    row = lax.broadcasted_iota(jnp.int32, x.shape, 0)
    prev = jnp.where(row == 0, prev_first, pltpu.roll(x, 1, axis=0))
    p = x + (prev - x) * mu_ref[...]
    r, k, v = p[:, 0:w], p[:, w:2 * w], p[:, 2 * w:3 * w]
    o_a, o_g = A_MAIN + LOW_RANK_PAD, A_MAIN + 2 * LOW_RANK_PAD
    xw, xa, xg = p[:, A_MAIN:o_a], p[:, o_a:o_g], p[:, o_g:A_PAD]
    wlog = -_softplus(-(w0_ref[...] + _dot_hi(jnp.tanh(xw), w2_ref[...]))) - 0.5
    a = _sigmoid(a0_ref[...] + _dot_hi(xa, a2_ref[...]))
    r_ref[0] = r
    k_ref[0] = k * (1.0 + (a - 1.0) * kaw_ref[...])
    v_ref[0] = v
    kk_ref[0] = k * kkw_ref[...]
    a_ref[0] = a
    lw_ref[0] = -jnp.exp(wlog)
    g_ref[0] = _dot_hi(_sigmoid(xg), g2_ref[...])


def _rwkv_prep(pa, mu, w0, a0, kkw, kaw, w2, a2, g2, tm=256):
    bsz, seq, ap = pa.shape
    w = BRANCH_WIDTH
    row = lambda t: t.reshape(1, -1)
    full = lambda shape: pl.BlockSpec(shape, lambda b, i: (0, 0))
    out_spec = pl.BlockSpec((1, tm, w), lambda b, i: (b, i, 0))
    return pl.pallas_call(
        _rwkv_prep_kernel, grid=(bsz, seq // tm),
        in_specs=[pl.BlockSpec((1, tm, ap), lambda b, i: (b, i, 0)),
                  pl.BlockSpec((1, 8, ap), lambda b, i: (b, jnp.maximum(i * (tm // 8) - 1, 0), 0)),
                  full((1, ap)), full((1, w)), full((1, w)), full((1, w)), full((1, w)),
                  full((LOW_RANK_PAD, w)), full((LOW_RANK_PAD, w)), full((GATE_RANK_PAD, w))],
        out_specs=[out_spec] * 7,
        out_shape=[SDS((bsz, seq, w), F32)] * 7,
        compiler_params=_cp("parallel", "parallel"), name="rwkv_prep",
    )(pa, pa, row(mu), row(w0), row(a0), row(kkw), row(kaw), w2, a2, g2)


def _rwkv_rec_kernel(r_ref, k_ref, v_ref, kk_ref, a_ref, lw_ref, rk_ref, lnw_ref, lnb_ref, o_ref, s_ref):
    @pl.when(pl.program_id(1) == 0)
    def _():
        s_ref[...] = jnp.zeros_like(s_ref)

    nh, n = s_ref.shape[0], s_ref.shape[1]
    heads = lambda ref: jnp.stack([ref[0, :, h * n:(h + 1) * n] for h in range(nh)])
    r, k, v, kk, a, lw = (heads(ref) for ref in (r_ref, k_ref, v_ref, kk_ref, a_ref, lw_ref))
    c = r.shape[1]
    kn = kk / jnp.maximum(jnp.sqrt(jnp.sum(kk * kk, axis=-1, keepdims=True)), 1e-12)
    b = kn * a
    row = lax.broadcasted_iota(jnp.int32, (c, c), 0)
    col = lax.broadcasted_iota(jnp.int32, (c, c), 1)
    strict, incl = (row > col)[None], (row >= col)[None]
    bdot = lambda spec: (lambda x, y: jnp.einsum(spec, x.astype(BF16), y.astype(BF16), preferred_element_type=F32))
    nt, nn, tn = bdot('hik,hjk->hij'), bdot('hij,hjk->hik'), bdot('hiv,hik->hvk')
    tril = jnp.broadcast_to((row >= col).astype(BF16)[None], (nh, c, c))
    lw_hi = lw.astype(BF16)
    lw_r1 = lw - lw_hi.astype(F32)
    lw_mid = lw_r1.astype(BF16)
    lw_lo = (lw_r1 - lw_mid.astype(F32)).astype(BF16)
    lc = nn(tril, lw_hi) + (nn(tril, lw_mid) + nn(tril, lw_lo))
    lc_last = lc[:, c - 1:c, :]
    e_neg, e_end = jnp.exp(-lc), jnp.exp(lc_last - lc)
    kt, rt = kn * jnp.exp(lc - lw), r * jnp.exp(lc)
    bt, kd = b * e_neg, k * e_neg
    kr = jnp.concatenate([kt, rt], axis=1)
    sc = nt(kr, jnp.concatenate([bt, kd], axis=1))
    lm = jnp.where(strict, -sc[:, :c, :c], 0.0)
    ak = jnp.where(strict, sc[:, :c, c:], 0.0)
    bb = jnp.where(incl, sc[:, c:, :c], 0.0)
    bk = jnp.where(incl, sc[:, c:, c:], 0.0)
    s0 = s_ref[...]
    kr_s0 = nt(kr, s0)
    abk_v = nn(jnp.concatenate([ak, bk], axis=1), v)
    u = -(kr_s0[:, :c] + abk_v[:, :c])
    pw = lm
    n_doublings = c.bit_length() - 1
    for it in range(n_doublings):
        u = u + nn(pw, u)
        if it + 1 < n_doublings:
            pw = nn(pw, pw)
    y = kr_s0[:, c:] + nn(bb, u) + abk_v[:, c:]
    s_ref[...] = s0 * jnp.exp(lc_last) + tn(jnp.concatenate([u, v], axis=1),
                                            jnp.concatenate([b * e_end, k * e_end], axis=1))
    mean = jnp.mean(y, axis=-1, keepdims=True)
    var = jnp.mean(jnp.square(y - mean), axis=-1, keepdims=True)
    out = (y - mean) * lax.rsqrt(var + RWKV_GN_EPS) * lnw_ref[...] + lnb_ref[...]
    out = out + jnp.sum(r * k * rk_ref[...], axis=-1, keepdims=True) * v
    for h in range(nh):
        o_ref[0, :, h * n:(h + 1) * n] = out[h]


def _rwkv_rec(r, k, v, kk, a, lw, rk, lnw, lnb, chunk=RWKV_CHUNK):
    bsz, seq, w = r.shape
    nh, n = rk.shape
    blk = pl.BlockSpec((1, chunk, w), lambda b, i: (b, i, 0))
    par = pl.BlockSpec((nh, 1, n), lambda b, i: (0, 0, 0))
    return pl.pallas_call(
        _rwkv_rec_kernel, grid=(bsz, seq // chunk),
        in_specs=[blk] * 6 + [par] * 3, out_specs=blk,
        out_shape=SDS((bsz, seq, w), F32),
        scratch_shapes=[pltpu.VMEM((nh, n, n), F32)],
        compiler_params=_cp("parallel", "arbitrary"), name="rwkv_rec",
    )(r, k, v, kk, a, lw, rk.reshape(nh, 1, n), lnw.reshape(nh, 1, n), lnb.reshape(nh, 1, n))


def _pad_rows(w, rows):
    return jnp.zeros((rows, w.shape[1]), w.dtype).at[:w.shape[0]].set(w)


def _rwkv_pack_cols(t):
    lead = t.shape[:-1]
    z = lambda n: jnp.zeros(lead + (n,), t.dtype)
    o1 = A_MAIN + RWKV_DECAY_RANK
    o2 = o1 + RWKV_ICLR_RANK
    return jnp.concatenate([t[..., :A_MAIN], t[..., A_MAIN:o1], z(LOW_RANK_PAD - RWKV_DECAY_RANK), t[..., o1:o2],
                            z(LOW_RANK_PAD - RWKV_ICLR_RANK), t[..., o2:], z(GATE_RANK_PAD - RWKV_GATE_RANK)], axis=-1)


def _rwkv_mixer(pa, mu, w0, w2, a0, a2, g2, kkw, kaw, rk, lnw, lnb):
    bsz, seq, _ = pa.shape
    r, k, v, kk, a, lw, g = _rwkv_prep(pa, _rwkv_pack_cols(mu), w0, a0, kkw, kaw,
                                       _pad_rows(w2, LOW_RANK_PAD), _pad_rows(a2, LOW_RANK_PAD),
                                       _pad_rows(g2, GATE_RANK_PAD))
    y = _rwkv_rec(r, k, v, kk, a, lw, rk, lnw.reshape(RWKV_HEADS, RWKV_HEAD), lnb.reshape(RWKV_HEADS, RWKV_HEAD))
    return y, g


def _att_kernel(q_ref, kc_ref, kp_ref, vc_ref, vp_ref, o_ref, lse_ref):
    nt = lambda x, y: lax.dot_general(x, y, (((1,), (1,)), ((), ())), preferred_element_type=F32)
    scale = ATT_HEAD ** -0.5
    qi = lax.broadcasted_iota(jnp.int32, (ATT_BLOCK, ATT_BLOCK), 0)
    ki = lax.broadcasted_iota(jnp.int32, (ATT_BLOCK, ATT_BLOCK), 1)
    cur_ok = ki <= qi
    prev_ok = (ki >= qi) & (pl.program_id(2) > 0)
    lane = lax.broadcasted_iota(jnp.int32, (ATT_BLOCK, LANES), 1)
    lse_all = jnp.zeros((ATT_BLOCK, LANES), F32)
    heads = lambda ref: jnp.stack([ref[0, 0, :, h * ATT_HEAD:(h + 1) * ATT_HEAD] for h in range(ATT_HEADS)])
    q, kc, kp, vc, vp = (heads(ref) for ref in (q_ref, kc_ref, kp_ref, vc_ref, vp_ref))
    bnt = lambda x, y: jnp.einsum('hqd,hkd->hqk', x, y, preferred_element_type=F32)
    bnn = lambda x, y: jnp.einsum('hqk,hkd->hqd', x, y, preferred_element_type=F32)
    s_c = jnp.where(cur_ok[None], bnt(q, kc) * scale, NEG)
    s_p = jnp.where(prev_ok[None], bnt(q, kp) * scale, NEG)
    m = jnp.maximum(jnp.max(s_c, axis=-1, keepdims=True), jnp.max(s_p, axis=-1, keepdims=True))
    e_c, e_p = jnp.exp(s_c - m), jnp.exp(s_p - m)
    den = jnp.sum(e_c, axis=-1, keepdims=True) + jnp.sum(e_p, axis=-1, keepdims=True)
    o = (bnn(e_c.astype(BF16), vc) + bnn(e_p.astype(BF16), vp)) / den
    lse = m + jnp.log(den)
    for h in range(ATT_HEADS):
        o_ref[0, 0, :, h * ATT_HEAD:(h + 1) * ATT_HEAD] = o[h]
        lse_all = jnp.where(lane == h, lse[h], lse_all)
    lse_ref[0, 0] = lse_all


def _att_group(pb, gi):
    bsz, dilation, n_phase, _ = pb.shape
    w = ATT_HEADS * ATT_HEAD
    nblk = n_phase // ATT_BLOCK
    blk = (1, 1, ATT_BLOCK, w)
    cur = lambda t: pl.BlockSpec(blk, lambda b, ph, i: (b, ph, i, t))
    prev = lambda t: pl.BlockSpec(blk, lambda b, ph, i: (b, ph, jnp.maximum(i - 1, 0), t))
    return pl.pallas_call(
        _att_kernel, grid=(bsz, dilation, nblk),
        in_specs=[cur(0), cur(1), prev(1), cur(2), prev(2)],
        out_specs=[pl.BlockSpec(blk, lambda b, ph, i: (b, ph, i, 0)),
                   pl.BlockSpec((1, 1, ATT_BLOCK, LANES), lambda b, ph, i: (b, ph, i, 0))],
        out_shape=[SDS((bsz, dilation, n_phase, w), F32), SDS((bsz, dilation, n_phase, LANES), F32)],
        compiler_params=_cp("parallel", "parallel", "arbitrary"), name=f"att_g{gi}",
    )(pb, pb, pb, pb, pb)


def _att_combine_kernel(o0, o1, o2, l0, l1, l2, e_ref, out_ref):
    tm = out_ref.shape[0]

    def tokens(ref):
        d = ref.shape[1]
        if d == 1:
            return ref[0, 0]
        return jnp.stack([ref[0, ph] for ph in range(d)], axis=1).reshape(tm, ref.shape[-1])

    lses = [tokens(l) for l in (l0, l1, l2)]
    m = jnp.maximum(jnp.maximum(lses[0], lses[1]), lses[2])
    es = [jnp.exp(l - m) for l in lses]
    inv = 1.0 / (es[0] + es[1] + es[2])
    acc = None
    for e, o in zip(es, (o0, o1, o2)):
        term = _dot_hi(e * inv, e_ref[...]) * tokens(o)
        acc = term if acc is None else acc + term
    out_ref[...] = acc.astype(out_ref.dtype)


def _attention_mixer(pbs, tm=256):
    res = [_att_group(pb, gi) for gi, pb in enumerate(pbs)]
    bsz, _, seq, w = res[0][0].shape
    nt = seq // tm
    spec = lambda t: pl.BlockSpec((1, t.shape[1], tm // t.shape[1], t.shape[3]), lambda b, i: (b, 0, i, 0))
    expand = (jnp.arange(LANES, dtype=jnp.int32)[:, None] == jnp.arange(w, dtype=jnp.int32)[None, :] // ATT_HEAD).astype(F32)
    args = [t[0] for t in res] + [t[1] for t in res]
    return pl.pallas_call(
        _att_combine_kernel, grid=(bsz, nt),
        in_specs=[spec(t) for t in args] + [pl.BlockSpec((LANES, w), lambda b, i: (0, 0))],
        out_specs=pl.BlockSpec((tm, w), lambda b, i: (b * nt + i, 0)),
        out_shape=SDS((bsz * seq, w), BF16), compiler_params=_cp("parallel", "parallel"), name="att_combine",
    )(*args, expand)


def _pool_kernel(x_ref, halo_ref, w_ref, sc_ref, o_ref, ext_ref):
    tm = x_ref.shape[1]
    i = pl.program_id(1)
    x = x_ref[0]
    ext_ref[0:POOL_HALO, :] = jnp.where(i > 0, halo_ref[0], 0.0)
    ext_ref[POOL_HALO:, :] = x
    pos = i * tm + lax.broadcasted_iota(jnp.int32, (tm, POOL_GROUP), 0)
    outs = []
    for gi, win in enumerate(POOL_WINDOWS):
        cols = slice(gi * POOL_GROUP, (gi + 1) * POOL_GROUP)
        xg = x[:, cols]
        s = xg
        for j in range(1, win):
            s = s + ext_ref[pl.ds(POOL_HALO - j, tm), cols]
        mixed = s / jnp.minimum(pos + 1, win).astype(F32) - xg
        outs.append(_dot(mixed.astype(BF16), w_ref[gi]))
    o_ref[0] = (jnp.concatenate(outs, axis=-1) * sc_ref[...]).astype(o_ref.dtype)


def _pool_mixer(pc, w_pool, scale, tm=256):
    bsz, seq, w = pc.shape
    out = pl.pallas_call(
        _pool_kernel, grid=(bsz, seq // tm),
        in_specs=[pl.BlockSpec((1, tm, w), lambda b, i: (b, i, 0)),
                  pl.BlockSpec((1, POOL_HALO, w), lambda b, i: (b, jnp.maximum(i * (tm // POOL_HALO) - 1, 0), 0)),
                  pl.BlockSpec(w_pool.shape, lambda b, i: (0, 0, 0)),
                  pl.BlockSpec((1, w), lambda b, i: (0, 0))],
        out_specs=pl.BlockSpec((1, tm, w), lambda b, i: (b, i, 0)),
        out_shape=SDS((bsz, seq, w), BF16),
        scratch_shapes=[pltpu.VMEM((tm + POOL_HALO, w), F32)],
        compiler_params=_cp("parallel", "parallel"), name="pool",
    )(pc, pc, w_pool.astype(BF16), scale.reshape(1, w))
    return out.reshape(bsz * seq, w)


SSM_HALO = 8


def _ssd_kernel(p_ref, halo_ref, cw_ref, cb_ref, dtb_ref, ah_ref, dsk_ref, nw_ref, e64_ref, e128_ref,
                o_ref, ext_ref, h_ref):
    ci = pl.program_id(1)
    q = SSM_CHUNK
    inner = SSM_INNER
    xbc_lo, xbc_hi = inner, inner + SSM_CONV_DIM

    @pl.when(ci == 0)
    def _():
        h_ref[...] = jnp.zeros_like(h_ref)

    z = p_ref[0, :, 0:inner]
    ext_ref[0:SSM_HALO, :] = jnp.where(ci > 0, halo_ref[0, :, xbc_lo:xbc_hi], 0.0)
    ext_ref[SSM_HALO:, :] = p_ref[0, :, xbc_lo:xbc_hi]
    conv = cb_ref[...]
    for j in range(SSM_CONV):
        conv = conv + cw_ref[j:j + 1, :] * ext_ref[pl.ds(SSM_HALO - (SSM_CONV - 1) + j, q), :]
    xbc = _silu(conv)
    xs = xbc[:, 0:inner]
    dt = _softplus(p_ref[0, :, xbc_hi:xbc_hi + LANES] + dtb_ref[...])
    a = dt * ah_ref[...]
    row = lax.broadcasted_iota(jnp.int32, (q, q), 0)
    col = lax.broadcasted_iota(jnp.int32, (q, q), 1)
    causal = row >= col
    a_cum = _dot_hi(causal.astype(F32), a)
    a_cum_t = a_cum.T
    dt_full = _dot_hi(dt, e64_ref[...])
    acum_full = _dot_hi(a_cum, e64_ref[...])
    alast_full = acum_full[q - 1:q, :]
    acum_b = _dot_hi(a_cum, e128_ref[...])
    xdt = xs * dt_full
    x_to_end = xdt * jnp.exp(alast_full - acum_full)
    exp_ac = jnp.exp(acum_full)
    chunk_dec = jnp.exp(alast_full)
    lane = lax.broadcasted_iota(jnp.int32, (q, LANES), 1)
    first_head = lane < SSM_HEAD
    ys = []
    for g in range(SSM_GROUPS):
        bm = xbc[:, inner + g * SSM_STATE:inner + (g + 1) * SSM_STATE]
        cm = xbc[:, inner + (SSM_GROUPS + g) * SSM_STATE:inner + (SSM_GROUPS + g + 1) * SSM_STATE].astype(BF16)
        bt = bm.T.astype(BF16)
        cb = _dot(cm, bt)
        for pr in range(SSM_HEADS // SSM_GROUPS // 2):
            pi = g * (SSM_HEADS // SSM_GROUPS // 2) + pr
            cols = slice(pi * LANES, (pi + 1) * LANES)
            mats = []
            for hd in (2 * pi, 2 * pi + 1):
                seg = acum_b[:, hd * LANES:(hd + 1) * LANES] - a_cum_t[hd:hd + 1, :]
                mats.append((cb * jnp.exp(jnp.where(causal, seg, NEG))).astype(BF16))
            xp = xdt[:, cols]
            x_blockdiag = jnp.concatenate([jnp.where(first_head, xp, 0.0), jnp.where(first_head, 0.0, xp)], axis=0)
            y_diag = _dot(jnp.concatenate(mats, axis=1), x_blockdiag.astype(BF16))
            h_t = h_ref[pi]
            y_off = _dot(cm, h_t.astype(BF16)) * exp_ac[:, cols]
            h_ref[pi] = h_t * chunk_dec[:, cols] + _dot(bt, x_to_end[:, cols].astype(BF16))
            ys.append(y_diag + y_off)
    y = (jnp.concatenate(ys, axis=-1) + xs * dsk_ref[...]) * _silu(z)
    gsize = inner // SSM_GROUPS
    outs = []
    for g in range(SSM_GROUPS):
        yg = y[:, g * gsize:(g + 1) * gsize]
        outs.append(yg * lax.rsqrt(jnp.mean(yg * yg, axis=-1, keepdims=True) + NORM_EPS))
    o_ref[0] = (jnp.concatenate(outs, axis=-1) * nw_ref[...]).astype(o_ref.dtype)


def _mamba_mixer(pd, conv_w, conv_b, dt_bias, a_log, d_skip, norm_w):
    bsz, seq, dp = pd.shape
    q = SSM_CHUNK
    pad_heads = lambda t: jnp.zeros((1, LANES), F32).at[0, :SSM_HEADS].set(t)
    head_of = lambda width: jnp.arange(SSM_HEADS * width, dtype=jnp.int32)[None, :] // width
    expand = lambda width: (jnp.arange(LANES, dtype=jnp.int32)[:, None] == head_of(width)).astype(F32)
    full2 = lambda shape: pl.BlockSpec(shape, lambda b, i: (0, 0))
    out = pl.pallas_call(
        _ssd_kernel, grid=(bsz, seq // q),
        in_specs=[pl.BlockSpec((1, q, dp), lambda b, i: (b, i, 0)),
                  pl.BlockSpec((1, SSM_HALO, dp), lambda b, i: (b, jnp.maximum(i * (q // SSM_HALO) - 1, 0), 0)),
                  full2((SSM_CONV, SSM_CONV_DIM)), full2((1, SSM_CONV_DIM)), full2((1, LANES)), full2((1, LANES)),
                  full2((1, SSM_INNER)), full2((1, SSM_INNER)),
                  full2((LANES, SSM_HEADS * SSM_HEAD)), full2((LANES, SSM_HEADS * LANES))],
        out_specs=pl.BlockSpec((1, q, SSM_INNER), lambda b, i: (b, i, 0)),
        out_shape=SDS((bsz, seq, SSM_INNER), BF16),
        scratch_shapes=[pltpu.VMEM((q + SSM_HALO, SSM_CONV_DIM), F32),
                        pltpu.VMEM((SSM_HEADS // 2, SSM_STATE, LANES), F32)],
        compiler_params=_cp("parallel", "arbitrary"), name="ssd",
    )(pd, pd, conv_w, conv_b.reshape(1, -1), pad_heads(dt_bias), pad_heads(-jnp.exp(a_log)),
      jnp.repeat(d_skip, SSM_HEAD).reshape(1, -1), norm_w.reshape(1, -1), expand(SSM_HEAD), expand(LANES))
    return out.reshape(bsz * seq, SSM_INNER)


def _merge_kernel(pg_ref, ya_ref, ga_ref, bb_ref, bc_ref, bd_ref, gu_ref, gb_ref, wb_ref, o_ref):
    pg = pg_ref[...]
    branches = ((ya_ref[...] * ga_ref[...]).astype(BF16), bb_ref[...], bc_ref[...], bd_ref[...])
    acc = None
    for bi, br in enumerate(branches):
        term = _sigmoid(_dot(pg, gu_ref[bi]) + gb_ref[bi]) * _dot(br, wb_ref[bi])
        acc = term if acc is None else acc + term
    o_ref[...] = acc.astype(o_ref.dtype)


def _merge(pg, ya, ga, bb, bc, bd, gate_up, gate_b, w_branch, tm=1024, tn=512):
    m = pg.shape[0]
    nb, kw, d = w_branch.shape
    rows = lambda k: pl.BlockSpec((tm, k), lambda i, j: (i, 0))
    return pl.pallas_call(
        _merge_kernel, grid=(m // tm, d // tn),
        in_specs=[rows(gate_up.shape[1]), rows(kw), rows(kw), rows(kw), rows(kw), rows(kw),
                  pl.BlockSpec((nb, gate_up.shape[1], tn), lambda i, j: (0, 0, j)),
                  pl.BlockSpec((nb, 1, tn), lambda i, j: (0, 0, j)),
                  pl.BlockSpec((nb, kw, tn), lambda i, j: (0, 0, j))],
        out_specs=pl.BlockSpec((tm, tn), lambda i, j: (i, j)),
        out_shape=SDS((m, d), BF16), compiler_params=_cp("parallel", "parallel"), name="merge",
    )(pg, ya, ga, bb, bc, bd, gate_up, gate_b.reshape(nb, 1, d), w_branch)


SUBLANES = 8


def _store_slabs(ref, rows):
    w = ref.shape[-1]
    for j in range(SUBLANES):
        ref[:, j, :] = rows[:, j * w:(j + 1) * w]


def _norm_route_kernel(x_ref, g_ref, sc_ref, sh_ref, wr_ref, br_ref, h_ref, ids_ref, wts_ref, hist_ref):
    x = x_ref[0]
    ms = jnp.mean(x * x, axis=-1, keepdims=True)
    h = x * lax.rsqrt(ms + NORM_EPS) * g_ref[...] * (1.0 + sc_ref[0]) + sh_ref[0]
    _store_slabs(h_ref, h)
    lg = lax.dot_general(h, wr_ref[...], (((1,), (1,)), ((), ())), precision=HI, preferred_element_type=F32) + br_ref[...]
    lane = lax.broadcasted_iota(jnp.int32, lg.shape, 1)
    lane_f = lane.astype(F32)
    first = lambda hit: jnp.min(jnp.where(hit, lane_f, float(LANES)), axis=-1, keepdims=True)
    gmask = lane < MOE_GROUPS
    gl = jnp.where(gmask, lg, NEG)
    gmax = jnp.max(gl, axis=-1, keepdims=True)
    gsel = first(gl == gmax)
    gprob = 1.0 / jnp.sum(jnp.where(gmask, jnp.exp(gl - gmax), 0.0), axis=-1, keepdims=True)
    lo = MOE_GROUPS + gsel * MOE_EXPERTS_PER_GROUP
    emask = (lane_f >= lo) & (lane_f < lo + MOE_EXPERTS_PER_GROUP)
    el = jnp.where(emask, lg, NEG)
    v1 = jnp.max(el, axis=-1, keepdims=True)
    i1 = first(el == v1)
    el2 = jnp.where(lane_f == i1, NEG, el)
    v2 = jnp.max(el2, axis=-1, keepdims=True)
    i2 = first((el2 == v2) & emask & (lane_f != i1))
    t = jnp.exp(v2 - v1)
    w1 = gprob / (1.0 + t)
    w2 = gprob * t / (1.0 + t)
    ids_ref[0] = jnp.where(lane == 0, i1, jnp.where(lane == 1, i2, float(MOE_GROUPS))).astype(jnp.int32) - MOE_GROUPS
    wts_ref[0] = jnp.where(lane == 0, w1, jnp.where(lane == 1, w2, 0.0))
    chosen = jnp.where((lane_f == i1) | (lane_f == i2), 1.0, 0.0)
    hist_ref[...] = jnp.broadcast_to(jnp.sum(chosen, axis=0, keepdims=True), hist_ref.shape)


def _norm_route(x, g, scale, shift, rg_wt, rg_b, re_wt, re_b, tm=256):
    bsz, seq, d = x.shape
    n_log = MOE_GROUPS + MOE_EXPERTS
    wr = jnp.zeros((LANES, d), F32).at[:n_log].set(jnp.concatenate([rg_wt, re_wt], axis=0))
    br = jnp.zeros((1, LANES), F32).at[0, :n_log].set(jnp.concatenate([rg_b, re_b]))
    tok = lambda width: pl.BlockSpec((1, tm, width), lambda b, i: (b, i, 0))
    m, dw, nt = bsz * seq, d // SUBLANES, seq // tm
    h, ids, wts, hist = pl.pallas_call(
        _norm_route_kernel, grid=(bsz, nt),
        in_specs=[tok(d), pl.BlockSpec((1, d), lambda b, i: (0, 0)),
                  pl.BlockSpec((1, 1, d), lambda b, i: (b, 0, 0)), pl.BlockSpec((1, 1, d), lambda b, i: (b, 0, 0)),
                  pl.BlockSpec((LANES, d), lambda b, i: (0, 0)), pl.BlockSpec((1, LANES), lambda b, i: (0, 0))],
        out_specs=[pl.BlockSpec((tm, SUBLANES, dw), lambda b, i: (b * nt + i, 0, 0)), tok(LANES), tok(LANES),
                   pl.BlockSpec((SUBLANES, LANES), lambda b, i: (b * nt + i, 0))],
        out_shape=[SDS((m, SUBLANES, dw), F32), SDS((bsz, seq, LANES), jnp.int32), SDS((bsz, seq, LANES), F32),
                   SDS((bsz * nt * SUBLANES, LANES), F32)],
        compiler_params=_cp("parallel", "parallel"), name="norm_route",
    )(x, g.reshape(1, d), scale.reshape(bsz, 1, d), shift.reshape(bsz, 1, d), wr, br)
    counts = jnp.sum(hist[::SUBLANES, MOE_GROUPS:n_log], axis=0).astype(jnp.int32)
    return h, ids.reshape(m, LANES)[:, :MOE_TOP_K], wts.reshape(m, LANES)[:, :MOE_TOP_K], counts


ROW_COPY_UNROLL = 8


FFN_W_SLOTS = 4
FFN_W_CHUNKS = 4


def _ffn_kernel(layer, be_ref, nb_ref, tok_ref, dst_ref, cnt_ref, h_hbm, wg_hbm, wu_hbm, wd_hbm, sw_ref, y_hbm,
                xbuf, ybuf, xrow, wg_s, wu_s, wd_s, stage_in, stage_out, gsem, ssem, wsem):
    j = pl.program_id(0)
    nb = nb_ref[0]
    cur = j % 2

    slots = stage_in.shape[0]

    def weight_steps(e):
        d, ff = wg_s.shape
        rin, rout = d // FFN_W_CHUNKS, ff // FFN_W_CHUNKS
        steps = []
        for src, dst, stage, rows in ((wg_hbm, wg_s, stage_in, rin), (wu_hbm, wu_s, stage_in, rin),
                                      (wd_hbm, wd_s, stage_out, rout)):
            for c in range(FFN_W_CHUNKS):
                i = len(steps)
                copy = pltpu.make_async_copy(src.at[layer, e, pl.ds(c * rows, rows)], stage.at[i % slots],
                                             wsem.at[i % slots])
                steps.append((copy, stage, dst.at[pl.ds(c * rows, rows)]))
        return steps

    def prefetch_expert(e):
        for copy, _, _ in weight_steps(e)[:slots - 1]:
            copy.start()

    def load_expert(e):
        steps = weight_steps(e)
        for i, (copy, stage, dst) in enumerate(steps):
            if i + slots - 1 < len(steps):
                steps[i + slots - 1][0].start()
            copy.wait()
            dst[...] = stage[i % slots].astype(BF16)

    def gather_copy(buf, s, tok):
        return pltpu.make_async_copy(h_hbm.at[pl.ds(tok, 1)], xbuf.at[buf, pl.ds(s, 1)], gsem.at[buf])

    def scatter_copy(buf, s, dst):
        return pltpu.make_async_copy(ybuf.at[buf, pl.ds(s, 1)], y_hbm.at[pl.ds(dst, 1)], ssem.at[buf])

    def start_gather(blk, buf):
        def body(s, carry):
            gather_copy(buf, s, tok_ref[blk * MOE_ROWS + s]).start()
            return carry
        lax.fori_loop(0, MOE_ROWS, body, 0, unroll=ROW_COPY_UNROLL)

    def wait_gather(buf):
        pltpu.make_async_copy(h_hbm.at[pl.ds(0, MOE_ROWS)], xbuf.at[buf], gsem.at[buf]).wait()

    def start_scatter(blk, buf):
        def body(s, carry):
            scatter_copy(buf, s, dst_ref[blk * MOE_ROWS + s]).start()
            return carry
        full = cnt_ref[blk] == MOE_ROWS

        @pl.when(full)
        def _():
            lax.fori_loop(0, MOE_ROWS, body, 0, unroll=ROW_COPY_UNROLL)

        @pl.when(jnp.logical_not(full))
        def _():
            lax.fori_loop(0, cnt_ref[blk], body, 0)

    def wait_scatter(blk, buf):
        n = cnt_ref[blk]

        @pl.when(n > 0)
        def _():
            pltpu.make_async_copy(ybuf.at[buf, pl.ds(0, n)], y_hbm.at[pl.ds(0, n)], ssem.at[buf]).wait()

    @pl.when(j == 0)
    def _():
        start_gather(0, 0)

    @pl.when(j + 1 < nb)
    def _():
        start_gather(j + 1, 1 - cur)

    @pl.when(j < nb)
    def _():
        e = be_ref[j]

        @pl.when(j == 0)
        def _():
            prefetch_expert(e)

        @pl.when((j == 0) | (e != be_ref[jnp.maximum(j - 1, 0)]))
        def _():
            load_expert(e)

        e_next = be_ref[jnp.minimum(j + 1, nb - 1)]

        @pl.when((j + 1 < nb) & (e_next != e))
        def _():
            prefetch_expert(e_next)

        wait_gather(cur)

        @pl.when(j >= 2)
        def _():
            wait_scatter(j - 2, cur)

        dw = xbuf.shape[-1]
        for c in range(SUBLANES):
            xrow[:, c * dw:(c + 1) * dw] = xbuf[cur, :, c, :]
        x = xrow[...].astype(BF16)
        hidden = (_silu(_dot(x, wg_s[...])) * _dot(x, wu_s[...])).astype(BF16)
        _store_slabs(ybuf.at[cur], _dot(hidden, wd_s[...]) * sw_ref[...])
        start_scatter(j, cur)

    @pl.when(j == nb - 1)
    def _():
        @pl.when(j >= 1)
        def _():
            wait_scatter(j - 1, 1 - cur)
        wait_scatter(j, cur)


def _moe_plan(expert, weight, counts):
    n_tok = expert.shape[0]
    n_assign = n_tok * MOE_TOP_K
    flat_e = expert.reshape(-1)
    flat_w = weight.reshape(-1)
    order = jnp.argsort(flat_e).astype(jnp.int32)
    padded = (counts + MOE_ROWS - 1) // MOE_ROWS * MOE_ROWS
    pad_end = jnp.cumsum(padded)
    pad_start = pad_end - padded
    start = jnp.cumsum(counts) - counts
    n_blocks = n_assign // MOE_ROWS + MOE_EXPERTS
    n_slots = n_blocks * MOE_ROWS
    blk0 = jnp.arange(n_blocks, dtype=jnp.int32) * MOE_ROWS
    block_e = jnp.minimum(jnp.searchsorted(pad_end, blk0, side='right'), MOE_EXPERTS - 1).astype(jnp.int32)
    block_cnt = jnp.clip(counts[block_e] - (blk0 - pad_start[block_e]), 0, MOE_ROWS).astype(jnp.int32)
    slot_e = jnp.repeat(block_e, MOE_ROWS)
    pos = jnp.arange(n_slots, dtype=jnp.int32) - pad_start[slot_e]
    valid = pos < counts[slot_e]
    assign = order[jnp.clip(start[slot_e] + pos, 0, n_assign - 1)]
    tok, k = assign // MOE_TOP_K, assign % MOE_TOP_K
    slot_tok = jnp.where(valid, tok, 0).astype(jnp.int32)
    slot_w = jnp.where(valid, flat_w[assign], 0.0)
    slot_dst = jnp.where(valid, k * n_tok + tok, 0).astype(jnp.int32)
    n_used_blocks = (pad_end[-1:] // MOE_ROWS).astype(jnp.int32)
    return slot_tok, slot_w, block_e, slot_dst, block_cnt, n_used_blocks


def _cast_kernel(x_ref, o_ref):
    o_ref[...] = x_ref[0].astype(o_ref.dtype)


def _cast_bf16(w, layer, block_bytes=2 * 1024 * 1024):
    _, e, r, c = w.shape
    tr = min(r, block_bytes // (4 * c))
    return pl.pallas_call(
        _cast_kernel, grid=(e, r // tr),
        in_specs=[pl.BlockSpec((1, 1, tr, c), lambda i, j: (layer, i, j, 0))],
        out_specs=pl.BlockSpec((1, tr, c), lambda i, j: (i, j, 0)),
        out_shape=SDS((e, r, c), BF16), compiler_params=_cp("parallel", "parallel"), name="cast_bf16")(w)


def _moe_apply(h, expert, weight, counts, w_gate, w_up, w_down, layer):
    n_tok, _, dw = h.shape
    d = dw * SUBLANES
    ff = w_gate.shape[-1]
    slot_tok, slot_w, block_e, slot_dst, block_cnt, n_used_blocks = _moe_plan(expert, weight, counts)
    n_slots = slot_tok.shape[0]
    any_space = pl.BlockSpec(memory_space=pl.ANY)
    return pl.pallas_call(
        functools.partial(_ffn_kernel, layer),
        grid_spec=pltpu.PrefetchScalarGridSpec(
            num_scalar_prefetch=5, grid=(n_slots // MOE_ROWS,),
            in_specs=[any_space, any_space, any_space, any_space,
                      pl.BlockSpec((MOE_ROWS, 1), lambda j, be, nb, *_: (jnp.minimum(j, nb[0] - 1), 0))],
            out_specs=any_space,
            scratch_shapes=[pltpu.VMEM((2, MOE_ROWS, SUBLANES, dw), F32), pltpu.VMEM((2, MOE_ROWS, SUBLANES, dw), F32),
                            pltpu.VMEM((MOE_ROWS, d), F32),
                            pltpu.VMEM((d, ff), BF16), pltpu.VMEM((d, ff), BF16), pltpu.VMEM((ff, d), BF16),
                            pltpu.VMEM((FFN_W_SLOTS, d // FFN_W_CHUNKS, ff), F32),
                            pltpu.VMEM((FFN_W_SLOTS, ff // FFN_W_CHUNKS, d), F32),
                            pltpu.SemaphoreType.DMA((2,)), pltpu.SemaphoreType.DMA((2,)),
                            pltpu.SemaphoreType.DMA((FFN_W_SLOTS,))]),
        out_shape=SDS((MOE_TOP_K * n_tok, SUBLANES, dw), F32), compiler_params=_cp("arbitrary"), name="moe_ffn",
    )(block_e, n_used_blocks, slot_tok, slot_dst, block_cnt, h, w_gate, w_up, w_down, slot_w.reshape(n_slots, 1))


def _moe_combine_kernel(x_ref, y0_ref, y1_ref, g_ref, *rest):
    o_ref = rest[-1]
    dw = y0_ref.shape[-1]
    for j in range(SUBLANES):
        cols = slice(j * dw, (j + 1) * dw)
        o_ref[0, :, cols] = x_ref[0, :, cols] + g_ref[0, :, cols] * (y0_ref[0, 0, :, j, :] + y1_ref[0, 0, :, j, :])
    if len(rest) == 2:
        x = o_ref[0]
        o_ref[0] = x * lax.rsqrt(jnp.mean(x * x, axis=-1, keepdims=True) + NORM_EPS) * rest[0][...]


def _moe_combine(x, y2, gate, norm_g=None, tm=256):
    bsz, seq, d = x.shape
    dw = d // SUBLANES
    y5 = y2.reshape(MOE_TOP_K, bsz, seq, SUBLANES, dw)
    tok = pl.BlockSpec((1, tm, d), lambda b, i: (b, i, 0))
    slab = lambda k: pl.BlockSpec((1, 1, tm, SUBLANES, dw), lambda b, i: (k, b, i, 0, 0))
    extra_specs = [] if norm_g is None else [pl.BlockSpec((1, d), lambda b, i: (0, 0))]
    extra_args = [] if norm_g is None else [norm_g.reshape(1, d)]
    return pl.pallas_call(
        _moe_combine_kernel, grid=(bsz, seq // tm),
        in_specs=[tok, slab(0), slab(1), pl.BlockSpec((1, 1, d), lambda b, i: (b, 0, 0))] + extra_specs,
        out_specs=tok, out_shape=SDS((bsz, seq, d), F32),
        compiler_params=_cp("parallel", "parallel"), name="moe_combine",
    )(x, y5, y5, gate.reshape(bsz, 1, d), *extra_args)


def _pack_plan(offs, n_cols):
    ng, wh = len(ATT_GROUPS), ATT_HEADS * ATT_HEAD
    plan, starts = [], []

    def dense(first, count):
        starts.append(len(plan))
        for c0 in range(0, count, PACK_COLS):
            plan.append([(first + c0, min(PACK_COLS, count - c0), 0)])

    dense(offs[0], A_MAIN)
    o_w = offs[0] + A_MAIN
    o_a = o_w + RWKV_DECAY_RANK
    o_g = o_a + RWKV_ICLR_RANK
    plan.append([(o_w, RWKV_DECAY_RANK, 0), (o_a, RWKV_ICLR_RANK, LOW_RANK_PAD), (o_g, RWKV_GATE_RANK, 2 * LOW_RANK_PAD)])
    for gi in range(ng):
        starts.append(len(plan))
        for t in range(3):
            for c0 in range(0, wh, PACK_COLS):
                plan.append([(offs[1] + (t * ng + gi) * wh + c0, PACK_COLS, 0)])
    dense(offs[2], offs[3] - offs[2])
    dense(offs[3], offs[4] - offs[3])
    dense(offs[4], n_cols - offs[4])
    return plan, starts


def _pack_w_in_kernel(layer, plan, w_hbm, o_ref, slab, sem):
    step = pl.program_id(0)

    def copies(j):
        return [pltpu.make_async_copy(w_hbm.at[layer, pl.ds(src, n)], slab.at[j % 2, pl.ds(dst, n)], sem.at[j % 2])
                for src, n, dst in plan[j]]

    def fetch(j):
        covered = sorted((dst, dst + n) for _, n, dst in plan[j])
        edge = 0
        for lo, hi in covered + [(PACK_COLS, PACK_COLS)]:
            if lo > edge:
                slab[j % 2, edge:lo, :] = jnp.zeros((lo - edge, slab.shape[2]), F32)
            edge = hi
        for cp in copies(j):
            cp.start()

    for j in range(len(plan)):
        @pl.when(step == j)
        def _():
            if j == 0:
                fetch(0)
            if j + 1 < len(plan):
                fetch(j + 1)
            for cp in copies(j):
                cp.wait()

    o_ref[...] = slab[step % 2].T.astype(BF16)


def _pack_w_in(w, layer, offs):
    _, d, n = w.shape
    plan, starts = _pack_plan(offs, n)
    packed = pl.pallas_call(
        functools.partial(_pack_w_in_kernel, layer, plan), grid=(len(plan),),
        in_specs=[pl.BlockSpec(memory_space=pl.ANY)],
        out_specs=pl.BlockSpec((d, PACK_COLS), lambda j: (0, j)),
        out_shape=SDS((d, len(plan) * PACK_COLS), BF16),
        scratch_shapes=[pltpu.VMEM((2, PACK_COLS, d), F32), pltpu.SemaphoreType.DMA((2,))],
        compiler_params=_cp("arbitrary"), name="pack_w_in",
    )(jnp.swapaxes(w, 1, 2))
    return packed, starts + [len(plan)]


def kernel(x, c, ada_w, ada_b, ada_table, norm1_g, norm2_g, w_in, rwkv_mu, rwkv_w0, rwkv_w2, rwkv_a0, rwkv_a2,
           rwkv_g2, rwkv_kk, rwkv_ka, rwkv_rk, rwkv_lnx_w, rwkv_lnx_b, pool_w, pool_scale, ssm_conv_w, ssm_conv_b,
           ssm_dt_bias, ssm_a_log, ssm_d, ssm_norm_w, gate_up, gate_b, w_branch, w_out, router_group_w,
           router_group_b, router_expert_w, router_expert_b, exp_w_gate, exp_w_up, exp_w_down, final_g):
    bsz, seq, d = x.shape
    m = bsz * seq
    a_cols = rwkv_mu.shape[1]
    b_cols = 3 * len(ATT_GROUPS) * ATT_HEADS * ATT_HEAD
    d_cols = SSM_INNER + SSM_CONV_DIM + SSM_HEADS
    offs = (0, a_cols, a_cols + b_cols, a_cols + b_cols + BRANCH_WIDTH, a_cols + b_cols + BRANCH_WIDTH + d_cols)
    cond = _cond(c, ada_w, ada_b)
    for l in range(DEPTH):
        shift1, scale1, gate1, shift2, scale2, gate2 = jnp.split(cond + ada_table[l], 6, axis=-1)
        dilations = tuple(dil for _, dil in ATT_GROUPS)
        strided = [dil for dil in dilations if dil > 1]
        h, *h_phases = _norm_mod_phases(x, norm1_g[l], scale1, shift1, strided)
        h = h.reshape(m, d)
        h_by_dilation = {dil: hp.reshape(m, d) for dil, hp in zip(strided, h_phases)}
        w_packed, blk = _pack_w_in(w_in, l, offs)
        proj = lambda a, seg, dtype: _matmul(a, w_packed, dtype, blk[seg], blk[seg + 1] - blk[seg], tn=PACK_COLS)
        pa = proj(h, 0, F32).reshape(bsz, seq, A_PAD)
        pbs = [proj(h_by_dilation.get(dil, h), 1 + gi, BF16).reshape(bsz, dil, seq // dil, -1)
               for gi, dil in enumerate(dilations)]
        pc = proj(h, 4, F32).reshape(bsz, seq, BRANCH_WIDTH)
        pd = proj(h, 5, F32).reshape(bsz, seq, D_PAD)
        pg = proj(h, 6, BF16)
        ya, ga = _rwkv_mixer(pa, rwkv_mu[l], rwkv_w0[l], rwkv_w2[l], rwkv_a0[l], rwkv_a2[l], rwkv_g2[l],
                             rwkv_kk[l], rwkv_ka[l], rwkv_rk[l], rwkv_lnx_w[l], rwkv_lnx_b[l])
        ob = _attention_mixer(pbs)
        oc = _pool_mixer(pc, pool_w[l], pool_scale[l])
        od = _mamba_mixer(pd, ssm_conv_w[l], ssm_conv_b[l], ssm_dt_bias[l], ssm_a_log[l], ssm_d[l], ssm_norm_w[l])
        merged = _merge(pg, ya.reshape(m, BRANCH_WIDTH), ga.reshape(m, BRANCH_WIDTH), ob, oc, od,
                        _cast_bf16(gate_up, l), gate_b[l], _cast_bf16(w_branch, l))
        w_out_l = _cast_bf16(w_out.reshape(DEPTH, 1, d, d), l).reshape(d, d)
        x = _matmul_resid(merged, w_out_l, x.reshape(m, d), gate1, seq).reshape(bsz, seq, d)
        h2, expert, weight, counts = _norm_route(x, norm2_g[l], scale2, shift2,
                                                 jnp.swapaxes(router_group_w, 1, 2)[l], router_group_b[l],
                                                 jnp.swapaxes(router_expert_w, 1, 2)[l], router_expert_b[l])
        y2 = _moe_apply(h2, expert, weight, counts, exp_w_gate, exp_w_up, exp_w_down, l)
        x = _moe_combine(x, y2, gate2, final_g if l == DEPTH - 1 else None)
    return x
```
